```python
import math
import jax
import jax.numpy as jnp
from jax import lax
import numpy as np

D_MODEL = 1024
BATCH = 8
SEQ = 2048
DEPTH = 4
DEC_BATCH = 32
DEC_SEQ = 1
PAST_LEN = 8192
PAGE_SIZE = 128

N_MIXERS = 3
PLE_DIM = 256
D_FF = 4 * D_MODEL
NORM_EPS = 1e-6
CONV_W = 4

SSD_EXPAND = 2
SSD_D_INNER = SSD_EXPAND * D_MODEL
SSD_HEAD_DIM = 64
SSD_HEADS = SSD_D_INNER // SSD_HEAD_DIM
SSD_GROUPS = 8
SSD_STATE = 128
SSD_CONV_DIM = SSD_D_INNER + 2 * SSD_GROUPS * SSD_STATE
SSD_IN_DIM = SSD_D_INNER + SSD_CONV_DIM + SSD_HEADS
SSD_CHUNK = 128

MOBA_HEADS = 8
MOBA_HEAD_DIM = D_MODEL // MOBA_HEADS
MOBA_BLOCK = 256
MOBA_TOPK = 3
MOBA_QBLOCK = 8

LRU_WIDTH = D_MODEL
LRU_BLOCKS = 8
LRU_BLOCK_W = LRU_WIDTH // LRU_BLOCKS
LRU_C = 8.0

N_SSD_LAYERS = (DEPTH + 2) // 3
N_MOBA_LAYERS = (DEPTH + 1) // 3
N_LRU_LAYERS = DEPTH // 3

N_PAGES = PAST_LEN // PAGE_SIZE
N_POOL_PAGES = (DEC_BATCH * N_PAGES * 5) // 4

kernel_name = 'ssd_moba_rglru_hybrid_step'


def rmsnorm(x, g):
    xf = x.astype(jnp.float32)
    var = jnp.mean(xf * xf, axis=-1, keepdims=True)
    return (xf * lax.rsqrt(var + NORM_EPS)).astype(x.dtype) * g


def causal_dwconv(x, buf, w, b):
    L = x.shape[1]
    xp = jnp.concatenate([buf, x], axis=1)
    y = sum(xp[:, k:k + L] * w[k] for k in range(CONV_W)) + b
    return y, xp[:, -(CONV_W - 1):]


def ssd_scan(x, dt, A, B, C, h0):
    b, L, H, P = x.shape
    G, N = B.shape[2], B.shape[3]
    R = H // G
    T = min(SSD_CHUNK, L)
    nc = -(-L // T)
    pad = nc * T - L
    xdt = x * dt[..., None]
    dA = dt * A
    if pad:
        padl = lambda a: jnp.pad(a, [(0, 0), (0, pad)] + [(0, 0)] * (a.ndim - 2))
        xdt, dA, B, C = padl(xdt), padl(dA), padl(B), padl(C)
    xdt = xdt.reshape(b, nc, T, G, R, P)
    dA = dA.reshape(b, nc, T, G, R)
    B = B.reshape(b, nc, T, G, N)
    C = C.reshape(b, nc, T, G, N)
    a_cum = jnp.cumsum(dA, axis=2)
    causal = jnp.tril(jnp.ones((T, T), bool))[:, :, None, None]
    seg = a_cum[:, :, :, None] - a_cum[:, :, None, :]
    decay_ts = jnp.where(causal, jnp.exp(jnp.where(causal, seg, 0.0)), 0.0)
    cb = jnp.einsum('bctgn,bcsgn->bctsg', C, B)
    y_diag = jnp.einsum('bctsg,bctsgr,bcsgrp->bctgrp', cb, decay_ts, xdt)
    decay_to_end = jnp.exp(a_cum[:, :, -1:] - a_cum)
    chunk_states = jnp.einsum('bcsgn,bcsgr,bcsgrp->bcgrpn', B, decay_to_end, xdt)
    chunk_decay = jnp.exp(a_cum[:, :, -1])

    def step(h, inp):
        st, dec = inp
        return h * dec[..., None, None] + st, h

    h_last, h_in = lax.scan(step, h0.reshape(b, G, R, P, N).astype(chunk_states.dtype),
                            (jnp.moveaxis(chunk_states, 1, 0), jnp.moveaxis(chunk_decay, 1, 0)))
    h_in = jnp.moveaxis(h_in, 0, 1)
    y_off = jnp.einsum('bctgn,bcgrpn,bctgr->bctgrp', C, h_in, jnp.exp(a_cum))
    y = (y_diag + y_off).reshape(b, nc * T, H, P)[:, :L]
    return y, h_last.reshape(b, H, P, N)


def ssd_mixer(xn, conv_buf, ssm_state, w_in, conv_w, conv_b, dt_bias, a_log, d_skip, norm_g, w_out):
    b, L, _ = xn.shape
    z, xbc, dt = jnp.split(xn @ w_in, [SSD_D_INNER, SSD_D_INNER + SSD_CONV_DIM], axis=-1)
    xbc, new_buf = causal_dwconv(xbc, conv_buf, conv_w, conv_b)
    xbc = jax.nn.silu(xbc)
    xs, B, C = jnp.split(xbc, [SSD_D_INNER, SSD_D_INNER + SSD_GROUPS * SSD_STATE], axis=-1)
    xs = xs.reshape(b, L, SSD_HEADS, SSD_HEAD_DIM)
    B = B.reshape(b, L, SSD_GROUPS, SSD_STATE)
    C = C.reshape(b, L, SSD_GROUPS, SSD_STATE)
    dt = jax.nn.softplus(dt + dt_bias)
    A = -jnp.exp(a_log)
    y, new_state = ssd_scan(xs, dt, A, B, C, ssm_state)
    y = y + xs * d_skip[:, None]
    y = y.reshape(b, L, SSD_D_INNER) * jax.nn.silu(z)
    y = rmsnorm(y.reshape(b, L, SSD_GROUPS, -1), norm_g.reshape(SSD_GROUPS, -1)).reshape(b, L, SSD_D_INNER)
    return y @ w_out, new_buf, new_state


def gather_pages(pool, li, table, new_pages, b_idx, lp, h_idx):
    n_new = new_pages.shape[1]
    if table is None:
        return new_pages[b_idx, jnp.clip(lp, 0, n_new - 1), :, h_idx]
    n_past = table.shape[1]
    new_rows = new_pages[b_idx, jnp.clip(lp - n_past, 0, n_new - 1), :, h_idx]
    past_rows = pool[li, table[b_idx, jnp.clip(lp, 0, n_past - 1)], :, h_idx]
    return jnp.where((lp < n_past)[..., None, None], past_rows, new_rows)


def moba_mixer(xn, pos0, w_qkv, w_o, k_pool, v_pool, li, table):
    b, L, _ = xn.shape
    H, HD = MOBA_HEADS, MOBA_HEAD_DIM
    q, k, v = jnp.split(xn @ w_qkv, 3, axis=-1)
    q = q.reshape(b, L, H, HD)
    k = k.reshape(b, L, H, HD)
    v = v.reshape(b, L, H, HD)
    ppb = MOBA_BLOCK // PAGE_SIZE
    n_new = -(-L // PAGE_SIZE)
    row_pad = ((0, 0), (0, n_new * PAGE_SIZE - L), (0, 0), (0, 0))
    k_new = jnp.pad(k, row_pad).reshape(b, n_new, PAGE_SIZE, H, HD)
    v_new = jnp.pad(v, row_pad).reshape(b, n_new, PAGE_SIZE, H, HD)
    page_sums = jnp.sum(k_new, axis=2, dtype=jnp.float32)
    n_past = 0
    if table is not None:
        n_past = table.shape[1]
        pool_sums = jnp.sum(k_pool[li], axis=1, dtype=jnp.float32)
        page_sums = jnp.concatenate([pool_sums[table], page_sums], axis=1)
    n_tot = n_past + n_new
    nb = -(-n_tot // ppb)
    page_sums = jnp.pad(page_sums, ((0, 0), (0, nb * ppb - n_tot), (0, 0), (0, 0)))
    block_mean = page_sums.reshape(b, nb, ppb, H, HD).sum(axis=2) / MOBA_BLOCK
    pos = pos0 + jnp.arange(L, dtype=jnp.int32)
    own = pos // MOBA_BLOCK
    gate = jnp.einsum('blhd,bnhd->blhn', q.astype(jnp.float32), block_mean)
    eligible = jnp.arange(nb)[None, :] < own[:, None]
    gate = jnp.where(eligible[None, :, None, :], gate, -jnp.inf)
    k_sel = min(MOBA_TOPK, nb)
    _, sel = lax.top_k(gate, k_sel)
    blocks = jnp.concatenate([sel.astype(jnp.int32), jnp.broadcast_to(own[None, :, None, None], (b, L, H, 1))], axis=-1)
    slot_ok = jnp.concatenate([jnp.arange(k_sel)[None, :] < own[:, None], jnp.ones((L, 1), bool)], axis=-1)
    n_slots = k_sel + 1
    qb = min(MOBA_QBLOCK, L)
    ng = -(-L // qb)
    q_pad = ng * qb - L

    def to_groups(a):
        a = jnp.pad(a, [(0, 0), (0, q_pad)] + [(0, 0)] * (a.ndim - 2))
        return jnp.moveaxis(a.reshape((b, ng, qb) + a.shape[2:]), 1, 0)

    pos_g = jnp.pad(pos, (0, q_pad)).reshape(ng, qb)
    ok_g = jnp.pad(slot_ok, ((0, q_pad), (0, 0)), constant_values=True).reshape(ng, qb, n_slots)
    b_idx = jnp.arange(b)[:, None, None, None, None]
    h_idx = jnp.arange(H)[None, None, :, None, None]
    scale = HD ** -0.5

    def attend(args):
        qg, blk, pg, okg = args
        lp = blk[..., None] * ppb + jnp.arange(ppb)
        kg = gather_pages(k_pool, li, table, k_new, b_idx, lp, h_idx).reshape(b, qb, H, n_slots, MOBA_BLOCK, HD)
        vg = gather_pages(v_pool, li, table, v_new, b_idx, lp, h_idx).reshape(b, qb, H, n_slots, MOBA_BLOCK, HD)
        key_pos = blk[..., None] * MOBA_BLOCK + jnp.arange(MOBA_BLOCK)
        mask = okg[None, :, None, :, None] & (key_pos <= pg[None, :, None, None, None])
        s = jnp.einsum('bqhd,bqhkld->bqhkl', qg, kg).astype(jnp.float32) * scale
        s = jnp.where(mask, s, -jnp.inf).reshape(b, qb, H, n_slots * MOBA_BLOCK)
        p = jax.nn.softmax(s, axis=-1).reshape(b, qb, H, n_slots, MOBA_BLOCK).astype(vg.dtype)
        return jnp.einsum('bqhkl,bqhkld->bqhd', p, vg)

    out = lax.map(attend, (to_groups(q), to_groups(blocks), pos_g, ok_g))
    out = jnp.moveaxis(out, 0, 1).reshape(b, ng * qb, H * HD)[:, :L]
    return out @ w_o, k, v


def rglru_mixer(xn, conv_buf, h0, w_in, conv_w, conv_b, w_a, b_a, w_x, b_x, lam, w_out):
    b, L, _ = xn.shape
    gate_br, x_br = jnp.split(xn @ w_in, 2, axis=-1)
    x_br, new_buf = causal_dwconv(x_br, conv_buf, conv_w, conv_b)
    xb = x_br.reshape(b, L, LRU_BLOCKS, LRU_BLOCK_W)
    r = jax.nn.sigmoid(jnp.einsum('blnc,ncd->blnd', xb, w_a) + b_a).reshape(b, L, LRU_WIDTH)
    i = jax.nn.sigmoid(jnp.einsum('blnc,ncd->blnd', xb, w_x) + b_x).reshape(b, L, LRU_WIDTH)
    log_a = -LRU_C * r * jax.nn.softplus(-lam)
    a = jnp.exp(log_a)
    u = jnp.sqrt(-jnp.expm1(2.0 * log_a)) * (i * x_br)

    def step(h, au):
        a_t, u_t = au
        h = a_t * h + u_t
        return h, h

    h_last, hs = lax.scan(step, h0.astype(u.dtype), (jnp.moveaxis(a, 1, 0), jnp.moveaxis(u, 1, 0)))
    hs = jnp.moveaxis(hs, 0, 1)
    y = jax.nn.gelu(gate_br) * hs
    return y @ w_out, new_buf, h_last


def trunk(h, p, pos0, ssd_conv, ssd_state, k_pool, v_pool, table, lru_conv, lru_state, W):
    b = h.shape[0]
    n_ssd_conv, n_ssd_state, n_k, n_v, n_lru_conv, n_lru_state = [], [], [], [], [], []
    for i in range(DEPTH):
        j = i // N_MIXERS
        kind = i % N_MIXERS
        xn = rmsnorm(h, W['norm_mix'][i])
        if kind == 0:
            buf = jnp.zeros((b, CONV_W - 1, SSD_CONV_DIM), h.dtype) if ssd_conv is None else ssd_conv[j]
            st = jnp.zeros((b, SSD_HEADS, SSD_HEAD_DIM, SSD_STATE), h.dtype) if ssd_state is None else ssd_state[j]
            out, buf, st = ssd_mixer(xn, buf, st, W['ssd_w_in'][j], W['ssd_conv_w'][j], W['ssd_conv_b'][j],
                                     W['ssd_dt_bias'][j], W['ssd_a_log'][j], W['ssd_d'][j], W['ssd_norm'][j],
                                     W['ssd_w_out'][j])
            n_ssd_conv.append(buf)
            n_ssd_state.append(st)
        elif kind == 1:
            out, kr, vr = moba_mixer(xn, pos0, W['moba_w_qkv'][j], W['moba_w_o'][j], k_pool, v_pool, j, table)
            n_k.append(kr)
            n_v.append(vr)
        else:
            buf = jnp.zeros((b, CONV_W - 1, LRU_WIDTH), h.dtype) if lru_conv is None else lru_conv[j]
            st = jnp.zeros((b, LRU_WIDTH), h.dtype) if lru_state is None else lru_state[j]
            out, buf, st = rglru_mixer(xn, buf, st, W['lru_w_in'][j], W['lru_conv_w'][j], W['lru_conv_b'][j],
                                       W['lru_w_a'][j], W['lru_b_a'][j], W['lru_w_x'][j], W['lru_b_x'][j],
                                       W['lru_lambda'][j], W['lru_w_out'][j])
            n_lru_conv.append(buf)
            n_lru_state.append(st)
        h = h + out
        hn = rmsnorm(h, W['norm_mlp'][i])
        h = h + jnp.square(jax.nn.relu(hn @ W['mlp_w_up'][i])) @ W['mlp_w_down'][i]
        gate = jax.nn.sigmoid(rmsnorm(h, W['norm_ple'][i]) @ W['ple_w_gate'][i])
        h = h + (p[i] @ W['ple_w_proj'][i]) * gate
    y = rmsnorm(h, W['norm_final'])
    return (y, jnp.stack(n_ssd_conv), jnp.stack(n_ssd_state), jnp.stack(n_k), jnp.stack(n_v),
            jnp.stack(n_lru_conv), jnp.stack(n_lru_state))


def setup_inputs(seed: int = 0) -> dict:
    key = jax.random.key(seed)
    keys = list(jax.random.split(key, 48))

    def normal(shape, scale=1.0):
        return scale * jax.random.normal(keys.pop(), shape, jnp.float32)

    def gain(shape):
        return 1.0 + normal(shape, 0.02)

    def uniform(shape, lo, hi):
        return jax.random.uniform(keys.pop(), shape, jnp.float32, lo, hi)

    NS, NM, NL = N_SSD_LAYERS, N_MOBA_LAYERS, N_LRU_LAYERS
    perm = jax.random.permutation(keys.pop(), N_POOL_PAGES)
    page_table = perm[:DEC_BATCH * N_PAGES].reshape(DEC_BATCH, N_PAGES).astype(jnp.int32)
    dt0 = jnp.exp(uniform((NS, SSD_HEADS), math.log(1e-3), math.log(1e-1)))
    a_lru = uniform((NL, LRU_WIDTH), 0.9, 0.999) ** (1.0 / LRU_C)
    return {
        'x_prompt': normal((BATCH, SEQ, D_MODEL)),
        'x_sample': normal((DEC_BATCH, DEC_SEQ, D_MODEL)),
        'state_ssd_conv': normal((NS, DEC_BATCH, CONV_W - 1, SSD_CONV_DIM)),
        'state_ssd': normal((NS, DEC_BATCH, SSD_HEADS, SSD_HEAD_DIM, SSD_STATE), 0.1),
        'cache_k': normal((NM, N_POOL_PAGES, PAGE_SIZE, MOBA_HEADS, MOBA_HEAD_DIM)),
        'cache_v': normal((NM, N_POOL_PAGES, PAGE_SIZE, MOBA_HEADS, MOBA_HEAD_DIM)),
        'page_table': page_table,
        'state_lru_conv': normal((NL, DEC_BATCH, CONV_W - 1, LRU_WIDTH)),
        'state_lru': normal((NL, DEC_BATCH, LRU_WIDTH), 0.5),
        'p_prompt': normal((DEPTH, BATCH, SEQ, PLE_DIM)),
        'p_sample': normal((DEPTH, DEC_BATCH, DEC_SEQ, PLE_DIM)),
        'norm_mix': gain((DEPTH, D_MODEL)),
        'norm_mlp': gain((DEPTH, D_MODEL)),
        'norm_ple': gain((DEPTH, D_MODEL)),
        'norm_final': gain((D_MODEL,)),
        'ssd_w_in': normal((NS, D_MODEL, SSD_IN_DIM), D_MODEL ** -0.5),
        'ssd_conv_w': normal((NS, CONV_W, SSD_CONV_DIM), CONV_W ** -0.5),
        'ssd_conv_b': normal((NS, SSD_CONV_DIM), 0.02),
        'ssd_dt_bias': dt0 + jnp.log(-jnp.expm1(-dt0)),
        'ssd_a_log': jnp.log(uniform((NS, SSD_HEADS), 1.0, 16.0)),
        'ssd_d': gain((NS, SSD_HEADS)),
        'ssd_norm': gain((NS, SSD_D_INNER)),
        'ssd_w_out': normal((NS, SSD_D_INNER, D_MODEL), SSD_D_INNER ** -0.5),
        'moba_w_qkv': normal((NM, D_MODEL, 3 * MOBA_HEADS * MOBA_HEAD_DIM), D_MODEL ** -0.5),
        'moba_w_o': normal((NM, MOBA_HEADS * MOBA_HEAD_DIM, D_MODEL), (MOBA_HEADS * MOBA_HEAD_DIM) ** -0.5),
        'lru_w_in': normal((NL, D_MODEL, 2 * LRU_WIDTH), D_MODEL ** -0.5),
        'lru_conv_w': normal((NL, CONV_W, LRU_WIDTH), CONV_W ** -0.5),
        'lru_conv_b': normal((NL, LRU_WIDTH), 0.02),
        'lru_w_a': normal((NL, LRU_BLOCKS, LRU_BLOCK_W, LRU_BLOCK_W), LRU_BLOCK_W ** -0.5),
        'lru_b_a': normal((NL, LRU_BLOCKS, LRU_BLOCK_W), 0.02),
        'lru_w_x': normal((NL, LRU_BLOCKS, LRU_BLOCK_W, LRU_BLOCK_W), LRU_BLOCK_W ** -0.5),
        'lru_b_x': normal((NL, LRU_BLOCKS, LRU_BLOCK_W), 0.02),
        'lru_lambda': jnp.log(a_lru) - jnp.log1p(-a_lru),
        'lru_w_out': normal((NL, LRU_WIDTH, D_MODEL), LRU_WIDTH ** -0.5),
        'mlp_w_up': normal((DEPTH, D_MODEL, D_FF), D_MODEL ** -0.5),
        'mlp_w_down': normal((DEPTH, D_FF, D_MODEL), D_FF ** -0.5),
        'ple_w_proj': normal((DEPTH, PLE_DIM, D_MODEL), PLE_DIM ** -0.5),
        'ple_w_gate': normal((DEPTH, D_MODEL, D_MODEL), D_MODEL ** -0.5),
    }


def reference(x_prompt, x_sample, state_ssd_conv, state_ssd, cache_k, cache_v, page_table, state_lru_conv,
              state_lru, p_prompt, p_sample, norm_mix, norm_mlp, norm_ple, norm_final, ssd_w_in, ssd_conv_w,
              ssd_conv_b, ssd_dt_bias, ssd_a_log, ssd_d, ssd_norm, ssd_w_out, moba_w_qkv, moba_w_o, lru_w_in,
              lru_conv_w, lru_conv_b, lru_w_a, lru_b_a, lru_w_x, lru_b_x, lru_lambda, lru_w_out, mlp_w_up,
              mlp_w_down, ple_w_proj, ple_w_gate):
    W = {
        'norm_mix': norm_mix, 'norm_mlp': norm_mlp, 'norm_ple': norm_ple, 'norm_final': norm_final,
        'ssd_w_in': ssd_w_in, 'ssd_conv_w': ssd_conv_w, 'ssd_conv_b': ssd_conv_b, 'ssd_dt_bias': ssd_dt_bias,
        'ssd_a_log': ssd_a_log, 'ssd_d': ssd_d, 'ssd_norm': ssd_norm, 'ssd_w_out': ssd_w_out,
        'moba_w_qkv': moba_w_qkv, 'moba_w_o': moba_w_o,
        'lru_w_in': lru_w_in, 'lru_conv_w': lru_conv_w, 'lru_conv_b': lru_conv_b, 'lru_w_a': lru_w_a,
        'lru_b_a': lru_b_a, 'lru_w_x': lru_w_x, 'lru_b_x': lru_b_x, 'lru_lambda': lru_lambda,
        'lru_w_out': lru_w_out, 'mlp_w_up': mlp_w_up, 'mlp_w_down': mlp_w_down,
        'ple_w_proj': ple_w_proj, 'ple_w_gate': ple_w_gate,
    }
    (y_prompt, ssd_conv_p, ssd_state_p, k_p, v_p, lru_conv_p, lru_state_p) = trunk(
        x_prompt, p_prompt, 0, None, None, None, None, None, None, None, W)
    pos0 = page_table.shape[1] * PAGE_SIZE
    (y_sample, ssd_conv_s, ssd_state_s, k_s, v_s, lru_conv_s, lru_state_s) = trunk(
        x_sample, p_sample, pos0, state_ssd_conv, state_ssd, cache_k, cache_v, page_table,
        state_lru_conv, state_lru, W)
    return (y_prompt, y_sample, ssd_conv_p, ssd_state_p, k_p, v_p, lru_conv_p, lru_state_p,
            ssd_conv_s, ssd_state_s, k_s, v_s, lru_conv_s, lru_state_s)
```

```python
import functools
import math

import jax
import jax.numpy as jnp
from jax import lax
from jax.experimental import pallas as pl
from jax.experimental.pallas import tpu as pltpu

D_MODEL = 1024
DEPTH = 4
N_MIXERS = 3
PLE_DIM = 256
D_FF = 4 * D_MODEL
NORM_EPS = 1e-6
CONV_W = 4
PAGE_SIZE = 128

SSD_D_INNER = 2 * D_MODEL
SSD_HEAD_DIM = 64
SSD_HEADS = SSD_D_INNER // SSD_HEAD_DIM
SSD_GROUPS = 8
SSD_HEADS_PER_GROUP = SSD_HEADS // SSD_GROUPS
SSD_STATE = 128
SSD_BC_DIM = SSD_GROUPS * SSD_STATE
SSD_CONV_DIM = SSD_D_INNER + 2 * SSD_BC_DIM
SSD_GROUP_W = SSD_D_INNER // SSD_GROUPS
SSD_CHUNK = 128

MOBA_HEADS = 8
MOBA_HEAD_DIM = D_MODEL // MOBA_HEADS
MOBA_BLOCK = 256
MOBA_TOPK = 3
PAGES_PER_BLOCK = MOBA_BLOCK // PAGE_SIZE

LRU_WIDTH = D_MODEL
LRU_BLOCKS = 8
LRU_BLOCK_W = LRU_WIDTH // LRU_BLOCKS
LRU_C = 8.0
LRU_CHUNK = 256

LANES = 128
SUBLANES = 8
VMEM_LIMIT = 56 * 1024 * 1024
NEG_BIG = -1e30

F32 = jnp.float32
BF16 = jnp.bfloat16


def _cparams(*sem):
    return pltpu.CompilerParams(dimension_semantics=sem, vmem_limit_bytes=VMEM_LIMIT)


def _rms(x, g):
    var = jnp.mean(x * x, axis=-1, keepdims=True)
    return (x * lax.rsqrt(var + NORM_EPS)) * g


def _sigmoid(x):
    return 1.0 / (1.0 + jnp.exp(-x))


def _silu(x):
    return x * _sigmoid(x)


def _softplus(x):
    return jnp.maximum(x, 0.0) + jnp.log1p(jnp.exp(-jnp.abs(x)))


def _gelu_tanh(x):
    return 0.5 * x * (1.0 + jnp.tanh(math.sqrt(2.0 / math.pi) * (x + 0.044715 * (x * x * x))))


def _split3(x):
    hi = x.astype(BF16)
    r1 = x - hi.astype(F32)
    mid = r1.astype(BF16)
    lo = (r1 - mid.astype(F32)).astype(BF16)
    return hi, mid, lo


def _dot(a, b):
    return jnp.dot(a, b, preferred_element_type=F32)


def _dot_nt(a, b):
    return lax.dot_general(a, b, (((1,), (1,)), ((), ())), preferred_element_type=F32)


def _dot3_nt(a, b):
    ah, am, _ = _split3(a)
    bh, bm, _ = _split3(b)
    return _dot_nt(ah, bh) + (_dot_nt(ah, bm) + _dot_nt(am, bh))


def _norm_linear_kernel(x_ref, g_ref, w_ref, o_ref, xn_ref):
    @pl.when(pl.program_id(1) == 0)
    def _():
        xn_ref[...] = _rms(x_ref[...], g_ref[...]).astype(BF16)

    o_ref[...] = _dot(xn_ref[...], w_ref[...]).astype(o_ref.dtype)


def norm_linear(x, g, w, *, tm, tn, out_dtype=F32):
    m, d = x.shape
    n = w.shape[1]
    return pl.pallas_call(
        _norm_linear_kernel,
        grid=(m // tm, n // tn),
        in_specs=[pl.BlockSpec((tm, d), lambda i, j: (i, 0)),
                  pl.BlockSpec((1, d), lambda i, j: (0, 0)),
                  pl.BlockSpec((d, tn), lambda i, j: (0, j))],
        out_specs=pl.BlockSpec((tm, tn), lambda i, j: (i, j)),
        out_shape=jax.ShapeDtypeStruct((m, n), out_dtype),
        scratch_shapes=[pltpu.VMEM((tm, d), BF16)],
        compiler_params=_cparams("parallel", "arbitrary"),
        name="norm_linear",
    )(x, g.reshape(1, d), w)


def _linear_res_kernel(x_ref, w_ref, r_ref, o_ref):
    o_ref[...] = r_ref[...] + _dot(x_ref[...].astype(BF16), w_ref[...])


def linear_residual(x, w, res, *, tm):
    m, k = x.shape
    n = w.shape[1]
    return pl.pallas_call(
        _linear_res_kernel,
        grid=(m // tm,),
        in_specs=[pl.BlockSpec((tm, k), lambda i: (i, 0)),
                  pl.BlockSpec((k, n), lambda i: (0, 0)),
                  pl.BlockSpec((tm, n), lambda i: (i, 0))],
        out_specs=pl.BlockSpec((tm, n), lambda i: (i, 0)),
        out_shape=jax.ShapeDtypeStruct((m, n), F32),
        compiler_params=_cparams("parallel"),
        name="linear_residual",
    )(x, w, res)


def _mlp_ple_kernel(h_ref, gm_ref, wu_ref, wd_ref, p_ref, gp_ref, wg_ref, wp_ref, gf_ref, o_ref,
                    xn_ref, acc_ref, *, final):
    f = pl.program_id(1)

    @pl.when(f == 0)
    def _():
        xn_ref[...] = _rms(h_ref[...], gm_ref[...]).astype(BF16)
        acc_ref[...] = jnp.zeros_like(acc_ref)

    hid = jnp.square(jnp.maximum(_dot(xn_ref[...], wu_ref[...]), 0.0))
    acc_ref[...] += _dot(hid.astype(BF16), wd_ref[...])

    @pl.when(f == pl.num_programs(1) - 1)
    def _():
        h1 = h_ref[...] + acc_ref[...]
        gate = _sigmoid(_dot(_rms(h1, gp_ref[...]).astype(BF16), wg_ref[...]))
        h2 = h1 + _dot(p_ref[...].astype(BF16), wp_ref[...]) * gate
        o_ref[...] = _rms(h2, gf_ref[...]) if final else h2


def mlp_ple(h, g_mlp, w_up, w_down, p, g_ple, w_gate, w_proj, g_final, *, tm, tf, final):
    m, d = h.shape
    ff = w_up.shape[1]
    pd = p.shape[1]
    row = lambda i, f: (i, 0)
    fixed = lambda i, f: (0, 0)
    return pl.pallas_call(
        functools.partial(_mlp_ple_kernel, final=final),
        grid=(m // tm, ff // tf),
        in_specs=[pl.BlockSpec((tm, d), row),
                  pl.BlockSpec((1, d), fixed),
                  pl.BlockSpec((d, tf), lambda i, f: (0, f)),
                  pl.BlockSpec((tf, d), lambda i, f: (f, 0)),
                  pl.BlockSpec((tm, pd), row),
                  pl.BlockSpec((1, d), fixed),
                  pl.BlockSpec((d, d), fixed),
                  pl.BlockSpec((pd, d), fixed),
                  pl.BlockSpec((1, d), fixed)],
        out_specs=pl.BlockSpec((tm, d), row),
        out_shape=jax.ShapeDtypeStruct((m, d), F32),
        scratch_shapes=[pltpu.VMEM((tm, d), BF16), pltpu.VMEM((tm, d), F32)],
        compiler_params=_cparams("parallel", "arbitrary"),
        name="mlp_ple",
    )(h, g_mlp.reshape(1, d), w_up, w_down, p, g_ple.reshape(1, d), w_gate, w_proj, g_final.reshape(1, d))


def _ssd_head_select():
    rows = jnp.arange(LANES)[:, None]
    cols = jnp.arange(SSD_HEADS * LANES)[None, :] // LANES
    return ((rows % SSD_HEADS == cols) & (rows < 3 * SSD_HEADS)).astype(BF16)


def _ssd_chunk_kernel(z_ref, xbc_ref, dt_ref, cw_ref, cb_ref, dtb_ref, alog_ref, dskip_ref, ng_ref, sel_ref,
                      y_ref, conv_ref, state_ref, xp_ref, act_ref, acol_ref, st_ref):
    c = pl.program_id(1)
    t = SSD_CHUNK
    gw = SSD_GROUP_W
    hpg = SSD_HEADS_PER_GROUP

    @pl.when(c == 0)
    def _():
        xp_ref[0:SUBLANES, :] = jnp.zeros((SUBLANES, SSD_CONV_DIM), F32)
        st_ref[...] = jnp.zeros_like(st_ref)

    xp_ref[SUBLANES:SUBLANES + t, :] = xbc_ref[...]
    slab = 512
    for s in range(SSD_CONV_DIM // slab):
        cols = slice(s * slab, (s + 1) * slab)
        acc = xp_ref[SUBLANES - 3:SUBLANES - 3 + t, cols] * cw_ref[0:1, cols]
        for k in range(1, CONV_W):
            acc = acc + xp_ref[SUBLANES - 3 + k:SUBLANES - 3 + k + t, cols] * cw_ref[k:k + 1, cols]
        act_ref[:, cols] = _silu(acc + cb_ref[:, cols])
    tail = xbc_ref[t - SUBLANES:t, :]
    xp_ref[0:SUBLANES, :] = tail
    conv_ref[0] = tail

    dtv = _softplus(dt_ref[...] + dtb_ref[...])
    da = dtv * (-jnp.exp(alog_ref[...]))
    row = lax.broadcasted_iota(jnp.int32, (t, t), 0)
    col = lax.broadcasted_iota(jnp.int32, (t, t), 1)
    causal = row >= col
    tril = causal.astype(BF16)
    d_hi, d_mid, d_lo = _split3(da)
    a_cum = _dot(tril, d_hi) + (_dot(tril, d_mid) + _dot(tril, d_lo))
    a_cum_t = a_cum.T
    dt_t = dtv.T

    lane = lax.broadcasted_iota(jnp.int32, (t, LANES), 1)
    a_hi, a_mid, a_lo = _split3(jnp.where(lane < SSD_HEADS, a_cum, 0.0))
    packed = (a_hi.astype(F32) + pltpu.roll(a_mid.astype(F32), SSD_HEADS, 1)
              + pltpu.roll(a_lo.astype(F32), 2 * SSD_HEADS, 1)).astype(BF16)
    acol_ref[...] = _dot(packed, sel_ref[...])

    lane_g = lax.broadcasted_iota(jnp.int32, (1, gw), 1) // SSD_HEAD_DIM
    for g in range(SSD_GROUPS):
        xs_g = act_ref[:, g * gw:(g + 1) * gw]
        b_g = act_ref[:, SSD_D_INNER + g * SSD_STATE:SSD_D_INNER + (g + 1) * SSD_STATE]
        c_g = act_ref[:, SSD_D_INNER + SSD_BC_DIM + g * SSD_STATE:
                      SSD_D_INNER + SSD_BC_DIM + (g + 1) * SSD_STATE]
        cb = _dot_nt(c_g.astype(BF16), b_g.astype(BF16))
        b_gt = b_g.T
        st_g = st_ref[g]
        m_parts, d_parts, w_parts, x_diag, s_diag = [], [], [], [], []
        dec_g = jnp.zeros((1, gw), F32)
        for r in range(hpg):
            h = g * hpg + r
            acol = acol_ref[:, h * LANES:(h + 1) * LANES]
            arow = a_cum_t[h:h + 1, :]
            dtrow = dt_t[h:h + 1, :]
            aend = acol[t - 1:t, :]
            decay = jnp.where(causal, jnp.exp(jnp.where(causal, acol - arow, 0.0)), 0.0)
            m_parts.append((cb * decay * dtrow).astype(BF16))
            d_parts.append((c_g * jnp.exp(acol)).astype(BF16))
            w_parts.append((b_gt * (dtrow * jnp.exp(aend - arow))).astype(BF16))
            in_head = lane_g == r
            x_diag.append(jnp.where(in_head, xs_g, 0.0).astype(BF16))
            s_diag.append(jnp.where(in_head, st_g, 0.0).astype(BF16))
            dec_g = jnp.where(in_head, jnp.exp(jnp.concatenate([aend, aend], axis=1)), dec_g)
        x_bd = jnp.concatenate(x_diag, axis=0)
        lhs = jnp.concatenate(m_parts + d_parts, axis=1)
        rhs = jnp.concatenate([x_bd] + s_diag, axis=0)
        y_g = _dot(lhs, rhs) + xs_g * dskip_ref[:, g * gw:(g + 1) * gw]
        st_ref[g] = st_g * dec_g + _dot(jnp.concatenate(w_parts, axis=1), x_bd)
        yz = y_g * _silu(z_ref[:, g * gw:(g + 1) * gw])
        y_ref[:, g * gw:(g + 1) * gw] = _rms(yz, ng_ref[:, g * gw:(g + 1) * gw]).astype(y_ref.dtype)

    @pl.when(c == pl.num_programs(1) - 1)
    def _():
        for g in range(SSD_GROUPS):
            state_ref[0, g * hpg:(g + 1) * hpg] = st_ref[g].T.reshape(hpg, SSD_HEAD_DIM, SSD_STATE)


def ssd_prompt(z, xbc, dt, conv_w, conv_b, dt_bias, a_log, d_skip, norm_g, batch, seq):
    t = SSD_CHUNK
    nc = seq // t
    pad = LANES - SSD_HEADS
    rowblk = lambda b, c: (b * nc + c, 0)
    fixed = lambda b, c: (0, 0)
    y, conv_tail, state = pl.pallas_call(
        _ssd_chunk_kernel,
        grid=(batch, nc),
        in_specs=[pl.BlockSpec((t, SSD_D_INNER), rowblk),
                  pl.BlockSpec((t, SSD_CONV_DIM), rowblk),
                  pl.BlockSpec((t, LANES), rowblk),
                  pl.BlockSpec((CONV_W, SSD_CONV_DIM), fixed),
                  pl.BlockSpec((1, SSD_CONV_DIM), fixed),
                  pl.BlockSpec((1, LANES), fixed),
                  pl.BlockSpec((1, LANES), fixed),
                  pl.BlockSpec((1, SSD_D_INNER), fixed),
                  pl.BlockSpec((1, SSD_D_INNER), fixed),
                  pl.BlockSpec((LANES, SSD_HEADS * LANES), fixed)],
        out_specs=[pl.BlockSpec((t, SSD_D_INNER), rowblk),
                   pl.BlockSpec((1, SUBLANES, SSD_CONV_DIM), lambda b, c: (b, 0, 0)),
                   pl.BlockSpec((1, SSD_HEADS, SSD_HEAD_DIM, SSD_STATE), lambda b, c: (b, 0, 0, 0))],
        out_shape=[jax.ShapeDtypeStruct((batch * seq, SSD_D_INNER), BF16),
                   jax.ShapeDtypeStruct((batch, SUBLANES, SSD_CONV_DIM), F32),
                   jax.ShapeDtypeStruct((batch, SSD_HEADS, SSD_HEAD_DIM, SSD_STATE), F32)],
        scratch_shapes=[pltpu.VMEM((SUBLANES + t, SSD_CONV_DIM), F32),
                        pltpu.VMEM((t, SSD_CONV_DIM), F32),
                        pltpu.VMEM((t, SSD_HEADS * LANES), F32),
                        pltpu.VMEM((SSD_GROUPS, SSD_STATE, SSD_GROUP_W), F32)],
        compiler_params=_cparams("parallel", "arbitrary"),
        name="ssd_prompt",
    )(z, xbc, dt, conv_w, conv_b.reshape(1, -1), jnp.pad(dt_bias, (0, pad)).reshape(1, LANES),
      jnp.pad(a_log, (0, pad)).reshape(1, LANES), jnp.repeat(d_skip, SSD_HEAD_DIM).reshape(1, -1),
      norm_g.reshape(1, -1), _ssd_head_select())
    return y, conv_tail[:, SUBLANES - (CONV_W - 1):, :], state


def _moba_prompt_kernel(q_ref, k_ref, v_ref, o_ref, kmean_ref, vt_ref, sel_ref):
    qi = pl.program_id(2)
    blk = MOBA_BLOCK
    seq = k_ref.shape[0]
    nb = seq // blk

    @pl.when(qi == 0)
    def _():
        kmean_ref[...] = jnp.zeros_like(kmean_ref)
        for j in range(nb):
            kmean_ref[j:j + 1, :] = jnp.sum(k_ref[j * blk:(j + 1) * blk, :], axis=0, keepdims=True) * (1.0 / blk)
        for j in range(seq // LANES):
            vt_ref[:, j * LANES:(j + 1) * LANES] = v_ref[j * LANES:(j + 1) * LANES, :].T.astype(BF16)

    q = q_ref[...]
    nbp = kmean_ref.shape[0]
    gate = _dot3_nt(kmean_ref[...], q)
    sub = lax.broadcasted_iota(jnp.int32, (nbp, blk), 0)
    elig = sub < qi
    gm = jnp.where(elig, gate, -jnp.inf)
    rank = jnp.zeros((nbp, blk), F32)
    for m in range(nb):
        gm_m = gm[m:m + 1, :]
        beats = (gm_m > gm) | ((gm_m == gm) & (sub > m))
        rank = rank + jnp.where(beats, 1.0, 0.0)
    chosen = (elig & (rank < MOBA_TOPK)) | (sub == qi)
    sel_ref[...] = jnp.where(chosen, 1.0, 0.0)

    qb = q.astype(BF16)
    krow = lax.broadcasted_iota(jnp.int32, (blk, blk), 0)
    qcol = lax.broadcasted_iota(jnp.int32, (blk, blk), 1)
    causal = krow <= qcol
    scale = MOBA_HEAD_DIM ** -0.5

    def body(j, carry):
        m, l, acc = carry
        start = pl.multiple_of(j * blk, blk)
        kj = k_ref[pl.ds(start, blk), :].astype(BF16)
        s = _dot_nt(kj, qb) * scale
        ok = (sel_ref[pl.ds(j, 1), :] > 0.0) & (causal | (j < qi))
        s = jnp.where(ok, s, NEG_BIG)
        m_new = jnp.maximum(m, jnp.max(s, axis=0, keepdims=True))
        alpha = jnp.exp(m - m_new)
        p = jnp.exp(s - m_new)
        l = alpha * l + jnp.sum(p, axis=0, keepdims=True)
        acc = alpha * acc + _dot(vt_ref[:, pl.ds(start, blk)], p.astype(BF16))
        return m_new, l, acc

    init = (jnp.full((1, blk), NEG_BIG, F32), jnp.zeros((1, blk), F32), jnp.zeros((MOBA_HEAD_DIM, blk), F32))
    _, l, acc = lax.fori_loop(0, qi + 1, body, init)
    o_ref[...] = (acc / l).T.astype(o_ref.dtype)


def moba_prompt(q, k, v, batch, seq):
    nb = seq // MOBA_BLOCK
    hd = MOBA_HEAD_DIM
    return pl.pallas_call(
        _moba_prompt_kernel,
        grid=(batch, MOBA_HEADS, nb),
        in_specs=[pl.BlockSpec((MOBA_BLOCK, hd), lambda b, h, i: (b * nb + i, h)),
                  pl.BlockSpec((seq, hd), lambda b, h, i: (b, h)),
                  pl.BlockSpec((seq, hd), lambda b, h, i: (b, h))],
        out_specs=pl.BlockSpec((MOBA_BLOCK, hd), lambda b, h, i: (b * nb + i, h)),
        out_shape=jax.ShapeDtypeStruct((batch * seq, D_MODEL), BF16),
        scratch_shapes=[pltpu.VMEM((-(-nb // SUBLANES) * SUBLANES, hd), F32),
                        pltpu.VMEM((hd, seq), BF16),
                        pltpu.VMEM((-(-nb // SUBLANES) * SUBLANES, MOBA_BLOCK), F32)],
        compiler_params=_cparams("parallel", "parallel", "arbitrary"),
        name="moba_prompt",
    )(q, k, v)


def _lru_gates(xc, wa, ba, wx, bx, lam):
    xb = xc.astype(BF16)
    r = _sigmoid(_dot(xb, wa) + ba)
    i = _sigmoid(_dot(xb, wx) + bx)
    log_a = (-LRU_C) * r * _softplus(-lam)
    a = jnp.exp(log_a)
    u = jnp.sqrt(-jnp.tanh(log_a) * (a * a + 1.0)) * (i * xc)
    return a, u


def _lru_chunk_kernel(gate_ref, x_ref, cw_ref, cb_ref, wa_ref, ba_ref, wx_ref, bx_ref, lam_ref,
                      y_ref, conv_ref, hlast_ref, xp_ref, a_ref, u_ref, h_ref):
    c = pl.program_id(1)
    t = LRU_CHUNK
    w = LRU_BLOCK_W

    @pl.when(c == 0)
    def _():
        xp_ref[0:SUBLANES, :] = jnp.zeros((SUBLANES, LRU_WIDTH), F32)
        h_ref[...] = jnp.zeros_like(h_ref)

    xp_ref[SUBLANES:SUBLANES + t, :] = x_ref[...]
    for n in range(LRU_BLOCKS):
        cols = slice(n * w, (n + 1) * w)
        acc = xp_ref[SUBLANES - 3:SUBLANES - 3 + t, cols] * cw_ref[0:1, cols]
        for k in range(1, CONV_W):
            acc = acc + xp_ref[SUBLANES - 3 + k:SUBLANES - 3 + k + t, cols] * cw_ref[k:k + 1, cols]
        xc = acc + cb_ref[:, cols]
        a, u = _lru_gates(xc, wa_ref[n], ba_ref[:, cols], wx_ref[n], bx_ref[:, cols], lam_ref[:, cols])
        a_ref[:, cols] = a
        u_ref[:, cols] = u
    tail = x_ref[t - SUBLANES:t, :]
    xp_ref[0:SUBLANES, :] = tail
    conv_ref[0] = tail

    srow = lax.broadcasted_iota(jnp.int32, (SUBLANES, LRU_WIDTH), 0)

    def tile(i, h):
        rows = pl.ds(pl.multiple_of(i * SUBLANES, SUBLANES), SUBLANES)
        a = a_ref[rows, :]
        u = u_ref[rows, :]
        for d in (1, 2, 4):
            keep = srow >= d
            a_sh = jnp.where(keep, pltpu.roll(a, d, 0), 1.0)
            u_sh = jnp.where(keep, pltpu.roll(u, d, 0), 0.0)
            u = a * u_sh + u
            a = a * a_sh
        hs = a * h + u
        u_ref[rows, :] = hs
        return jnp.broadcast_to(hs[SUBLANES - 1:SUBLANES, :], (SUBLANES, LRU_WIDTH))

    h = lax.fori_loop(0, t // SUBLANES, tile, h_ref[...])
    h_ref[...] = h
    hlast_ref[0] = h
    y_ref[...] = (_gelu_tanh(gate_ref[...]) * u_ref[...]).astype(y_ref.dtype)


def lru_prompt(gate_br, x_br, conv_w, conv_b, w_a, b_a, w_x, b_x, lam, batch, seq):
    t = LRU_CHUNK
    nc = seq // t
    wd = LRU_WIDTH
    rowblk = lambda b, c: (b * nc + c, 0)
    fixed = lambda b, c: (0, 0)
    fixed3 = lambda b, c: (0, 0, 0)
    perb = lambda b, c: (b, 0, 0)
    y, conv_tail, h_last = pl.pallas_call(
        _lru_chunk_kernel,
        grid=(batch, nc),
        in_specs=[pl.BlockSpec((t, wd), rowblk),
                  pl.BlockSpec((t, wd), rowblk),
                  pl.BlockSpec((CONV_W, wd), fixed),
                  pl.BlockSpec((1, wd), fixed),
                  pl.BlockSpec((LRU_BLOCKS, LRU_BLOCK_W, LRU_BLOCK_W), fixed3),
                  pl.BlockSpec((1, wd), fixed),
                  pl.BlockSpec((LRU_BLOCKS, LRU_BLOCK_W, LRU_BLOCK_W), fixed3),
                  pl.BlockSpec((1, wd), fixed),
                  pl.BlockSpec((1, wd), fixed)],
        out_specs=[pl.BlockSpec((t, wd), rowblk),
                   pl.BlockSpec((1, SUBLANES, wd), perb),
                   pl.BlockSpec((1, SUBLANES, wd), perb)],
        out_shape=[jax.ShapeDtypeStruct((batch * seq, wd), BF16),
                   jax.ShapeDtypeStruct((batch, SUBLANES, wd), F32),
                   jax.ShapeDtypeStruct((batch, SUBLANES, wd), F32)],
        scratch_shapes=[pltpu.VMEM((SUBLANES + t, wd), F32),
                        pltpu.VMEM((t, wd), F32),
                        pltpu.VMEM((t, wd), F32),
                        pltpu.VMEM((SUBLANES, wd), F32)],
        compiler_params=_cparams("parallel", "arbitrary"),
        name="lru_prompt",
    )(gate_br, x_br, conv_w, conv_b.reshape(1, wd), w_a.astype(BF16), b_a.reshape(1, wd),
      w_x.astype(BF16), b_x.reshape(1, wd), lam.reshape(1, wd))
    return y, conv_tail[:, SUBLANES - (CONV_W - 1):, :], h_last[:, 0, :]


def _ssd_step_kernel(z_ref, xbc_ref, dt_ref, cs_ref, st_ref, cw_ref, cb_ref, dtb_ref, alog_ref, dskip_ref, ng_ref,
                     y_ref, cso_ref, sto_ref):
    gw = SSD_GROUP_W
    hpg = SSD_HEADS_PER_GROUP
    x = xbc_ref[0]
    buf = cs_ref[0]
    conv = x * cw_ref[CONV_W - 1:CONV_W, :]
    for k in range(CONV_W - 1):
        conv = conv + buf[k:k + 1, :] * cw_ref[k:k + 1, :]
    act = _silu(conv + cb_ref[...])
    cso_ref[0, 0:CONV_W - 2, :] = buf[1:CONV_W - 1, :]
    cso_ref[0, CONV_W - 2:CONV_W - 1, :] = x

    dtv = _softplus(dt_ref[0] + dtb_ref[...])
    dec = jnp.exp(dtv * (-jnp.exp(alog_ref[...])))
    lane_g = lax.broadcasted_iota(jnp.int32, (1, gw), 1) // SSD_HEAD_DIM
    first_row = lax.broadcasted_iota(jnp.int32, (SUBLANES, gw), 0) == 0
    ys = []
    for g in range(SSD_GROUPS):
        xs_g = act[:, g * gw:(g + 1) * gw]
        b_g = act[:, SSD_D_INNER + g * SSD_STATE:SSD_D_INNER + (g + 1) * SSD_STATE]
        c_g = act[:, SSD_D_INNER + SSD_BC_DIM + g * SSD_STATE:SSD_D_INNER + SSD_BC_DIM + (g + 1) * SSD_STATE]
        dt_e = jnp.zeros((1, gw), F32)
        dec_rows = []
        for r in range(hpg):
            h = g * hpg + r
            dt_e = jnp.where(lane_g == r, dtv[:, h:h + 1], dt_e)
            dec_rows.append(jnp.broadcast_to(dec[:, h:h + 1], (SSD_HEAD_DIM, SSD_STATE)))
        xdt = xs_g * dt_e
        x8 = jnp.where(first_row, jnp.broadcast_to(xdt, (SUBLANES, gw)), 0.0).astype(BF16)
        b8 = jnp.broadcast_to(b_g, (SUBLANES, SSD_STATE)).astype(BF16)
        outer = lax.dot_general(x8, b8, (((0,), (0,)), ((), ())), preferred_element_type=F32)
        old = st_ref[0, g * hpg:(g + 1) * hpg].reshape(gw, SSD_STATE)
        new = old * jnp.concatenate(dec_rows, axis=0) + outer
        sto_ref[0, g * hpg:(g + 1) * hpg] = new.reshape(hpg, SSD_HEAD_DIM, SSD_STATE)
        c8 = jnp.broadcast_to(c_g, (SUBLANES, SSD_STATE)).astype(BF16)
        y_g = _dot_nt(c8, new.astype(BF16))[0:1, :] + xs_g * dskip_ref[:, g * gw:(g + 1) * gw]
        yz = y_g * _silu(z_ref[0][:, g * gw:(g + 1) * gw])
        ys.append(_rms(yz, ng_ref[:, g * gw:(g + 1) * gw]))
    y_ref[0] = jnp.concatenate(ys, axis=1)


def ssd_step(z, xbc, dt, conv_state, ssm_state, conv_w, conv_b, dt_bias, a_log, d_skip, norm_g):
    nb = z.shape[0]
    pad = LANES - SSD_HEADS
    per3 = lambda b: (b, 0, 0)
    fixed = lambda b: (0, 0)
    y, conv_new, state_new = pl.pallas_call(
        _ssd_step_kernel,
        grid=(nb,),
        in_specs=[pl.BlockSpec((1, 1, SSD_D_INNER), per3),
                  pl.BlockSpec((1, 1, SSD_CONV_DIM), per3),
                  pl.BlockSpec((1, 1, LANES), per3),
                  pl.BlockSpec((1, CONV_W - 1, SSD_CONV_DIM), per3),
                  pl.BlockSpec((1, SSD_HEADS, SSD_HEAD_DIM, SSD_STATE), lambda b: (b, 0, 0, 0)),
                  pl.BlockSpec((CONV_W, SSD_CONV_DIM), fixed),
                  pl.BlockSpec((1, SSD_CONV_DIM), fixed),
                  pl.BlockSpec((1, LANES), fixed),
                  pl.BlockSpec((1, LANES), fixed),
                  pl.BlockSpec((1, SSD_D_INNER), fixed),
                  pl.BlockSpec((1, SSD_D_INNER), fixed)],
        out_specs=[pl.BlockSpec((1, 1, SSD_D_INNER), per3),
                   pl.BlockSpec((1, CONV_W - 1, SSD_CONV_DIM), per3),
                   pl.BlockSpec((1, SSD_HEADS, SSD_HEAD_DIM, SSD_STATE), lambda b: (b, 0, 0, 0))],
        out_shape=[jax.ShapeDtypeStruct((nb, 1, SSD_D_INNER), F32),
                   jax.ShapeDtypeStruct((nb, CONV_W - 1, SSD_CONV_DIM), F32),
                   jax.ShapeDtypeStruct((nb, SSD_HEADS, SSD_HEAD_DIM, SSD_STATE), F32)],
        compiler_params=_cparams("parallel"),
        name="ssd_step",
    )(z.reshape(nb, 1, -1), xbc.reshape(nb, 1, -1), dt.reshape(nb, 1, -1), conv_state, ssm_state,
      conv_w, conv_b.reshape(1, -1), jnp.pad(dt_bias, (0, pad)).reshape(1, LANES),
      jnp.pad(a_log, (0, pad)).reshape(1, LANES), jnp.repeat(d_skip, SSD_HEAD_DIM).reshape(1, -1),
      norm_g.reshape(1, -1))
    return y.reshape(nb, SSD_D_INNER), conv_new, state_new


def _lru_step_kernel(gate_ref, x_ref, cs_ref, h0_ref, cw_ref, cb_ref, wa_ref, ba_ref, wx_ref, bx_ref, lam_ref,
                     y_ref, cso_ref, ho_ref):
    w = LRU_BLOCK_W
    x = x_ref[...]
    xc = x * cw_ref[CONV_W - 1:CONV_W, :] + cb_ref[...]
    for k in range(CONV_W - 1):
        xc = xc + cs_ref[k] * cw_ref[k:k + 1, :]
    for k in range(CONV_W - 2):
        cso_ref[k] = cs_ref[k + 1]
    cso_ref[CONV_W - 2] = x
    for n in range(LRU_BLOCKS):
        cols = slice(n * w, (n + 1) * w)
        a, u = _lru_gates(xc[:, cols], wa_ref[n], ba_ref[:, cols], wx_ref[n], bx_ref[:, cols], lam_ref[:, cols])
        h = a * h0_ref[:, cols] + u
        ho_ref[:, cols] = h
        y_ref[:, cols] = _gelu_tanh(gate_ref[:, cols]) * h


def lru_step(gate_br, x_br, conv_state, h0, conv_w, conv_b, w_a, b_a, w_x, b_x, lam):
    nb, wd = x_br.shape
    return pl.pallas_call(
        _lru_step_kernel,
        out_shape=[jax.ShapeDtypeStruct((nb, wd), F32),
                   jax.ShapeDtypeStruct((CONV_W - 1, nb, wd), F32),
                   jax.ShapeDtypeStruct((nb, wd), F32)],
        compiler_params=pltpu.CompilerParams(vmem_limit_bytes=VMEM_LIMIT),
        name="lru_step",
    )(gate_br, x_br, conv_state, h0, conv_w, conv_b.reshape(1, wd), w_a.astype(BF16), b_a.reshape(1, wd),
      w_x.astype(BF16), b_x.reshape(1, wd), lam.reshape(1, wd))


MOBA_SUM_PAGES = 16


def _moba_block_sums_kernel(tbl_ref, *refs):
    del tbl_ref
    pages, o_ref = refs[:-1], refs[-1]
    for k in range(len(pages) // PAGES_PER_BLOCK):
        s = jnp.sum(pages[PAGES_PER_BLOCK * k][...], axis=0, keepdims=True)
        for j in range(1, PAGES_PER_BLOCK):
            s = s + jnp.sum(pages[PAGES_PER_BLOCK * k + j][...], axis=0, keepdims=True)
        o_ref[0, k:k + 1, :] = s


def moba_block_sums(pool, layer, table):
    nb, n_pages = table.shape
    steps = n_pages // MOBA_SUM_PAGES
    blocks_per_step = MOBA_SUM_PAGES // PAGES_PER_BLOCK

    def page_spec(k):
        return pl.BlockSpec((None, None, PAGE_SIZE, D_MODEL),
                            lambda b, c, tbl: (layer, tbl[b * n_pages + c * MOBA_SUM_PAGES + k], 0, 0))

    return pl.pallas_call(
        _moba_block_sums_kernel,
        grid_spec=pltpu.PrefetchScalarGridSpec(
            num_scalar_prefetch=1,
            grid=(nb, steps),
            in_specs=[page_spec(k) for k in range(MOBA_SUM_PAGES)],
            out_specs=pl.BlockSpec((1, blocks_per_step, D_MODEL), lambda b, c, tbl: (b, c, 0))),
        out_shape=jax.ShapeDtypeStruct((nb, n_pages // PAGES_PER_BLOCK, D_MODEL), F32),
        compiler_params=_cparams("parallel", "arbitrary"),
        name="moba_block_sums",
    )(table.reshape(-1), *([pool] * MOBA_SUM_PAGES))


def _moba_gate_kernel(q_ref, bs_ref, sel_ref):
    n_blocks = bs_ref.shape[1]
    q = q_ref[0]
    head = lax.broadcasted_iota(jnp.int32, (MOBA_HEADS, D_MODEL), 0)
    lane_head = lax.broadcasted_iota(jnp.int32, (MOBA_HEADS, D_MODEL), 1) // MOBA_HEAD_DIM
    q_bd = jnp.where(head == lane_head, jnp.broadcast_to(q, (MOBA_HEADS, D_MODEL)), 0.0)
    means = bs_ref[0] * (1.0 / MOBA_BLOCK)
    means = jnp.concatenate([means, jnp.zeros((LANES - n_blocks, D_MODEL), F32)], axis=0)
    gate = _dot3_nt(q_bd, means)
    lane = lax.broadcasted_iota(jnp.int32, (MOBA_HEADS, LANES), 1)
    lane_f = lane.astype(F32)
    gate = jnp.where(lane < n_blocks, gate, -jnp.inf)
    out = jnp.zeros((MOBA_HEADS, LANES), F32)
    for k in range(MOBA_TOPK):
        best = jnp.max(gate, axis=1, keepdims=True)
        idx = jnp.min(jnp.where(gate == best, lane_f, float(LANES)), axis=1, keepdims=True)
        out = jnp.where(lane == k, idx, out)
        gate = jnp.where(lane_f == idx, -jnp.inf, gate)
    sel_ref[0] = out.astype(jnp.int32)


def moba_gate(q, block_sums):
    nb = q.shape[0]
    n_blocks = block_sums.shape[1]
    return pl.pallas_call(
        _moba_gate_kernel,
        grid=(nb,),
        in_specs=[pl.BlockSpec((1, 1, D_MODEL), lambda b: (b, 0, 0)),
                  pl.BlockSpec((1, n_blocks, D_MODEL), lambda b: (b, 0, 0))],
        out_specs=pl.BlockSpec((1, MOBA_HEADS, LANES), lambda b: (b, 0, 0)),
        out_shape=jax.ShapeDtypeStruct((nb, MOBA_HEADS, LANES), jnp.int32),
        compiler_params=_cparams("parallel"),
        name="moba_gate",
    )(q.reshape(nb, 1, D_MODEL), block_sums)


def _moba_decode_kernel(tbl_ref, sel_ref, q_ref, kn_ref, vn_ref, *refs):
    del tbl_ref, sel_ref
    n_pages = MOBA_TOPK * PAGES_PER_BLOCK
    k_pages, v_pages, o_ref = refs[:n_pages], refs[n_pages:2 * n_pages], refs[-1]
    scale = MOBA_HEAD_DIM ** -0.5
    q = q_ref[0]
    q8 = jnp.broadcast_to(q, (SUBLANES, MOBA_HEAD_DIM)).astype(BF16)
    k_cat = jnp.concatenate([r[...] for r in k_pages], axis=0).astype(BF16)
    v_cat = jnp.concatenate([r[...] for r in v_pages], axis=0).astype(BF16)
    s = _dot_nt(q8, k_cat) * scale
    s_new = jnp.sum(q * kn_ref[0], axis=1, keepdims=True) * scale
    m = jnp.maximum(jnp.max(s, axis=1, keepdims=True), s_new)
    p = jnp.exp(s - m)
    p_new = jnp.exp(s_new - m)
    denom = jnp.sum(p, axis=1, keepdims=True) + p_new
    o = (_dot(p.astype(BF16), v_cat) + p_new * vn_ref[0]) / denom
    o_ref[0] = o[0:1, :]


def moba_decode(q, k_new, v_new, k_pool, v_pool, layer, table, sel):
    nb = q.shape[0]
    n_pages = table.shape[1]
    hd = MOBA_HEAD_DIM
    tok = pl.BlockSpec((1, 1, hd), lambda b, h, tbl, sl: (b, 0, h))

    def page_spec(k, s):
        def index(b, h, tbl, sl):
            blk = sl[(b * MOBA_HEADS + h) * MOBA_TOPK + k]
            return (layer, tbl[b * n_pages + blk * PAGES_PER_BLOCK + s], 0, h)
        return pl.BlockSpec((None, None, PAGE_SIZE, hd), index)

    specs = [page_spec(k, s) for k in range(MOBA_TOPK) for s in range(PAGES_PER_BLOCK)]
    out = pl.pallas_call(
        _moba_decode_kernel,
        grid_spec=pltpu.PrefetchScalarGridSpec(
            num_scalar_prefetch=2,
            grid=(nb, MOBA_HEADS),
            in_specs=[tok, tok, tok] + specs + specs,
            out_specs=tok),
        out_shape=jax.ShapeDtypeStruct((nb, 1, D_MODEL), F32),
        compiler_params=_cparams("parallel", "arbitrary"),
        name="moba_decode",
    )(table.reshape(-1), sel[:, :, :MOBA_TOPK].reshape(-1),
      q.reshape(nb, 1, D_MODEL), k_new.reshape(nb, 1, D_MODEL), v_new.reshape(nb, 1, D_MODEL),
      *([k_pool] * len(specs)), *([v_pool] * len(specs)))
    return out.reshape(nb, D_MODEL)


def _prep_weights(W):
    bf = lambda a: a.astype(BF16)
    w_in = W["ssd_w_in"]
    zx = SSD_D_INNER + SSD_CONV_DIM
    return dict(
        ssd_w_z=bf(w_in[:, :, :SSD_D_INNER]),
        ssd_w_xbc=bf(w_in[:, :, SSD_D_INNER:zx]),
        ssd_w_dt=bf(jnp.pad(w_in[:, :, zx:], ((0, 0), (0, 0), (0, LANES - SSD_HEADS)))),
        ssd_w_out=bf(W["ssd_w_out"]),
        moba_w_q=bf(W["moba_w_qkv"][:, :, :D_MODEL]),
        moba_w_k=bf(W["moba_w_qkv"][:, :, D_MODEL:2 * D_MODEL]),
        moba_w_v=bf(W["moba_w_qkv"][:, :, 2 * D_MODEL:]),
        moba_w_o=bf(W["moba_w_o"]),
        lru_w_gate=bf(W["lru_w_in"][:, :, :LRU_WIDTH]),
        lru_w_x=bf(W["lru_w_in"][:, :, LRU_WIDTH:]),
        lru_w_out=bf(W["lru_w_out"]),
        mlp_w_up=bf(W["mlp_w_up"]), mlp_w_down=bf(W["mlp_w_down"]),
        ple_w_proj=bf(W["ple_w_proj"]), ple_w_gate=bf(W["ple_w_gate"]),
    )


def _trunk(h, p, W, Wb, tiles, ssd_fn, moba_fn, lru_fn):
    tm, tn, tf = tiles
    proj = lambda x, g, w: norm_linear(x, g, w, tm=tm, tn=min(tn, w.shape[1]))
    ssd_out, moba_out, lru_out = [], [], []
    for i in range(DEPTH):
        j, kind = i // N_MIXERS, i % N_MIXERS
        g = W["norm_mix"][i]
        if kind == 0:
            y, conv, state = ssd_fn(j, proj(h, g, Wb["ssd_w_z"][j]), proj(h, g, Wb["ssd_w_xbc"][j]),
                                    proj(h, g, Wb["ssd_w_dt"][j]))
            ssd_out.append((conv, state))
            h = linear_residual(y, Wb["ssd_w_out"][j], h, tm=tm)
        elif kind == 1:
            q, k, v = (proj(h, g, Wb[n][j]) for n in ("moba_w_q", "moba_w_k", "moba_w_v"))
            moba_out.append((k, v))
            h = linear_residual(moba_fn(j, q, k, v), Wb["moba_w_o"][j], h, tm=tm)
        else:
            y, conv, state = lru_fn(j, proj(h, g, Wb["lru_w_gate"][j]), proj(h, g, Wb["lru_w_x"][j]))
            lru_out.append((conv, state))
            h = linear_residual(y, Wb["lru_w_out"][j], h, tm=tm)
        h = mlp_ple(h, W["norm_mlp"][i], Wb["mlp_w_up"][i], Wb["mlp_w_down"][i], p[i], W["norm_ple"][i],
                    Wb["ple_w_gate"][i], Wb["ple_w_proj"][i], W["norm_final"], tm=tm, tf=tf,
                    final=(i == DEPTH - 1))
    stack = lambda pairs, n: jnp.stack([pr[n] for pr in pairs])
    return (h, stack(ssd_out, 0), stack(ssd_out, 1), stack(moba_out, 0), stack(moba_out, 1),
            stack(lru_out, 0), stack(lru_out, 1))


PROMPT_TILES = (512, 1024, 1024)
SAMPLE_TILES = (32, 1024, 1024)


def kernel(x_prompt, x_sample, state_ssd_conv, state_ssd, cache_k, cache_v, page_table, state_lru_conv,
           state_lru, p_prompt, p_sample, norm_mix, norm_mlp, norm_ple, norm_final, ssd_w_in, ssd_conv_w,
           ssd_conv_b, ssd_dt_bias, ssd_a_log, ssd_d, ssd_norm, ssd_w_out, moba_w_qkv, moba_w_o, lru_w_in,
           lru_conv_w, lru_conv_b, lru_w_a, lru_b_a, lru_w_x, lru_b_x, lru_lambda, lru_w_out, mlp_w_up,
           mlp_w_down, ple_w_proj, ple_w_gate):
    W = dict(norm_mix=norm_mix, norm_mlp=norm_mlp, norm_ple=norm_ple, norm_final=norm_final,
             ssd_w_in=ssd_w_in, ssd_w_out=ssd_w_out, moba_w_qkv=moba_w_qkv, moba_w_o=moba_w_o,
             lru_w_in=lru_w_in, lru_w_out=lru_w_out, mlp_w_up=mlp_w_up, mlp_w_down=mlp_w_down,
             ple_w_proj=ple_w_proj, ple_w_gate=ple_w_gate)
    Wb = _prep_weights(W)
    batch, seq, d = x_prompt.shape
    dec_batch, dec_seq, _ = x_sample.shape
    assert dec_seq == 1 and seq % MOBA_BLOCK == 0 and seq % LRU_CHUNK == 0

    ssd_args = lambda j: (ssd_conv_w[j], ssd_conv_b[j], ssd_dt_bias[j], ssd_a_log[j], ssd_d[j], ssd_norm[j])
    lru_args = lambda j: (lru_conv_w[j], lru_conv_b[j], lru_w_a[j], lru_b_a[j].reshape(-1), lru_w_x[j],
                          lru_b_x[j].reshape(-1), lru_lambda[j])

    out_p = _trunk(
        x_prompt.reshape(batch * seq, d), p_prompt.reshape(DEPTH, batch * seq, PLE_DIM), W, Wb, PROMPT_TILES,
        lambda j, z, xbc, dt: ssd_prompt(z, xbc, dt, *ssd_args(j), batch, seq),
        lambda j, q, k, v: moba_prompt(q, k, v, batch, seq),
        lambda j, gate, x: lru_prompt(gate, x, *lru_args(j), batch, seq))

    k_pool = cache_k.reshape(cache_k.shape[:3] + (D_MODEL,))
    v_pool = cache_v.reshape(cache_v.shape[:3] + (D_MODEL,))

    def moba_sample(j, q, k, v):
        sel = moba_gate(q, moba_block_sums(k_pool, j, page_table))
        return moba_decode(q, k, v, k_pool, v_pool, j, page_table, sel)

    def lru_sample(j, gate, x):
        y, conv, h = lru_step(gate, x, jnp.swapaxes(state_lru_conv[j], 0, 1), state_lru[j], *lru_args(j))
        return y, jnp.swapaxes(conv, 0, 1), h

    out_s = _trunk(
        x_sample.reshape(dec_batch, d), p_sample.reshape(DEPTH, dec_batch, PLE_DIM), W, Wb, SAMPLE_TILES,
        lambda j, z, xbc, dt: ssd_step(z, xbc, dt, state_ssd_conv[j], state_ssd[j], *ssd_args(j)),
        moba_sample, lru_sample)

    def shaped(out, b, s):
        y, ssd_conv, ssd_state, k, v, lru_conv, lru_state = out
        kv_shape = (-1, b, s, MOBA_HEADS, MOBA_HEAD_DIM)
        return (y.reshape(b, s, d), ssd_conv, ssd_state, k.reshape(kv_shape), v.reshape(kv_shape),
                lru_conv, lru_state)

    yp, *rest_p = shaped(out_p, batch, seq)
    ys, *rest_s = shaped(out_s, dec_batch, dec_seq)
    return (yp, ys, *rest_p, *rest_s)
```

```python
import functools
import math

import jax
import jax.numpy as jnp
from jax import lax
from jax.experimental import pallas as pl
from jax.experimental.pallas import tpu as pltpu

D_MODEL = 1024
DEPTH = 4
N_MIXERS = 3
PLE_DIM = 256
D_FF = 4 * D_MODEL
NORM_EPS = 1e-6
CONV_W = 4
PAGE_SIZE = 128

SSD_D_INNER = 2 * D_MODEL
SSD_HEAD_DIM = 64
SSD_HEADS = SSD_D_INNER // SSD_HEAD_DIM
SSD_GROUPS = 8
SSD_HEADS_PER_GROUP = SSD_HEADS // SSD_GROUPS
SSD_STATE = 128
SSD_BC_DIM = SSD_GROUPS * SSD_STATE
SSD_CONV_DIM = SSD_D_INNER + 2 * SSD_BC_DIM
SSD_GROUP_W = SSD_D_INNER // SSD_GROUPS
SSD_CHUNK = 128

MOBA_HEADS = 8
MOBA_HEAD_DIM = D_MODEL // MOBA_HEADS
MOBA_BLOCK = 256
MOBA_TOPK = 3
PAGES_PER_BLOCK = MOBA_BLOCK // PAGE_SIZE

LRU_WIDTH = D_MODEL
LRU_BLOCKS = 8
LRU_BLOCK_W = LRU_WIDTH // LRU_BLOCKS
LRU_C = 8.0
LRU_CHUNK = 256

LANES = 128
SUBLANES = 8
VMEM_LIMIT = 56 * 1024 * 1024
NEG_BIG = -1e30
LOG2E = 1.0 / math.log(2.0)
EXP2_CAP = 126.0

F32 = jnp.float32
BF16 = jnp.bfloat16


def _cparams(*sem):
    return pltpu.CompilerParams(dimension_semantics=sem, vmem_limit_bytes=VMEM_LIMIT)


def _rms(x, g):
    var = jnp.mean(x * x, axis=-1, keepdims=True)
    return (x * lax.rsqrt(var + NORM_EPS)) * g


def _sigmoid(x):
    return 0.5 + 0.5 * jnp.tanh(0.5 * x)


def _silu(x):
    hx = 0.5 * x
    return hx + hx * jnp.tanh(hx)


def _softplus(x):
    return jnp.maximum(x, 0.0) + jnp.log1p(jnp.exp(-jnp.abs(x)))


def _gelu_tanh(x):
    return 0.5 * x * (1.0 + jnp.tanh(math.sqrt(2.0 / math.pi) * (x + 0.044715 * (x * x * x))))


def _split3(x):
    hi = x.astype(BF16)
    r1 = x - hi.astype(F32)
    mid = r1.astype(BF16)
    lo = (r1 - mid.astype(F32)).astype(BF16)
    return hi, mid, lo


def _dot(a, b):
    return jnp.dot(a, b, preferred_element_type=F32)


def _dot_nt(a, b):
    return lax.dot_general(a, b, (((1,), (1,)), ((), ())), preferred_element_type=F32)


def _dot3_nt(a, b):
    ah, am, _ = _split3(a)
    bh, bm, _ = _split3(b)
    return _dot_nt(ah, bh) + (_dot_nt(ah, bm) + _dot_nt(am, bh))


def _norm_linear_kernel(x_ref, g_ref, w_ref, o_ref, xn_ref):
    @pl.when(pl.program_id(1) == 0)
    def _():
        xn_ref[...] = _rms(x_ref[...], g_ref[...]).astype(BF16)

    o_ref[...] = _dot(xn_ref[...], w_ref[...]).astype(o_ref.dtype)


def norm_linear(x, g, w, *, tm, tn, out_dtype=F32):
    m, d = x.shape
    n = w.shape[1]
    return pl.pallas_call(
        _norm_linear_kernel,
        grid=(m // tm, n // tn),
        in_specs=[pl.BlockSpec((tm, d), lambda i, j: (i, 0)),
                  pl.BlockSpec((1, d), lambda i, j: (0, 0)),
                  pl.BlockSpec((d, tn), lambda i, j: (0, j))],
        out_specs=pl.BlockSpec((tm, tn), lambda i, j: (i, j)),
        out_shape=jax.ShapeDtypeStruct((m, n), out_dtype),
        scratch_shapes=[pltpu.VMEM((tm, d), BF16)],
        compiler_params=_cparams("parallel", "arbitrary"),
        name="norm_linear",
    )(x, g.reshape(1, d), w)


def _linear_res_kernel(x_ref, w_ref, r_ref, o_ref):
    o_ref[...] = r_ref[...] + _dot(x_ref[...].astype(BF16), w_ref[...])


def linear_residual(x, w, res, *, tm):
    m, k = x.shape
    n = w.shape[1]
    return pl.pallas_call(
        _linear_res_kernel,
        grid=(m // tm,),
        in_specs=[pl.BlockSpec((tm, k), lambda i: (i, 0)),
                  pl.BlockSpec((k, n), lambda i: (0, 0)),
                  pl.BlockSpec((tm, n), lambda i: (i, 0))],
        out_specs=pl.BlockSpec((tm, n), lambda i: (i, 0)),
        out_shape=jax.ShapeDtypeStruct((m, n), F32),
        compiler_params=_cparams("parallel"),
        name="linear_residual",
    )(x, w, res)


def _mlp_ple_kernel(h_ref, gm_ref, wu_ref, wd_ref, p_ref, gp_ref, wg_ref, wp_ref, gf_ref, o_ref,
                    xn_ref, acc_ref, *, final):
    f = pl.program_id(1)

    @pl.when(f == 0)
    def _():
        xn_ref[...] = _rms(h_ref[...], gm_ref[...]).astype(BF16)
        acc_ref[...] = jnp.zeros_like(acc_ref)

    hid = jnp.square(jnp.maximum(_dot(xn_ref[...], wu_ref[...]), 0.0))
    acc_ref[...] += _dot(hid.astype(BF16), wd_ref[...])

    @pl.when(f == pl.num_programs(1) - 1)
    def _():
        h1 = h_ref[...] + acc_ref[...]
        gate = _sigmoid(_dot(_rms(h1, gp_ref[...]).astype(BF16), wg_ref[...]))
        h2 = h1 + _dot(p_ref[...].astype(BF16), wp_ref[...]) * gate
        o_ref[...] = _rms(h2, gf_ref[...]) if final else h2


def mlp_ple(h, g_mlp, w_up, w_down, p, g_ple, w_gate, w_proj, g_final, *, tm, tf, final):
    m, d = h.shape
    ff = w_up.shape[1]
    pd = p.shape[1]
    row = lambda i, f: (i, 0)
    fixed = lambda i, f: (0, 0)
    return pl.pallas_call(
        functools.partial(_mlp_ple_kernel, final=final),
        grid=(m // tm, ff // tf),
        in_specs=[pl.BlockSpec((tm, d), row),
                  pl.BlockSpec((1, d), fixed),
                  pl.BlockSpec((d, tf), lambda i, f: (0, f)),
                  pl.BlockSpec((tf, d), lambda i, f: (f, 0)),
                  pl.BlockSpec((tm, pd), row),
                  pl.BlockSpec((1, d), fixed),
                  pl.BlockSpec((d, d), fixed),
                  pl.BlockSpec((pd, d), fixed),
                  pl.BlockSpec((1, d), fixed)],
        out_specs=pl.BlockSpec((tm, d), row),
        out_shape=jax.ShapeDtypeStruct((m, d), F32),
        scratch_shapes=[pltpu.VMEM((tm, d), BF16), pltpu.VMEM((tm, d), F32)],
        compiler_params=_cparams("parallel", "arbitrary"),
        name="mlp_ple",
    )(h, g_mlp.reshape(1, d), w_up, w_down, p, g_ple.reshape(1, d), w_gate, w_proj, g_final.reshape(1, d))


def _ssd_head_select():
    rows = jnp.arange(LANES)[:, None]
    cols = jnp.arange(SSD_HEADS * LANES)[None, :] // LANES
    return ((rows % SSD_HEADS == cols) & (rows < 3 * SSD_HEADS)).astype(BF16)


def _ssd_chunk_kernel(z_ref, xbc_ref, dt_ref, cw_ref, cb_ref, dtb_ref, alog_ref, dskip_ref, ng_ref, sel_ref,
                      y_ref, conv_ref, state_ref, xp_ref, act_ref, acol_ref, st_ref, xbd_ref, sbd_ref):
    c = pl.program_id(1)
    t = SSD_CHUNK
    gw = SSD_GROUP_W
    hpg = SSD_HEADS_PER_GROUP
    hd = SSD_HEAD_DIM

    @pl.when(c == 0)
    def _():
        xp_ref[:, 0:SUBLANES, :] = jnp.zeros((xp_ref.shape[0], SUBLANES, LANES), F32)
        st_ref[...] = jnp.zeros_like(st_ref)
        xbd_ref[...] = jnp.zeros_like(xbd_ref)
        sbd_ref[...] = jnp.zeros_like(sbd_ref)

    for j in range(SSD_CONV_DIM // LANES):
        cols = slice(j * LANES, (j + 1) * LANES)
        xp_ref[j, SUBLANES:SUBLANES + t, :] = xbc_ref[:, cols]
        acc = xp_ref[j, SUBLANES - 3:SUBLANES - 3 + t, :] * cw_ref[0:1, cols]
        for k in range(1, CONV_W):
            acc = acc + xp_ref[j, SUBLANES - 3 + k:SUBLANES - 3 + k + t, :] * cw_ref[k:k + 1, cols]
        act_ref[:, cols] = _silu(acc + cb_ref[:, cols])
        xp_ref[j, 0:SUBLANES, :] = xbc_ref[t - SUBLANES:t, cols]
    conv_ref[0] = xbc_ref[t - SUBLANES:t, :]

    dtv = _softplus(dt_ref[...] + dtb_ref[...])
    da = dtv * (-jnp.exp(alog_ref[...]))
    row = lax.broadcasted_iota(jnp.int32, (t, t), 0)
    col = lax.broadcasted_iota(jnp.int32, (t, t), 1)
    causal = row >= col
    tril = causal.astype(BF16)
    d_hi, d_mid, d_lo = _split3(da)
    a2 = (_dot(tril, d_hi) + (_dot(tril, d_mid) + _dot(tril, d_lo))) * LOG2E
    arow_t = (a2 - jnp.log(dtv) * LOG2E).T

    lane = lax.broadcasted_iota(jnp.int32, (t, LANES), 1)
    a_hi, a_mid, a_lo = _split3(jnp.where(lane < SSD_HEADS, a2, 0.0))
    packed = (a_hi.astype(F32) + pltpu.roll(a_mid.astype(F32), SSD_HEADS, 1)
              + pltpu.roll(a_lo.astype(F32), 2 * SSD_HEADS, 1)).astype(BF16)
    acol_ref[...] = _dot(packed, sel_ref[...])

    lane_g = lax.broadcasted_iota(jnp.int32, (1, gw), 1) // hd
    for g in range(SSD_GROUPS):
        xs_g = act_ref[:, g * gw:(g + 1) * gw]
        b_g = act_ref[:, SSD_D_INNER + g * SSD_STATE:SSD_D_INNER + (g + 1) * SSD_STATE]
        c_g = act_ref[:, SSD_D_INNER + SSD_BC_DIM + g * SSD_STATE:
                      SSD_D_INNER + SSD_BC_DIM + (g + 1) * SSD_STATE]
        cb = jnp.where(causal, _dot_nt(c_g.astype(BF16), b_g.astype(BF16)), 0.0)
        b_gt = b_g.T
        st_g = st_ref[g]
        xs_b = xs_g.astype(BF16)
        for r in range(hpg):
            xbd_ref[g, r * t:(r + 1) * t, r * hd:(r + 1) * hd] = xs_b[:, r * hd:(r + 1) * hd]
        m_parts, d_parts, w_parts = [], [], []
        dec_g = jnp.zeros((1, gw), F32)
        for r in range(hpg):
            h = g * hpg + r
            acol = acol_ref[:, h * LANES:(h + 1) * LANES]
            arow = arow_t[h:h + 1, :]
            aend = acol[t - 1:t, :]
            m_parts.append((cb * jnp.exp2(jnp.minimum(acol - arow, EXP2_CAP))).astype(BF16))
            d_parts.append((c_g * jnp.exp2(acol)).astype(BF16))
            w_parts.append((b_gt * jnp.exp2(aend - arow)).astype(BF16))
            dec_g = jnp.where(lane_g == r, jnp.exp2(jnp.concatenate([aend, aend], axis=1)), dec_g)
        x_bd = xbd_ref[g]
        y_g = (_dot(jnp.concatenate(m_parts, axis=1), x_bd) + _dot(jnp.concatenate(d_parts, axis=1), sbd_ref[g])
               + xs_g * dskip_ref[:, g * gw:(g + 1) * gw])
        st_new = st_g * dec_g + _dot(jnp.concatenate(w_parts, axis=1), x_bd)
        st_ref[g] = st_new
        st_b = st_new.astype(BF16)
        for r in range(hpg):
            sbd_ref[g, r * SSD_STATE:(r + 1) * SSD_STATE, r * hd:(r + 1) * hd] = st_b[:, r * hd:(r + 1) * hd]
        yz = y_g * _silu(z_ref[:, g * gw:(g + 1) * gw])
        y_ref[:, g * gw:(g + 1) * gw] = _rms(yz, ng_ref[:, g * gw:(g + 1) * gw]).astype(y_ref.dtype)

    @pl.when(c == pl.num_programs(1) - 1)
    def _():
        for g in range(SSD_GROUPS):
            state_ref[0, g * hpg:(g + 1) * hpg] = st_ref[g].T.reshape(hpg, SSD_HEAD_DIM, SSD_STATE)


def ssd_prompt(z, xbc, dt, conv_w, conv_b, dt_bias, a_log, d_skip, norm_g, batch, seq):
    t = SSD_CHUNK
    nc = seq // t
    pad = LANES - SSD_HEADS
    rowblk = lambda b, c: (b * nc + c, 0)
    fixed = lambda b, c: (0, 0)
    y, conv_tail, state = pl.pallas_call(
        _ssd_chunk_kernel,
        grid=(batch, nc),
        in_specs=[pl.BlockSpec((t, SSD_D_INNER), rowblk),
                  pl.BlockSpec((t, SSD_CONV_DIM), rowblk),
                  pl.BlockSpec((t, LANES), rowblk),
                  pl.BlockSpec((CONV_W, SSD_CONV_DIM), fixed),
                  pl.BlockSpec((1, SSD_CONV_DIM), fixed),
                  pl.BlockSpec((1, LANES), fixed),
                  pl.BlockSpec((1, LANES), fixed),
                  pl.BlockSpec((1, SSD_D_INNER), fixed),
                  pl.BlockSpec((1, SSD_D_INNER), fixed),
                  pl.BlockSpec((LANES, SSD_HEADS * LANES), fixed)],
        out_specs=[pl.BlockSpec((t, SSD_D_INNER), rowblk),
                   pl.BlockSpec((1, SUBLANES, SSD_CONV_DIM), lambda b, c: (b, 0, 0)),
                   pl.BlockSpec((1, SSD_HEADS, SSD_HEAD_DIM, SSD_STATE), lambda b, c: (b, 0, 0, 0))],
        out_shape=[jax.ShapeDtypeStruct((batch * seq, SSD_D_INNER), BF16),
                   jax.ShapeDtypeStruct((batch, SUBLANES, SSD_CONV_DIM), F32),
                   jax.ShapeDtypeStruct((batch, SSD_HEADS, SSD_HEAD_DIM, SSD_STATE), F32)],
        scratch_shapes=[pltpu.VMEM((SSD_CONV_DIM // LANES, SUBLANES + t, LANES), F32),
                        pltpu.VMEM((t, SSD_CONV_DIM), F32),
                        pltpu.VMEM((t, SSD_HEADS * LANES), F32),
                        pltpu.VMEM((SSD_GROUPS, SSD_STATE, SSD_GROUP_W), F32),
                        pltpu.VMEM((SSD_GROUPS, SSD_HEADS_PER_GROUP * t, SSD_GROUP_W), BF16),
                        pltpu.VMEM((SSD_GROUPS, SSD_HEADS_PER_GROUP * SSD_STATE, SSD_GROUP_W), BF16)],
        compiler_params=_cparams("parallel", "arbitrary"),
        name="ssd_prompt",
    )(z, xbc, dt, conv_w, conv_b.reshape(1, -1), jnp.pad(dt_bias, (0, pad)).reshape(1, LANES),
      jnp.pad(a_log, (0, pad)).reshape(1, LANES), jnp.repeat(d_skip, SSD_HEAD_DIM).reshape(1, -1),
      norm_g.reshape(1, -1), _ssd_head_select())
    return y, conv_tail[:, SUBLANES - (CONV_W - 1):, :], state


MOBA_SPAN = 4 * MOBA_BLOCK


def _col_reduce(x, pair_op, reduce_fn):
    n = x.shape[0]
    while n > 8 * SUBLANES and n % 2 == 0:
        n //= 2
        x = pair_op(x[:n], x[n:])
    return reduce_fn(x, axis=0, keepdims=True)


def _moba_prompt_kernel(q_ref, k_ref, v_ref, o_ref, kb_ref, vt_ref, sel_ref, s_ref, p_ref):
    qi = pl.program_id(2)
    blk = MOBA_BLOCK
    seq = k_ref.shape[0]
    nb = seq // blk
    nbp = sel_ref.shape[1]

    @pl.when(qi == 0)
    def _():
        means = []
        for j in range(nb):
            kj = k_ref[j * blk:(j + 1) * blk, :]
            means.append(jnp.sum(kj, axis=0, keepdims=True) * (1.0 / blk))
            kb_ref[j * blk:(j + 1) * blk, :] = kj.astype(BF16)
        for j in range(seq // LANES):
            vt_ref[:, j * LANES:(j + 1) * LANES] = v_ref[j * LANES:(j + 1) * LANES, :].T.astype(BF16)
        if nbp > nb:
            means.append(jnp.zeros((nbp - nb, MOBA_HEAD_DIM), F32))
        kmean = jnp.concatenate(means, axis=0)
        sub = lax.broadcasted_iota(jnp.int32, (nbp, blk), 0)
        for i in range(nb):
            gm = jnp.where(sub < i, _dot3_nt(kmean, q_ref[i * blk:(i + 1) * blk, :]), -jnp.inf)
            rank = jnp.zeros((nbp, blk), F32)
            for m in range(i):
                gm_m = gm[m:m + 1, :]
                beats = (gm_m > gm) | ((gm_m == gm) & (sub > m))
                rank = rank + jnp.where(beats, 1.0, 0.0)
            chosen = ((sub < i) & (rank < MOBA_TOPK)) | (sub == i)
            sel_ref[i] = jnp.where(chosen, 1.0, 0.0)

    q = q_ref[pl.ds(pl.multiple_of(qi * blk, blk), blk), :]
    qb = (q * (MOBA_HEAD_DIM ** -0.5)).astype(BF16)
    sel = sel_ref[qi]
    krow = lax.broadcasted_iota(jnp.int32, (blk, blk), 0)
    qcol = lax.broadcasted_iota(jnp.int32, (blk, blk), 1)
    causal = krow <= qcol

    span = min(MOBA_SPAN, seq)
    per_span = span // blk

    def attend(n_keys):
        m = None
        for j in range(n_keys // blk):
            rows = slice(j * blk, (j + 1) * blk)
            ok = (sel[j:j + 1, :] > 0.0) & (causal | (j < qi))
            s = jnp.where(ok, _dot_nt(kb_ref[rows, :], qb), NEG_BIG)
            s_ref[rows, :] = s
            bm = _col_reduce(s, jnp.maximum, jnp.max)
            m = bm if m is None else jnp.maximum(m, bm)
        l = jnp.zeros((1, blk), F32)
        for j in range(n_keys // blk):
            rows = slice(j * blk, (j + 1) * blk)
            p = jnp.exp(s_ref[rows, :] - m)
            l = l + _col_reduce(p, jnp.add, jnp.sum)
            p_ref[rows, :] = p.astype(BF16)
        acc = _dot(vt_ref[:, 0:n_keys], p_ref[0:n_keys, :])
        o_ref[...] = (acc / l).T.astype(o_ref.dtype)

    for sp in range(seq // span):
        in_span = (qi >= sp * per_span) & (qi < (sp + 1) * per_span)
        pl.when(in_span)(functools.partial(attend, (sp + 1) * span))


def moba_prompt(q, k, v, batch, seq):
    nb = seq // MOBA_BLOCK
    hd = MOBA_HEAD_DIM
    return pl.pallas_call(
        _moba_prompt_kernel,
        grid=(batch, MOBA_HEADS, nb),
        in_specs=[pl.BlockSpec((seq, hd), lambda b, h, i: (b, h)),
                  pl.BlockSpec((seq, hd), lambda b, h, i: (b, h)),
                  pl.BlockSpec((seq, hd), lambda b, h, i: (b, h))],
        out_specs=pl.BlockSpec((MOBA_BLOCK, hd), lambda b, h, i: (b * nb + i, h)),
        out_shape=jax.ShapeDtypeStruct((batch * seq, D_MODEL), BF16),
        scratch_shapes=[pltpu.VMEM((seq, hd), BF16),
                        pltpu.VMEM((hd, seq), BF16),
                        pltpu.VMEM((nb, -(-nb // SUBLANES) * SUBLANES, MOBA_BLOCK), F32),
                        pltpu.VMEM((seq, MOBA_BLOCK), F32),
                        pltpu.VMEM((seq, MOBA_BLOCK), BF16)],
        compiler_params=_cparams("parallel", "parallel", "arbitrary"),
        name="moba_prompt",
    )(q, k, v)


def _lru_gates(xc, wa, ba, wx, bx, lam):
    xb = xc.astype(BF16)
    r = _sigmoid(_dot(xb, wa) + ba)
    i = _sigmoid(_dot(xb, wx) + bx)
    log_a = (-LRU_C) * r * _softplus(-lam)
    a = jnp.exp(log_a)
    u = jnp.sqrt(-jnp.tanh(log_a) * (a * a + 1.0)) * (i * xc)
    return a, u


def _lru_chunk_kernel(gate_ref, x_ref, cw_ref, cb_ref, wa_ref, ba_ref, wx_ref, bx_ref, lam_ref,
                      y_ref, conv_ref, hlast_ref, xp_ref, a_ref, u_ref, h_ref):
    c = pl.program_id(1)
    t = LRU_CHUNK
    w = LRU_BLOCK_W

    @pl.when(c == 0)
    def _():
        xp_ref[:, 0:SUBLANES, :] = jnp.zeros((LRU_BLOCKS, SUBLANES, w), F32)
        h_ref[...] = jnp.zeros_like(h_ref)

    for n in range(LRU_BLOCKS):
        cols = slice(n * w, (n + 1) * w)
        xp_ref[n, SUBLANES:SUBLANES + t, :] = x_ref[:, cols]
        acc = xp_ref[n, SUBLANES - 3:SUBLANES - 3 + t, :] * cw_ref[0:1, cols]
        for k in range(1, CONV_W):
            acc = acc + xp_ref[n, SUBLANES - 3 + k:SUBLANES - 3 + k + t, :] * cw_ref[k:k + 1, cols]
        xc = acc + cb_ref[:, cols]
        a, u = _lru_gates(xc, wa_ref[n], ba_ref[:, cols], wx_ref[n], bx_ref[:, cols], lam_ref[:, cols])
        a_ref[:, cols] = a
        u_ref[:, cols] = u
        xp_ref[n, 0:SUBLANES, :] = x_ref[t - SUBLANES:t, cols]
    conv_ref[0] = x_ref[t - SUBLANES:t, :]

    srow = lax.broadcasted_iota(jnp.int32, (SUBLANES, LRU_WIDTH), 0)

    def tile(i, h):
        rows = pl.ds(pl.multiple_of(i * SUBLANES, SUBLANES), SUBLANES)
        a = a_ref[rows, :]
        u = u_ref[rows, :]
        for d in (1, 2, 4):
            keep = srow >= d
            a_sh = jnp.where(keep, pltpu.roll(a, d, 0), 1.0)
            u_sh = jnp.where(keep, pltpu.roll(u, d, 0), 0.0)
            u = a * u_sh + u
            a = a * a_sh
        hs = a * h + u
        u_ref[rows, :] = hs
        return jnp.broadcast_to(hs[SUBLANES - 1:SUBLANES, :], (SUBLANES, LRU_WIDTH))

    h = lax.fori_loop(0, t // SUBLANES, tile, h_ref[...])
    h_ref[...] = h
    hlast_ref[0] = h
    y_ref[...] = (_gelu_tanh(gate_ref[...]) * u_ref[...]).astype(y_ref.dtype)


def lru_prompt(gate_br, x_br, conv_w, conv_b, w_a, b_a, w_x, b_x, lam, batch, seq):
    t = LRU_CHUNK
    nc = seq // t
    wd = LRU_WIDTH
    rowblk = lambda b, c: (b * nc + c, 0)
    fixed = lambda b, c: (0, 0)
    fixed3 = lambda b, c: (0, 0, 0)
    perb = lambda b, c: (b, 0, 0)
    y, conv_tail, h_last = pl.pallas_call(
        _lru_chunk_kernel,
        grid=(batch, nc),
        in_specs=[pl.BlockSpec((t, wd), rowblk),
                  pl.BlockSpec((t, wd), rowblk),
                  pl.BlockSpec((CONV_W, wd), fixed),
                  pl.BlockSpec((1, wd), fixed),
                  pl.BlockSpec((LRU_BLOCKS, LRU_BLOCK_W, LRU_BLOCK_W), fixed3),
                  pl.BlockSpec((1, wd), fixed),
                  pl.BlockSpec((LRU_BLOCKS, LRU_BLOCK_W, LRU_BLOCK_W), fixed3),
                  pl.BlockSpec((1, wd), fixed),
                  pl.BlockSpec((1, wd), fixed)],
        out_specs=[pl.BlockSpec((t, wd), rowblk),
                   pl.BlockSpec((1, SUBLANES, wd), perb),
                   pl.BlockSpec((1, SUBLANES, wd), perb)],
        out_shape=[jax.ShapeDtypeStruct((batch * seq, wd), BF16),
                   jax.ShapeDtypeStruct((batch, SUBLANES, wd), F32),
                   jax.ShapeDtypeStruct((batch, SUBLANES, wd), F32)],
        scratch_shapes=[pltpu.VMEM((LRU_BLOCKS, SUBLANES + t, LRU_BLOCK_W), F32),
                        pltpu.VMEM((t, wd), F32),
                        pltpu.VMEM((t, wd), F32),
                        pltpu.VMEM((SUBLANES, wd), F32)],
        compiler_params=_cparams("parallel", "arbitrary"),
        name="lru_prompt",
    )(gate_br, x_br, conv_w, conv_b.reshape(1, wd), w_a.astype(BF16), b_a.reshape(1, wd),
      w_x.astype(BF16), b_x.reshape(1, wd), lam.reshape(1, wd))
    return y, conv_tail[:, SUBLANES - (CONV_W - 1):, :], h_last[:, 0, :]


def _ssd_step_kernel(z_ref, xbc_ref, dt_ref, cs_ref, st_ref, cw_ref, cb_ref, dtb_ref, alog_ref, dskip_ref, ng_ref,
                     y_ref, cso_ref, sto_ref):
    gw = SSD_GROUP_W
    hpg = SSD_HEADS_PER_GROUP
    x = xbc_ref[0]
    buf = cs_ref[0]
    conv = x * cw_ref[CONV_W - 1:CONV_W, :]
    for k in range(CONV_W - 1):
        conv = conv + buf[k:k + 1, :] * cw_ref[k:k + 1, :]
    act = _silu(conv + cb_ref[...])
    cso_ref[0, 0:CONV_W - 2, :] = buf[1:CONV_W - 1, :]
    cso_ref[0, CONV_W - 2:CONV_W - 1, :] = x

    dtv = _softplus(dt_ref[0] + dtb_ref[...])
    dec = jnp.exp(dtv * (-jnp.exp(alog_ref[...])))
    lane_g = lax.broadcasted_iota(jnp.int32, (1, gw), 1) // SSD_HEAD_DIM
    first_row = lax.broadcasted_iota(jnp.int32, (SUBLANES, gw), 0) == 0
    ys = []
    for g in range(SSD_GROUPS):
        xs_g = act[:, g * gw:(g + 1) * gw]
        b_g = act[:, SSD_D_INNER + g * SSD_STATE:SSD_D_INNER + (g + 1) * SSD_STATE]
        c_g = act[:, SSD_D_INNER + SSD_BC_DIM + g * SSD_STATE:SSD_D_INNER + SSD_BC_DIM + (g + 1) * SSD_STATE]
        dt_e = jnp.zeros((1, gw), F32)
        dec_rows = []
        for r in range(hpg):
            h = g * hpg + r
            dt_e = jnp.where(lane_g == r, dtv[:, h:h + 1], dt_e)
            dec_rows.append(jnp.broadcast_to(dec[:, h:h + 1], (SSD_HEAD_DIM, SSD_STATE)))
        xdt = xs_g * dt_e
        x8 = jnp.where(first_row, jnp.broadcast_to(xdt, (SUBLANES, gw)), 0.0).astype(BF16)
        b8 = jnp.broadcast_to(b_g, (SUBLANES, SSD_STATE)).astype(BF16)
        outer = lax.dot_general(x8, b8, (((0,), (0,)), ((), ())), preferred_element_type=F32)
        old = st_ref[0, g * hpg:(g + 1) * hpg].reshape(gw, SSD_STATE)
        new = old * jnp.concatenate(dec_rows, axis=0) + outer
        sto_ref[0, g * hpg:(g + 1) * hpg] = new.reshape(hpg, SSD_HEAD_DIM, SSD_STATE)
        c8 = jnp.broadcast_to(c_g, (SUBLANES, SSD_STATE)).astype(BF16)
        y_g = _dot_nt(c8, new.astype(BF16))[0:1, :] + xs_g * dskip_ref[:, g * gw:(g + 1) * gw]
        yz = y_g * _silu(z_ref[0][:, g * gw:(g + 1) * gw])
        ys.append(_rms(yz, ng_ref[:, g * gw:(g + 1) * gw]))
    y_ref[0] = jnp.concatenate(ys, axis=1)


def ssd_step(z, xbc, dt, conv_state, ssm_state, conv_w, conv_b, dt_bias, a_log, d_skip, norm_g):
    nb = z.shape[0]
    pad = LANES - SSD_HEADS
    per3 = lambda b: (b, 0, 0)
    fixed = lambda b: (0, 0)
    y, conv_new, state_new = pl.pallas_call(
        _ssd_step_kernel,
        grid=(nb,),
        in_specs=[pl.BlockSpec((1, 1, SSD_D_INNER), per3),
                  pl.BlockSpec((1, 1, SSD_CONV_DIM), per3),
                  pl.BlockSpec((1, 1, LANES), per3),
                  pl.BlockSpec((1, CONV_W - 1, SSD_CONV_DIM), per3),
                  pl.BlockSpec((1, SSD_HEADS, SSD_HEAD_DIM, SSD_STATE), lambda b: (b, 0, 0, 0)),
                  pl.BlockSpec((CONV_W, SSD_CONV_DIM), fixed),
                  pl.BlockSpec((1, SSD_CONV_DIM), fixed),
                  pl.BlockSpec((1, LANES), fixed),
                  pl.BlockSpec((1, LANES), fixed),
                  pl.BlockSpec((1, SSD_D_INNER), fixed),
                  pl.BlockSpec((1, SSD_D_INNER), fixed)],
        out_specs=[pl.BlockSpec((1, 1, SSD_D_INNER), per3),
                   pl.BlockSpec((1, CONV_W - 1, SSD_CONV_DIM), per3),
                   pl.BlockSpec((1, SSD_HEADS, SSD_HEAD_DIM, SSD_STATE), lambda b: (b, 0, 0, 0))],
        out_shape=[jax.ShapeDtypeStruct((nb, 1, SSD_D_INNER), F32),
                   jax.ShapeDtypeStruct((nb, CONV_W - 1, SSD_CONV_DIM), F32),
                   jax.ShapeDtypeStruct((nb, SSD_HEADS, SSD_HEAD_DIM, SSD_STATE), F32)],
        compiler_params=_cparams("parallel"),
        name="ssd_step",
    )(z.reshape(nb, 1, -1), xbc.reshape(nb, 1, -1), dt.reshape(nb, 1, -1), conv_state, ssm_state,
      conv_w, conv_b.reshape(1, -1), jnp.pad(dt_bias, (0, pad)).reshape(1, LANES),
      jnp.pad(a_log, (0, pad)).reshape(1, LANES), jnp.repeat(d_skip, SSD_HEAD_DIM).reshape(1, -1),
      norm_g.reshape(1, -1))
    return y.reshape(nb, SSD_D_INNER), conv_new, state_new


def _lru_step_kernel(gate_ref, x_ref, cs_ref, h0_ref, cw_ref, cb_ref, wa_ref, ba_ref, wx_ref, bx_ref, lam_ref,
                     y_ref, cso_ref, ho_ref):
    w = LRU_BLOCK_W
    x = x_ref[...]
    xc = x * cw_ref[CONV_W - 1:CONV_W, :] + cb_ref[...]
    for k in range(CONV_W - 1):
        xc = xc + cs_ref[k] * cw_ref[k:k + 1, :]
    for k in range(CONV_W - 2):
        cso_ref[k] = cs_ref[k + 1]
    cso_ref[CONV_W - 2] = x
    for n in range(LRU_BLOCKS):
        cols = slice(n * w, (n + 1) * w)
        a, u = _lru_gates(xc[:, cols], wa_ref[n], ba_ref[:, cols], wx_ref[n], bx_ref[:, cols], lam_ref[:, cols])
        h = a * h0_ref[:, cols] + u
        ho_ref[:, cols] = h
        y_ref[:, cols] = _gelu_tanh(gate_ref[:, cols]) * h


def lru_step(gate_br, x_br, conv_state, h0, conv_w, conv_b, w_a, b_a, w_x, b_x, lam):
    nb, wd = x_br.shape
    return pl.pallas_call(
        _lru_step_kernel,
        out_shape=[jax.ShapeDtypeStruct((nb, wd), F32),
                   jax.ShapeDtypeStruct((CONV_W - 1, nb, wd), F32),
                   jax.ShapeDtypeStruct((nb, wd), F32)],
        compiler_params=pltpu.CompilerParams(vmem_limit_bytes=VMEM_LIMIT),
        name="lru_step",
    )(gate_br, x_br, conv_state, h0, conv_w, conv_b.reshape(1, wd), w_a.astype(BF16), b_a.reshape(1, wd),
      w_x.astype(BF16), b_x.reshape(1, wd), lam.reshape(1, wd))


MOBA_SUM_PAGES = 16


def _moba_block_sums_kernel(tbl_ref, *refs):
    del tbl_ref
    pages, o_ref = refs[:-1], refs[-1]
    for k in range(len(pages) // PAGES_PER_BLOCK):
        s = jnp.sum(pages[PAGES_PER_BLOCK * k][...], axis=0)
        for j in range(1, PAGES_PER_BLOCK):
            s = s + jnp.sum(pages[PAGES_PER_BLOCK * k + j][...], axis=0)
        o_ref[0, k] = s


def moba_block_sums(pool, layer, table):
    nb, n_pages = table.shape
    steps = n_pages // MOBA_SUM_PAGES
    blocks_per_step = MOBA_SUM_PAGES // PAGES_PER_BLOCK
    page_shape = pool.shape[2:]

    def page_spec(k):
        return pl.BlockSpec((None, None) + page_shape,
                            lambda b, c, tbl: (layer, tbl[b * n_pages + c * MOBA_SUM_PAGES + k], 0, 0, 0))

    return pl.pallas_call(
        _moba_block_sums_kernel,
        grid_spec=pltpu.PrefetchScalarGridSpec(
            num_scalar_prefetch=1,
            grid=(nb, steps),
            in_specs=[page_spec(k) for k in range(MOBA_SUM_PAGES)],
            out_specs=pl.BlockSpec((1, blocks_per_step) + page_shape[1:], lambda b, c, tbl: (b, c, 0, 0))),
        out_shape=jax.ShapeDtypeStruct((nb, n_pages // PAGES_PER_BLOCK) + page_shape[1:], F32),
        compiler_params=_cparams("parallel", "arbitrary"),
        name="moba_block_sums",
    )(table.reshape(-1), *([pool] * MOBA_SUM_PAGES))


def _moba_gate_kernel(q_ref, bs_ref, sel_ref):
    n_blocks = bs_ref.shape[1]
    q = q_ref[0]
    gates = [jnp.sum((bs_ref[0, n] * (1.0 / MOBA_BLOCK)) * q, axis=-1, keepdims=True) for n in range(n_blocks)]
    lane = lax.broadcasted_iota(jnp.int32, (MOBA_HEADS, LANES), 1)
    out = jnp.zeros((MOBA_HEADS, LANES), F32)
    for k in range(MOBA_TOPK):
        best = functools.reduce(jnp.maximum, gates)
        idx = jnp.full((MOBA_HEADS, 1), float(n_blocks), F32)
        for n in reversed(range(n_blocks)):
            idx = jnp.where(gates[n] == best, float(n), idx)
        out = jnp.where(lane == k, idx, out)
        gates = [jnp.where(idx == float(n), -jnp.inf, gates[n]) for n in range(n_blocks)]
    sel_ref[0] = out.astype(jnp.int32)


def moba_gate(q, block_sums):
    nb = q.shape[0]
    n_blocks = block_sums.shape[1]
    return pl.pallas_call(
        _moba_gate_kernel,
        grid=(nb,),
        in_specs=[pl.BlockSpec((1, MOBA_HEADS, MOBA_HEAD_DIM), lambda b: (b, 0, 0)),
                  pl.BlockSpec((1, n_blocks, MOBA_HEADS, MOBA_HEAD_DIM), lambda b: (b, 0, 0, 0))],
        out_specs=pl.BlockSpec((1, MOBA_HEADS, LANES), lambda b: (b, 0, 0)),
        out_shape=jax.ShapeDtypeStruct((nb, MOBA_HEADS, LANES), jnp.int32),
        compiler_params=_cparams("parallel"),
        name="moba_gate",
    )(q.reshape(nb, MOBA_HEADS, MOBA_HEAD_DIM), block_sums)


def _moba_decode_kernel(tbl_ref, sel_ref, q_ref, kn_ref, vn_ref, kpool_ref, vpool_ref, o_ref, kbuf, vbuf, sem,
                        *, layer, n_table_pages):
    b, h = pl.program_id(0), pl.program_id(1)
    nh = pl.num_programs(1)
    step = b * nh + h
    slot = lax.rem(step, 2)
    n_sel = MOBA_TOPK * PAGES_PER_BLOCK

    def page_copies(bb, hh, sl):
        out = []
        for k in range(MOBA_TOPK):
            blk = sel_ref[(bb * nh + hh) * MOBA_TOPK + k]
            for s in range(PAGES_PER_BLOCK):
                page = tbl_ref[bb * n_table_pages + blk * PAGES_PER_BLOCK + s]
                i = k * PAGES_PER_BLOCK + s
                out.append(pltpu.make_async_copy(kpool_ref.at[layer, page, :, hh, :], kbuf.at[sl, i], sem.at[sl, 0, i]))
                out.append(pltpu.make_async_copy(vpool_ref.at[layer, page, :, hh, :], vbuf.at[sl, i], sem.at[sl, 1, i]))
        return out

    @pl.when(step == 0)
    def _():
        for c in page_copies(b, h, 0):
            c.start()

    @pl.when(step + 1 < pl.num_programs(0) * nh)
    def _():
        wrap = h + 1 == nh
        for c in page_copies(jnp.where(wrap, b + 1, b), jnp.where(wrap, 0, h + 1), 1 - slot):
            c.start()

    for c in page_copies(b, h, slot):
        c.wait()

    scale = MOBA_HEAD_DIM ** -0.5
    q = q_ref[0]
    q8 = jnp.broadcast_to(q, (SUBLANES, MOBA_HEAD_DIM)).astype(BF16)
    k_cat = kbuf[slot].reshape(n_sel * PAGE_SIZE, MOBA_HEAD_DIM).astype(BF16)
    v_cat = vbuf[slot].reshape(n_sel * PAGE_SIZE, MOBA_HEAD_DIM).astype(BF16)
    s = _dot_nt(q8, k_cat) * scale
    s_new = jnp.sum(q * kn_ref[0], axis=1, keepdims=True) * scale
    m = jnp.maximum(jnp.max(s, axis=1, keepdims=True), s_new)
    p = jnp.exp(s - m)
    p_new = jnp.exp(s_new - m)
    denom = jnp.sum(p, axis=1, keepdims=True) + p_new
    o = (_dot(p.astype(BF16), v_cat) + p_new * vn_ref[0]) / denom
    o_ref[0] = o[0:1, :]


def moba_decode(q, k_new, v_new, k_pool, v_pool, layer, table, sel):
    nb = q.shape[0]
    n_pages = table.shape[1]
    hd = MOBA_HEAD_DIM
    n_sel = MOBA_TOPK * PAGES_PER_BLOCK
    tok = pl.BlockSpec((1, 1, hd), lambda b, h, tbl, sl: (b, 0, h))
    hbm = pl.BlockSpec(memory_space=pl.ANY)
    out = pl.pallas_call(
        functools.partial(_moba_decode_kernel, layer=layer, n_table_pages=n_pages),
        grid_spec=pltpu.PrefetchScalarGridSpec(
            num_scalar_prefetch=2,
            grid=(nb, MOBA_HEADS),
            in_specs=[tok, tok, tok, hbm, hbm],
            out_specs=tok,
            scratch_shapes=[pltpu.VMEM((2, n_sel, PAGE_SIZE, hd), F32),
                            pltpu.VMEM((2, n_sel, PAGE_SIZE, hd), F32),
                            pltpu.SemaphoreType.DMA((2, 2, n_sel))]),
        out_shape=jax.ShapeDtypeStruct((nb, 1, D_MODEL), F32),
        compiler_params=_cparams("arbitrary", "arbitrary"),
        name="moba_decode",
    )(table.reshape(-1), sel[:, :, :MOBA_TOPK].reshape(-1),
      q.reshape(nb, 1, D_MODEL), k_new.reshape(nb, 1, D_MODEL), v_new.reshape(nb, 1, D_MODEL), k_pool, v_pool)
    return out.reshape(nb, D_MODEL)


def _prep_weights(W):
    bf = lambda a: a.astype(BF16)
    w_in = W["ssd_w_in"]
    zx = SSD_D_INNER + SSD_CONV_DIM
    return dict(
        ssd_w_z=bf(w_in[:, :, :SSD_D_INNER]),
        ssd_w_xbc=bf(w_in[:, :, SSD_D_INNER:zx]),
        ssd_w_dt=bf(jnp.pad(w_in[:, :, zx:], ((0, 0), (0, 0), (0, LANES - SSD_HEADS)))),
        ssd_w_out=bf(W["ssd_w_out"]),
        moba_w_q=bf(W["moba_w_qkv"][:, :, :D_MODEL]),
        moba_w_k=bf(W["moba_w_qkv"][:, :, D_MODEL:2 * D_MODEL]),
        moba_w_v=bf(W["moba_w_qkv"][:, :, 2 * D_MODEL:]),
        moba_w_o=bf(W["moba_w_o"]),
        lru_w_gate=bf(W["lru_w_in"][:, :, :LRU_WIDTH]),
        lru_w_x=bf(W["lru_w_in"][:, :, LRU_WIDTH:]),
        lru_w_out=bf(W["lru_w_out"]),
        mlp_w_up=bf(W["mlp_w_up"]), mlp_w_down=bf(W["mlp_w_down"]),
        ple_w_proj=bf(W["ple_w_proj"]), ple_w_gate=bf(W["ple_w_gate"]),
    )


def _trunk(h, p, W, Wb, tiles, ssd_fn, moba_fn, lru_fn):
    tm_proj, tn, tm, tf = tiles
    proj = lambda x, g, w: norm_linear(x, g, w, tm=tm_proj, tn=min(tn, w.shape[1]))
    ssd_out, moba_out, lru_out = [], [], []
    for i in range(DEPTH):
        j, kind = i // N_MIXERS, i % N_MIXERS
        g = W["norm_mix"][i]
        if kind == 0:
            y, conv, state = ssd_fn(j, proj(h, g, Wb["ssd_w_z"][j]), proj(h, g, Wb["ssd_w_xbc"][j]),
                                    proj(h, g, Wb["ssd_w_dt"][j]))
            ssd_out.append((conv, state))
            h = linear_residual(y, Wb["ssd_w_out"][j], h, tm=tm)
        elif kind == 1:
            q, k, v = (proj(h, g, Wb[n][j]) for n in ("moba_w_q", "moba_w_k", "moba_w_v"))
            moba_out.append((k, v))
            h = linear_residual(moba_fn(j, q, k, v), Wb["moba_w_o"][j], h, tm=tm)
        else:
            y, conv, state = lru_fn(j, proj(h, g, Wb["lru_w_gate"][j]), proj(h, g, Wb["lru_w_x"][j]))
            lru_out.append((conv, state))
            h = linear_residual(y, Wb["lru_w_out"][j], h, tm=tm)
        h = mlp_ple(h, W["norm_mlp"][i], Wb["mlp_w_up"][i], Wb["mlp_w_down"][i], p[i], W["norm_ple"][i],
                    Wb["ple_w_gate"][i], Wb["ple_w_proj"][i], W["norm_final"], tm=tm, tf=tf,
                    final=(i == DEPTH - 1))
    stack = lambda pairs, n: jnp.stack([pr[n] for pr in pairs])
    return (h, stack(ssd_out, 0), stack(ssd_out, 1), stack(moba_out, 0), stack(moba_out, 1),
            stack(lru_out, 0), stack(lru_out, 1))


PROMPT_TILES = (1024, 1024, 512, 1024)
SAMPLE_TILES = (32, 1024, 32, 1024)


def kernel(x_prompt, x_sample, state_ssd_conv, state_ssd, cache_k, cache_v, page_table, state_lru_conv,
           state_lru, p_prompt, p_sample, norm_mix, norm_mlp, norm_ple, norm_final, ssd_w_in, ssd_conv_w,
           ssd_conv_b, ssd_dt_bias, ssd_a_log, ssd_d, ssd_norm, ssd_w_out, moba_w_qkv, moba_w_o, lru_w_in,
           lru_conv_w, lru_conv_b, lru_w_a, lru_b_a, lru_w_x, lru_b_x, lru_lambda, lru_w_out, mlp_w_up,
           mlp_w_down, ple_w_proj, ple_w_gate):
    W = dict(norm_mix=norm_mix, norm_mlp=norm_mlp, norm_ple=norm_ple, norm_final=norm_final,
             ssd_w_in=ssd_w_in, ssd_w_out=ssd_w_out, moba_w_qkv=moba_w_qkv, moba_w_o=moba_w_o,
             lru_w_in=lru_w_in, lru_w_out=lru_w_out, mlp_w_up=mlp_w_up, mlp_w_down=mlp_w_down,
             ple_w_proj=ple_w_proj, ple_w_gate=ple_w_gate)
    Wb = _prep_weights(W)
    batch, seq, d = x_prompt.shape
    dec_batch, dec_seq, _ = x_sample.shape
    assert dec_seq == 1 and seq % MOBA_BLOCK == 0 and seq % LRU_CHUNK == 0

    ssd_args = lambda j: (ssd_conv_w[j], ssd_conv_b[j], ssd_dt_bias[j], ssd_a_log[j], ssd_d[j], ssd_norm[j])
    lru_args = lambda j: (lru_conv_w[j], lru_conv_b[j], lru_w_a[j], lru_b_a[j].reshape(-1), lru_w_x[j],
                          lru_b_x[j].reshape(-1), lru_lambda[j])

    out_p = _trunk(
        x_prompt.reshape(batch * seq, d), p_prompt.reshape(DEPTH, batch * seq, PLE_DIM), W, Wb, PROMPT_TILES,
        lambda j, z, xbc, dt: ssd_prompt(z, xbc, dt, *ssd_args(j), batch, seq),
        lambda j, q, k, v: moba_prompt(q, k, v, batch, seq),
        lambda j, gate, x: lru_prompt(gate, x, *lru_args(j), batch, seq))

    def moba_sample(j, q, k, v):
        sel = moba_gate(q, moba_block_sums(cache_k, j, page_table))
        return moba_decode(q, k, v, cache_k, cache_v, j, page_table, sel)

    def lru_sample(j, gate, x):
        y, conv, h = lru_step(gate, x, jnp.swapaxes(state_lru_conv[j], 0, 1), state_lru[j], *lru_args(j))
        return y, jnp.swapaxes(conv, 0, 1), h

    out_s = _trunk(
        x_sample.reshape(dec_batch, d), p_sample.reshape(DEPTH, dec_batch, PLE_DIM), W, Wb, SAMPLE_TILES,
        lambda j, z, xbc, dt: ssd_step(z, xbc, dt, state_ssd_conv[j], state_ssd[j], *ssd_args(j)),
        moba_sample, lru_sample)

    def shaped(out, b, s):
        y, ssd_conv, ssd_state, k, v, lru_conv, lru_state = out
        kv_shape = (-1, b, s, MOBA_HEADS, MOBA_HEAD_DIM)
        return (y.reshape(b, s, d), ssd_conv, ssd_state, k.reshape(kv_shape), v.reshape(kv_shape),
                lru_conv, lru_state)

    yp, *rest_p = shaped(out_p, batch, seq)
    ys, *rest_s = shaped(out_s, dec_batch, dec_seq)
    return (yp, ys, *rest_p, *rest_s)
```

```python
import functools
import math

import jax
import jax.numpy as jnp
from jax import lax
from jax.experimental import pallas as pl
from jax.experimental.pallas import tpu as pltpu

D_MODEL = 1024
DEPTH = 4
N_MIXERS = 3
PLE_DIM = 256
D_FF = 4 * D_MODEL
NORM_EPS = 1e-6
CONV_W = 4
PAGE_SIZE = 128

SSD_D_INNER = 2 * D_MODEL
SSD_HEAD_DIM = 64
SSD_HEADS = SSD_D_INNER // SSD_HEAD_DIM
SSD_GROUPS = 8
SSD_HEADS_PER_GROUP = SSD_HEADS // SSD_GROUPS
SSD_STATE = 128
SSD_BC_DIM = SSD_GROUPS * SSD_STATE
SSD_CONV_DIM = SSD_D_INNER + 2 * SSD_BC_DIM
SSD_GROUP_W = SSD_D_INNER // SSD_GROUPS
SSD_CHUNK = 128

MOBA_HEADS = 8
MOBA_HEAD_DIM = D_MODEL // MOBA_HEADS
MOBA_BLOCK = 256
MOBA_TOPK = 3
PAGES_PER_BLOCK = MOBA_BLOCK // PAGE_SIZE

LRU_WIDTH = D_MODEL
LRU_BLOCKS = 8
LRU_BLOCK_W = LRU_WIDTH // LRU_BLOCKS
LRU_C = 8.0
LRU_CHUNK = 256

LANES = 128
SUBLANES = 8
VMEM_LIMIT = 56 * 1024 * 1024
NEG_BIG = -1e30
LOG2E = 1.0 / math.log(2.0)
EXP2_CAP = 126.0

F32 = jnp.float32
BF16 = jnp.bfloat16


def _cparams(*sem):
    return pltpu.CompilerParams(dimension_semantics=sem, vmem_limit_bytes=VMEM_LIMIT)


def _rms(x, g):
    var = jnp.mean(x * x, axis=-1, keepdims=True)
    return (x * lax.rsqrt(var + NORM_EPS)) * g


def _sigmoid(x):
    return 0.5 + 0.5 * jnp.tanh(0.5 * x)


def _silu(x):
    hx = 0.5 * x
    return hx + hx * jnp.tanh(hx)


def _softplus(x):
    return jnp.maximum(x, 0.0) + jnp.log1p(jnp.exp(-jnp.abs(x)))


def _gelu_tanh(x):
    return 0.5 * x * (1.0 + jnp.tanh(math.sqrt(2.0 / math.pi) * (x + 0.044715 * (x * x * x))))


def _split3(x):
    hi = x.astype(BF16)
    r1 = x - hi.astype(F32)
    mid = r1.astype(BF16)
    lo = (r1 - mid.astype(F32)).astype(BF16)
    return hi, mid, lo


def _dot(a, b):
    return jnp.dot(a, b, preferred_element_type=F32)


def _dot_nt(a, b):
    return lax.dot_general(a, b, (((1,), (1,)), ((), ())), preferred_element_type=F32)


def _dot3_nt(a, b):
    ah, am, _ = _split3(a)
    bh, bm, _ = _split3(b)
    return _dot_nt(ah, bh) + (_dot_nt(ah, bm) + _dot_nt(am, bh))


def _rmsnorm_kernel(x_ref, g_ref, o_ref):
    o_ref[...] = _rms(x_ref[...], g_ref[...]).astype(o_ref.dtype)


def rmsnorm_bf16(x, g, *, tm):
    m, d = x.shape
    return pl.pallas_call(
        _rmsnorm_kernel,
        grid=(m // tm,),
        in_specs=[pl.BlockSpec((tm, d), lambda i: (i, 0)), pl.BlockSpec((1, d), lambda i: (0, 0))],
        out_specs=pl.BlockSpec((tm, d), lambda i: (i, 0)),
        out_shape=jax.ShapeDtypeStruct((m, d), BF16),
        compiler_params=_cparams("parallel"),
        name="rmsnorm",
    )(x, g.reshape(1, d))


def _linear_kernel(x_ref, w_ref, o_ref):
    o_ref[...] = _dot(x_ref[...], w_ref[...]).astype(o_ref.dtype)


def linear(x, w, *, tm, tn, out_dtype=F32):
    m, d = x.shape
    n = w.shape[1]
    return pl.pallas_call(
        _linear_kernel,
        grid=(m // tm, n // tn),
        in_specs=[pl.BlockSpec((tm, d), lambda i, j: (i, 0)),
                  pl.BlockSpec((d, tn), lambda i, j: (0, j))],
        out_specs=pl.BlockSpec((tm, tn), lambda i, j: (i, j)),
        out_shape=jax.ShapeDtypeStruct((m, n), out_dtype),
        compiler_params=_cparams("parallel", "arbitrary"),
        name="linear",
    )(x, w)


def _layer_tail_kernel(h_ref, y_ref, wo_ref, gm_ref, wu_ref, wd_ref, p_ref, gp_ref, wg_ref, wp_ref, gn_ref,
                       *refs, final):
    outs, (h1_ref, xn_ref, acc_ref) = refs[:-3], refs[-3:]
    f = pl.program_id(1)

    @pl.when(f == 0)
    def _():
        h1 = h_ref[...] + _dot(y_ref[...].astype(BF16), wo_ref[...])
        h1_ref[...] = h1
        xn_ref[...] = _rms(h1, gm_ref[...]).astype(BF16)
        acc_ref[...] = jnp.zeros_like(acc_ref)

    hid = jnp.square(jnp.maximum(_dot(xn_ref[...], wu_ref[...]), 0.0))
    acc_ref[...] += _dot(hid.astype(BF16), wd_ref[...])

    @pl.when(f == pl.num_programs(1) - 1)
    def _():
        h2 = h1_ref[...] + acc_ref[...]
        gate = _sigmoid(_dot(_rms(h2, gp_ref[...]).astype(BF16), wg_ref[...]))
        h3 = h2 + _dot(p_ref[...].astype(BF16), wp_ref[...]) * gate
        if final:
            outs[0][...] = _rms(h3, gn_ref[...])
        else:
            outs[0][...] = h3
            outs[1][...] = _rms(h3, gn_ref[...]).astype(BF16)


def layer_tail(h, y, w_out, g_mlp, w_up, w_down, p, g_ple, w_gate, w_proj, g_next, *, tm, tf, final):
    m, d = h.shape
    k = y.shape[1]
    ff = w_up.shape[1]
    pd = p.shape[1]
    row = lambda i, f: (i, 0)
    fixed = lambda i, f: (0, 0)
    once = pl.Buffered(1)
    out_specs = [pl.BlockSpec((tm, d), row)]
    out_shape = [jax.ShapeDtypeStruct((m, d), F32)]
    if not final:
        out_specs.append(pl.BlockSpec((tm, d), row))
        out_shape.append(jax.ShapeDtypeStruct((m, d), BF16))
    outs = pl.pallas_call(
        functools.partial(_layer_tail_kernel, final=final),
        grid=(m // tm, ff // tf),
        in_specs=[pl.BlockSpec((tm, d), row),
                  pl.BlockSpec((tm, k), row),
                  pl.BlockSpec((k, d), fixed, pipeline_mode=once),
                  pl.BlockSpec((1, d), fixed),
                  pl.BlockSpec((d, tf), lambda i, f: (0, f)),
                  pl.BlockSpec((tf, d), lambda i, f: (f, 0)),
                  pl.BlockSpec((tm, pd), row),
                  pl.BlockSpec((1, d), fixed),
                  pl.BlockSpec((d, d), fixed, pipeline_mode=once),
                  pl.BlockSpec((pd, d), fixed, pipeline_mode=once),
                  pl.BlockSpec((1, d), fixed)],
        out_specs=out_specs,
        out_shape=out_shape,
        scratch_shapes=[pltpu.VMEM((tm, d), F32), pltpu.VMEM((tm, d), BF16), pltpu.VMEM((tm, d), F32)],
        compiler_params=_cparams("parallel", "arbitrary"),
        name="layer_tail",
    )(h, y, w_out, g_mlp.reshape(1, d), w_up, w_down, p, g_ple.reshape(1, d), w_gate, w_proj, g_next.reshape(1, d))
    return outs[0] if final else tuple(outs)


def _ssd_head_select():
    rows = jnp.arange(LANES)[:, None]
    cols = jnp.arange(SSD_HEADS * LANES)[None, :] // LANES
    return ((rows % SSD_HEADS == cols) & (rows < 3 * SSD_HEADS)).astype(BF16)


def _ssd_chunk_kernel(z_ref, xbc_ref, dt_ref, cw_ref, cb_ref, dtb_ref, alog_ref, dskip_ref, ng_ref, sel_ref,
                      y_ref, conv_ref, state_ref, xp_ref, act_ref, acol_ref, st_ref, xbd_ref, sbd_ref):
    c = pl.program_id(1)
    t = SSD_CHUNK
    gw = SSD_GROUP_W
    hpg = SSD_HEADS_PER_GROUP
    hd = SSD_HEAD_DIM

    @pl.when(c == 0)
    def _():
        xp_ref[:, 0:SUBLANES, :] = jnp.zeros((xp_ref.shape[0], SUBLANES, LANES), F32)
        st_ref[...] = jnp.zeros_like(st_ref)
        xbd_ref[...] = jnp.zeros_like(xbd_ref)
        sbd_ref[...] = jnp.zeros_like(sbd_ref)

    for j in range(SSD_CONV_DIM // LANES):
        cols = slice(j * LANES, (j + 1) * LANES)
        xp_ref[j, SUBLANES:SUBLANES + t, :] = xbc_ref[:, cols]
        acc = xp_ref[j, SUBLANES - 3:SUBLANES - 3 + t, :] * cw_ref[0:1, cols]
        for k in range(1, CONV_W):
            acc = acc + xp_ref[j, SUBLANES - 3 + k:SUBLANES - 3 + k + t, :] * cw_ref[k:k + 1, cols]
        act_ref[:, cols] = _silu(acc + cb_ref[:, cols])
        xp_ref[j, 0:SUBLANES, :] = xbc_ref[t - SUBLANES:t, cols]
    conv_ref[0] = xbc_ref[t - SUBLANES:t, :]

    dtv = _softplus(dt_ref[...] + dtb_ref[...])
    da = dtv * (-jnp.exp(alog_ref[...]))
    row = lax.broadcasted_iota(jnp.int32, (t, t), 0)
    col = lax.broadcasted_iota(jnp.int32, (t, t), 1)
    causal = row >= col
    tril = causal.astype(BF16)
    d_hi, d_mid, d_lo = _split3(da)
    a2 = (_dot(tril, d_hi) + (_dot(tril, d_mid) + _dot(tril, d_lo))) * LOG2E
    arow_t = (a2 - jnp.log(dtv) * LOG2E).T

    lane = lax.broadcasted_iota(jnp.int32, (t, LANES), 1)
    a_hi, a_mid, a_lo = _split3(jnp.where(lane < SSD_HEADS, a2, 0.0))
    packed = (a_hi.astype(F32) + pltpu.roll(a_mid.astype(F32), SSD_HEADS, 1)
              + pltpu.roll(a_lo.astype(F32), 2 * SSD_HEADS, 1)).astype(BF16)
    acol_ref[...] = _dot(packed, sel_ref[...])

    lane_g = lax.broadcasted_iota(jnp.int32, (1, gw), 1) // hd
    for g in range(SSD_GROUPS):
        xs_g = act_ref[:, g * gw:(g + 1) * gw]
        b_g = act_ref[:, SSD_D_INNER + g * SSD_STATE:SSD_D_INNER + (g + 1) * SSD_STATE]
        c_g = act_ref[:, SSD_D_INNER + SSD_BC_DIM + g * SSD_STATE:
                      SSD_D_INNER + SSD_BC_DIM + (g + 1) * SSD_STATE]
        cb = jnp.where(causal, _dot_nt(c_g.astype(BF16), b_g.astype(BF16)), 0.0)
        b_gt = b_g.T
        st_g = st_ref[g]
        xs_b = xs_g.astype(BF16)
        for r in range(hpg):
            xbd_ref[g, r * t:(r + 1) * t, r * hd:(r + 1) * hd] = xs_b[:, r * hd:(r + 1) * hd]
        m_parts, d_parts, w_parts = [], [], []
        dec_g = jnp.zeros((1, gw), F32)
        for r in range(hpg):
            h = g * hpg + r
            acol = acol_ref[:, h * LANES:(h + 1) * LANES]
            arow = arow_t[h:h + 1, :]
            aend = acol[t - 1:t, :]
            m_parts.append((cb * jnp.exp2(jnp.minimum(acol - arow, EXP2_CAP))).astype(BF16))
            d_parts.append((c_g * jnp.exp2(acol)).astype(BF16))
            w_parts.append((b_gt * jnp.exp2(aend - arow)).astype(BF16))
            dec_g = jnp.where(lane_g == r, jnp.exp2(jnp.concatenate([aend, aend], axis=1)), dec_g)
        x_bd = xbd_ref[g]
        y_g = (_dot(jnp.concatenate(m_parts, axis=1), x_bd) + _dot(jnp.concatenate(d_parts, axis=1), sbd_ref[g])
               + xs_g * dskip_ref[:, g * gw:(g + 1) * gw])
        st_new = st_g * dec_g + _dot(jnp.concatenate(w_parts, axis=1), x_bd)
        st_ref[g] = st_new
        st_b = st_new.astype(BF16)
        for r in range(hpg):
            sbd_ref[g, r * SSD_STATE:(r + 1) * SSD_STATE, r * hd:(r + 1) * hd] = st_b[:, r * hd:(r + 1) * hd]
        yz = y_g * _silu(z_ref[:, g * gw:(g + 1) * gw])
        y_ref[:, g * gw:(g + 1) * gw] = _rms(yz, ng_ref[:, g * gw:(g + 1) * gw]).astype(y_ref.dtype)

    @pl.when(c == pl.num_programs(1) - 1)
    def _():
        for g in range(SSD_GROUPS):
            state_ref[0, g * hpg:(g + 1) * hpg] = st_ref[g].T.reshape(hpg, SSD_HEAD_DIM, SSD_STATE)


def ssd_prompt(z, xbc, dt, conv_w, conv_b, dt_bias, a_log, d_skip, norm_g, batch, seq):
    t = SSD_CHUNK
    nc = seq // t
    pad = LANES - SSD_HEADS
    rowblk = lambda b, c: (b * nc + c, 0)
    fixed = lambda b, c: (0, 0)
    y, conv_tail, state = pl.pallas_call(
        _ssd_chunk_kernel,
        grid=(batch, nc),
        in_specs=[pl.BlockSpec((t, SSD_D_INNER), rowblk),
                  pl.BlockSpec((t, SSD_CONV_DIM), rowblk),
                  pl.BlockSpec((t, LANES), rowblk),
                  pl.BlockSpec((CONV_W, SSD_CONV_DIM), fixed),
                  pl.BlockSpec((1, SSD_CONV_DIM), fixed),
                  pl.BlockSpec((1, LANES), fixed),
                  pl.BlockSpec((1, LANES), fixed),
                  pl.BlockSpec((1, SSD_D_INNER), fixed),
                  pl.BlockSpec((1, SSD_D_INNER), fixed),
                  pl.BlockSpec((LANES, SSD_HEADS * LANES), fixed)],
        out_specs=[pl.BlockSpec((t, SSD_D_INNER), rowblk),
                   pl.BlockSpec((1, SUBLANES, SSD_CONV_DIM), lambda b, c: (b, 0, 0)),
                   pl.BlockSpec((1, SSD_HEADS, SSD_HEAD_DIM, SSD_STATE), lambda b, c: (b, 0, 0, 0))],
        out_shape=[jax.ShapeDtypeStruct((batch * seq, SSD_D_INNER), BF16),
                   jax.ShapeDtypeStruct((batch, SUBLANES, SSD_CONV_DIM), F32),
                   jax.ShapeDtypeStruct((batch, SSD_HEADS, SSD_HEAD_DIM, SSD_STATE), F32)],
        scratch_shapes=[pltpu.VMEM((SSD_CONV_DIM // LANES, SUBLANES + t, LANES), F32),
                        pltpu.VMEM((t, SSD_CONV_DIM), F32),
                        pltpu.VMEM((t, SSD_HEADS * LANES), F32),
                        pltpu.VMEM((SSD_GROUPS, SSD_STATE, SSD_GROUP_W), F32),
                        pltpu.VMEM((SSD_GROUPS, SSD_HEADS_PER_GROUP * t, SSD_GROUP_W), BF16),
                        pltpu.VMEM((SSD_GROUPS, SSD_HEADS_PER_GROUP * SSD_STATE, SSD_GROUP_W), BF16)],
        compiler_params=_cparams("parallel", "arbitrary"),
        name="ssd_prompt",
    )(z, xbc, dt, conv_w, conv_b.reshape(1, -1), jnp.pad(dt_bias, (0, pad)).reshape(1, LANES),
      jnp.pad(a_log, (0, pad)).reshape(1, LANES), jnp.repeat(d_skip, SSD_HEAD_DIM).reshape(1, -1),
      norm_g.reshape(1, -1), _ssd_head_select())
    return y, conv_tail[:, SUBLANES - (CONV_W - 1):, :], state


MOBA_SPAN = 4 * MOBA_BLOCK
MOBA_HEADS_PER_STEP = 2


def _col_reduce(x, pair_op, reduce_fn):
    n = x.shape[0]
    while n > 8 * SUBLANES and n % 2 == 0:
        n //= 2
        x = pair_op(x[:n], x[n:])
    return reduce_fn(x, axis=0, keepdims=True)


def _moba_prompt_kernel(q_ref, k_ref, v_ref, o_ref, kb_ref, vt_ref, sel_ref, s_ref, p_ref):
    qi = pl.program_id(2)
    blk = MOBA_BLOCK
    seq = k_ref.shape[0]
    nb = seq // blk
    nbp = sel_ref.shape[2]
    hd = MOBA_HEAD_DIM
    heads = q_ref.shape[1] // hd

    def prepare(hh):
        cols = slice(hh * hd, (hh + 1) * hd)
        means = []
        for j in range(nb):
            kj = k_ref[j * blk:(j + 1) * blk, cols]
            means.append(jnp.sum(kj, axis=0, keepdims=True) * (1.0 / blk))
            kb_ref[hh, j * blk:(j + 1) * blk, :] = kj.astype(BF16)
        for j in range(seq // LANES):
            vt_ref[hh, :, j * LANES:(j + 1) * LANES] = v_ref[j * LANES:(j + 1) * LANES, cols].T.astype(BF16)
        if nbp > nb:
            means.append(jnp.zeros((nbp - nb, hd), F32))
        kmean = jnp.concatenate(means, axis=0)
        sub = lax.broadcasted_iota(jnp.int32, (nbp, blk), 0)
        for i in range(nb):
            gm = jnp.where(sub < i, _dot3_nt(kmean, q_ref[i * blk:(i + 1) * blk, cols]), -jnp.inf)
            rank = jnp.zeros((nbp, blk), F32)
            for m in range(i):
                gm_m = gm[m:m + 1, :]
                beats = (gm_m > gm) | ((gm_m == gm) & (sub > m))
                rank = rank + jnp.where(beats, 1.0, 0.0)
            chosen = ((sub < i) & (rank < MOBA_TOPK)) | (sub == i)
            sel_ref[hh, i] = jnp.where(chosen, 1.0, 0.0)

    @pl.when(qi == 0)
    def _():
        for hh in range(heads):
            prepare(hh)

    krow = lax.broadcasted_iota(jnp.int32, (blk, blk), 0)
    qcol = lax.broadcasted_iota(jnp.int32, (blk, blk), 1)
    causal = krow <= qcol

    span = min(MOBA_SPAN, seq)
    per_span = span // blk

    def attend_head(hh, n_keys):
        cols = slice(hh * hd, (hh + 1) * hd)
        q = q_ref[pl.ds(pl.multiple_of(qi * blk, blk), blk), cols]
        qb = (q * (hd ** -0.5)).astype(BF16)
        sel = sel_ref[hh, qi]
        m = None
        for j in range(n_keys // blk):
            rows = slice(j * blk, (j + 1) * blk)
            ok = (sel[j:j + 1, :] > 0.0) & (causal | (j < qi))
            s = jnp.where(ok, _dot_nt(kb_ref[hh, rows, :], qb), NEG_BIG)
            s_ref[hh, rows, :] = s
            bm = _col_reduce(s, jnp.maximum, jnp.max)
            m = bm if m is None else jnp.maximum(m, bm)
        l = jnp.zeros((1, blk), F32)
        for j in range(n_keys // blk):
            rows = slice(j * blk, (j + 1) * blk)
            p = jnp.exp(s_ref[hh, rows, :] - m)
            l = l + _col_reduce(p, jnp.add, jnp.sum)
            p_ref[hh, rows, :] = p.astype(BF16)
        acc = _dot(vt_ref[hh, :, 0:n_keys], p_ref[hh, 0:n_keys, :])
        o_ref[:, cols] = (acc / l).T.astype(o_ref.dtype)

    def attend(n_keys):
        for hh in range(heads):
            attend_head(hh, n_keys)

    for sp in range(seq // span):
        in_span = (qi >= sp * per_span) & (qi < (sp + 1) * per_span)
        pl.when(in_span)(functools.partial(attend, (sp + 1) * span))


def moba_prompt(q, k, v, batch, seq):
    nb = seq // MOBA_BLOCK
    hd = MOBA_HEAD_DIM
    hps = MOBA_HEADS_PER_STEP
    wide = lambda b, h, i: (b, h)
    return pl.pallas_call(
        _moba_prompt_kernel,
        grid=(batch, MOBA_HEADS // hps, nb),
        in_specs=[pl.BlockSpec((seq, hps * hd), wide),
                  pl.BlockSpec((seq, hps * hd), wide),
                  pl.BlockSpec((seq, hps * hd), wide)],
        out_specs=pl.BlockSpec((MOBA_BLOCK, hps * hd), lambda b, h, i: (b * nb + i, h)),
        out_shape=jax.ShapeDtypeStruct((batch * seq, D_MODEL), BF16),
        scratch_shapes=[pltpu.VMEM((hps, seq, hd), BF16),
                        pltpu.VMEM((hps, hd, seq), BF16),
                        pltpu.VMEM((hps, nb, -(-nb // SUBLANES) * SUBLANES, MOBA_BLOCK), F32),
                        pltpu.VMEM((hps, seq, MOBA_BLOCK), F32),
                        pltpu.VMEM((hps, seq, MOBA_BLOCK), BF16)],
        compiler_params=_cparams("parallel", "parallel", "arbitrary"),
        name="moba_prompt",
    )(q, k, v)


def _lru_gates(xc, wa, ba, wx, bx, lam):
    xb = xc.astype(BF16)
    r = _sigmoid(_dot(xb, wa) + ba)
    i = _sigmoid(_dot(xb, wx) + bx)
    log_a = (-LRU_C) * r * _softplus(-lam)
    a = jnp.exp(log_a)
    u = jnp.sqrt(-jnp.tanh(log_a) * (a * a + 1.0)) * (i * xc)
    return a, u


def _lru_chunk_kernel(gate_ref, x_ref, cw_ref, cb_ref, wa_ref, ba_ref, wx_ref, bx_ref, lam_ref,
                      y_ref, conv_ref, hlast_ref, xp_ref, a_ref, u_ref, h_ref):
    c = pl.program_id(1)
    t = LRU_CHUNK
    w = LRU_BLOCK_W

    @pl.when(c == 0)
    def _():
        xp_ref[:, 0:SUBLANES, :] = jnp.zeros((LRU_BLOCKS, SUBLANES, w), F32)
        h_ref[...] = jnp.zeros_like(h_ref)

    for n in range(LRU_BLOCKS):
        cols = slice(n * w, (n + 1) * w)
        xp_ref[n, SUBLANES:SUBLANES + t, :] = x_ref[:, cols]
        acc = xp_ref[n, SUBLANES - 3:SUBLANES - 3 + t, :] * cw_ref[0:1, cols]
        for k in range(1, CONV_W):
            acc = acc + xp_ref[n, SUBLANES - 3 + k:SUBLANES - 3 + k + t, :] * cw_ref[k:k + 1, cols]
        xc = acc + cb_ref[:, cols]
        a, u = _lru_gates(xc, wa_ref[n], ba_ref[:, cols], wx_ref[n], bx_ref[:, cols], lam_ref[:, cols])
        a_ref[:, cols] = a
        u_ref[:, cols] = u
        xp_ref[n, 0:SUBLANES, :] = x_ref[t - SUBLANES:t, cols]
    conv_ref[0] = x_ref[t - SUBLANES:t, :]

    srow = lax.broadcasted_iota(jnp.int32, (SUBLANES, LRU_WIDTH), 0)

    def tile(i, h):
        rows = pl.ds(pl.multiple_of(i * SUBLANES, SUBLANES), SUBLANES)
        a = a_ref[rows, :]
        u = u_ref[rows, :]
        for d in (1, 2, 4):
            keep = srow >= d
            a_sh = jnp.where(keep, pltpu.roll(a, d, 0), 1.0)
            u_sh = jnp.where(keep, pltpu.roll(u, d, 0), 0.0)
            u = a * u_sh + u
            a = a * a_sh
        hs = a * h + u
        u_ref[rows, :] = hs
        return jnp.broadcast_to(hs[SUBLANES - 1:SUBLANES, :], (SUBLANES, LRU_WIDTH))

    h = lax.fori_loop(0, t // SUBLANES, tile, h_ref[...])
    h_ref[...] = h
    hlast_ref[0] = h
    y_ref[...] = (_gelu_tanh(gate_ref[...]) * u_ref[...]).astype(y_ref.dtype)


def lru_prompt(gate_br, x_br, conv_w, conv_b, w_a, b_a, w_x, b_x, lam, batch, seq):
    t = LRU_CHUNK
    nc = seq // t
    wd = LRU_WIDTH
    rowblk = lambda b, c: (b * nc + c, 0)
    fixed = lambda b, c: (0, 0)
    fixed3 = lambda b, c: (0, 0, 0)
    perb = lambda b, c: (b, 0, 0)
    y, conv_tail, h_last = pl.pallas_call(
        _lru_chunk_kernel,
        grid=(batch, nc),
        in_specs=[pl.BlockSpec((t, wd), rowblk),
                  pl.BlockSpec((t, wd), rowblk),
                  pl.BlockSpec((CONV_W, wd), fixed),
                  pl.BlockSpec((1, wd), fixed),
                  pl.BlockSpec((LRU_BLOCKS, LRU_BLOCK_W, LRU_BLOCK_W), fixed3),
                  pl.BlockSpec((1, wd), fixed),
                  pl.BlockSpec((LRU_BLOCKS, LRU_BLOCK_W, LRU_BLOCK_W), fixed3),
                  pl.BlockSpec((1, wd), fixed),
                  pl.BlockSpec((1, wd), fixed)],
        out_specs=[pl.BlockSpec((t, wd), rowblk),
                   pl.BlockSpec((1, SUBLANES, wd), perb),
                   pl.BlockSpec((1, SUBLANES, wd), perb)],
        out_shape=[jax.ShapeDtypeStruct((batch * seq, wd), BF16),
                   jax.ShapeDtypeStruct((batch, SUBLANES, wd), F32),
                   jax.ShapeDtypeStruct((batch, SUBLANES, wd), F32)],
        scratch_shapes=[pltpu.VMEM((LRU_BLOCKS, SUBLANES + t, LRU_BLOCK_W), F32),
                        pltpu.VMEM((t, wd), F32),
                        pltpu.VMEM((t, wd), F32),
                        pltpu.VMEM((SUBLANES, wd), F32)],
        compiler_params=_cparams("parallel", "arbitrary"),
        name="lru_prompt",
    )(gate_br, x_br, conv_w, conv_b.reshape(1, wd), w_a.astype(BF16), b_a.reshape(1, wd),
      w_x.astype(BF16), b_x.reshape(1, wd), lam.reshape(1, wd))
    return y, conv_tail[:, SUBLANES - (CONV_W - 1):, :], h_last[:, 0, :]


def _ssd_step_kernel(z_ref, xbc_ref, dt_ref, cs_ref, st_ref, cw_ref, cb_ref, dtb_ref, alog_ref, dskip_ref, ng_ref,
                     y_ref, cso_ref, sto_ref):
    gw = SSD_GROUP_W
    hpg = SSD_HEADS_PER_GROUP
    x = xbc_ref[0]
    buf = cs_ref[0]
    conv = x * cw_ref[CONV_W - 1:CONV_W, :]
    for k in range(CONV_W - 1):
        conv = conv + buf[k:k + 1, :] * cw_ref[k:k + 1, :]
    act = _silu(conv + cb_ref[...])
    cso_ref[0, 0:CONV_W - 2, :] = buf[1:CONV_W - 1, :]
    cso_ref[0, CONV_W - 2:CONV_W - 1, :] = x

    dtv = _softplus(dt_ref[0] + dtb_ref[...])
    dec = jnp.exp(dtv * (-jnp.exp(alog_ref[...])))
    lane_g = lax.broadcasted_iota(jnp.int32, (1, gw), 1) // SSD_HEAD_DIM
    first_row = lax.broadcasted_iota(jnp.int32, (SUBLANES, gw), 0) == 0
    ys = []
    for g in range(SSD_GROUPS):
        xs_g = act[:, g * gw:(g + 1) * gw]
        b_g = act[:, SSD_D_INNER + g * SSD_STATE:SSD_D_INNER + (g + 1) * SSD_STATE]
        c_g = act[:, SSD_D_INNER + SSD_BC_DIM + g * SSD_STATE:SSD_D_INNER + SSD_BC_DIM + (g + 1) * SSD_STATE]
        dt_e = jnp.zeros((1, gw), F32)
        dec_rows = []
        for r in range(hpg):
            h = g * hpg + r
            dt_e = jnp.where(lane_g == r, dtv[:, h:h + 1], dt_e)
            dec_rows.append(jnp.broadcast_to(dec[:, h:h + 1], (SSD_HEAD_DIM, SSD_STATE)))
        xdt = xs_g * dt_e
        x8 = jnp.where(first_row, jnp.broadcast_to(xdt, (SUBLANES, gw)), 0.0).astype(BF16)
        b8 = jnp.broadcast_to(b_g, (SUBLANES, SSD_STATE)).astype(BF16)
        outer = lax.dot_general(x8, b8, (((0,), (0,)), ((), ())), preferred_element_type=F32)
        old = st_ref[0, g * hpg:(g + 1) * hpg].reshape(gw, SSD_STATE)
        new = old * jnp.concatenate(dec_rows, axis=0) + outer
        sto_ref[0, g * hpg:(g + 1) * hpg] = new.reshape(hpg, SSD_HEAD_DIM, SSD_STATE)
        c8 = jnp.broadcast_to(c_g, (SUBLANES, SSD_STATE)).astype(BF16)
        y_g = _dot_nt(c8, new.astype(BF16))[0:1, :] + xs_g * dskip_ref[:, g * gw:(g + 1) * gw]
        yz = y_g * _silu(z_ref[0][:, g * gw:(g + 1) * gw])
        ys.append(_rms(yz, ng_ref[:, g * gw:(g + 1) * gw]))
    y_ref[0] = jnp.concatenate(ys, axis=1)


def ssd_step(z, xbc, dt, conv_state, ssm_state, conv_w, conv_b, dt_bias, a_log, d_skip, norm_g):
    nb = z.shape[0]
    pad = LANES - SSD_HEADS
    per3 = lambda b: (b, 0, 0)
    fixed = lambda b: (0, 0)
    y, conv_new, state_new = pl.pallas_call(
        _ssd_step_kernel,
        grid=(nb,),
        in_specs=[pl.BlockSpec((1, 1, SSD_D_INNER), per3),
                  pl.BlockSpec((1, 1, SSD_CONV_DIM), per3),
                  pl.BlockSpec((1, 1, LANES), per3),
                  pl.BlockSpec((1, CONV_W - 1, SSD_CONV_DIM), per3),
                  pl.BlockSpec((1, SSD_HEADS, SSD_HEAD_DIM, SSD_STATE), lambda b: (b, 0, 0, 0)),
                  pl.BlockSpec((CONV_W, SSD_CONV_DIM), fixed),
                  pl.BlockSpec((1, SSD_CONV_DIM), fixed),
                  pl.BlockSpec((1, LANES), fixed),
                  pl.BlockSpec((1, LANES), fixed),
                  pl.BlockSpec((1, SSD_D_INNER), fixed),
                  pl.BlockSpec((1, SSD_D_INNER), fixed)],
        out_specs=[pl.BlockSpec((1, 1, SSD_D_INNER), per3),
                   pl.BlockSpec((1, CONV_W - 1, SSD_CONV_DIM), per3),
                   pl.BlockSpec((1, SSD_HEADS, SSD_HEAD_DIM, SSD_STATE), lambda b: (b, 0, 0, 0))],
        out_shape=[jax.ShapeDtypeStruct((nb, 1, SSD_D_INNER), F32),
                   jax.ShapeDtypeStruct((nb, CONV_W - 1, SSD_CONV_DIM), F32),
                   jax.ShapeDtypeStruct((nb, SSD_HEADS, SSD_HEAD_DIM, SSD_STATE), F32)],
        compiler_params=_cparams("parallel"),
        name="ssd_step",
    )(z.reshape(nb, 1, -1), xbc.reshape(nb, 1, -1), dt.reshape(nb, 1, -1), conv_state, ssm_state,
      conv_w, conv_b.reshape(1, -1), jnp.pad(dt_bias, (0, pad)).reshape(1, LANES),
      jnp.pad(a_log, (0, pad)).reshape(1, LANES), jnp.repeat(d_skip, SSD_HEAD_DIM).reshape(1, -1),
      norm_g.reshape(1, -1))
    return y.reshape(nb, SSD_D_INNER), conv_new, state_new


def _lru_step_kernel(gate_ref, x_ref, cs_ref, h0_ref, cw_ref, cb_ref, wa_ref, ba_ref, wx_ref, bx_ref, lam_ref,
                     y_ref, cso_ref, ho_ref):
    w = LRU_BLOCK_W
    x = x_ref[...]
    xc = x * cw_ref[CONV_W - 1:CONV_W, :] + cb_ref[...]
    for k in range(CONV_W - 1):
        xc = xc + cs_ref[k] * cw_ref[k:k + 1, :]
    for k in range(CONV_W - 2):
        cso_ref[k] = cs_ref[k + 1]
    cso_ref[CONV_W - 2] = x
    for n in range(LRU_BLOCKS):
        cols = slice(n * w, (n + 1) * w)
        a, u = _lru_gates(xc[:, cols], wa_ref[n], ba_ref[:, cols], wx_ref[n], bx_ref[:, cols], lam_ref[:, cols])
        h = a * h0_ref[:, cols] + u
        ho_ref[:, cols] = h
        y_ref[:, cols] = _gelu_tanh(gate_ref[:, cols]) * h


def lru_step(gate_br, x_br, conv_state, h0, conv_w, conv_b, w_a, b_a, w_x, b_x, lam):
    nb, wd = x_br.shape
    return pl.pallas_call(
        _lru_step_kernel,
        out_shape=[jax.ShapeDtypeStruct((nb, wd), F32),
                   jax.ShapeDtypeStruct((CONV_W - 1, nb, wd), F32),
                   jax.ShapeDtypeStruct((nb, wd), F32)],
        compiler_params=pltpu.CompilerParams(vmem_limit_bytes=VMEM_LIMIT),
        name="lru_step",
    )(gate_br, x_br, conv_state, h0, conv_w, conv_b.reshape(1, wd), w_a.astype(BF16), b_a.reshape(1, wd),
      w_x.astype(BF16), b_x.reshape(1, wd), lam.reshape(1, wd))


MOBA_SUM_PAGES = 16


def _moba_block_sums_kernel(tbl_ref, *refs):
    del tbl_ref
    pages, o_ref = refs[:-1], refs[-1]
    for k in range(len(pages) // PAGES_PER_BLOCK):
        s = jnp.sum(pages[PAGES_PER_BLOCK * k][...], axis=0)
        for j in range(1, PAGES_PER_BLOCK):
            s = s + jnp.sum(pages[PAGES_PER_BLOCK * k + j][...], axis=0)
        o_ref[0, k] = s


def moba_block_sums(pool, layer, table):
    nb, n_pages = table.shape
    steps = n_pages // MOBA_SUM_PAGES
    blocks_per_step = MOBA_SUM_PAGES // PAGES_PER_BLOCK
    page_shape = pool.shape[2:]

    def page_spec(k):
        return pl.BlockSpec((None, None) + page_shape,
                            lambda b, c, tbl: (layer, tbl[b * n_pages + c * MOBA_SUM_PAGES + k], 0, 0, 0))

    return pl.pallas_call(
        _moba_block_sums_kernel,
        grid_spec=pltpu.PrefetchScalarGridSpec(
            num_scalar_prefetch=1,
            grid=(nb, steps),
            in_specs=[page_spec(k) for k in range(MOBA_SUM_PAGES)],
            out_specs=pl.BlockSpec((1, blocks_per_step) + page_shape[1:], lambda b, c, tbl: (b, c, 0, 0))),
        out_shape=jax.ShapeDtypeStruct((nb, n_pages // PAGES_PER_BLOCK) + page_shape[1:], F32),
        compiler_params=_cparams("parallel", "arbitrary"),
        name="moba_block_sums",
    )(table.reshape(-1), *([pool] * MOBA_SUM_PAGES))


def _moba_gate_kernel(q_ref, bs_ref, sel_ref):
    n_blocks = bs_ref.shape[1]
    q = q_ref[0]
    gates = [jnp.sum((bs_ref[0, n] * (1.0 / MOBA_BLOCK)) * q, axis=-1, keepdims=True) for n in range(n_blocks)]
    lane = lax.broadcasted_iota(jnp.int32, (MOBA_HEADS, LANES), 1)
    out = jnp.zeros((MOBA_HEADS, LANES), F32)
    for k in range(MOBA_TOPK):
        best = functools.reduce(jnp.maximum, gates)
        idx = jnp.full((MOBA_HEADS, 1), float(n_blocks), F32)
        for n in reversed(range(n_blocks)):
            idx = jnp.where(gates[n] == best, float(n), idx)
        out = jnp.where(lane == k, idx, out)
        gates = [jnp.where(idx == float(n), -jnp.inf, gates[n]) for n in range(n_blocks)]
    sel_ref[0] = out.astype(jnp.int32)


def moba_gate(q, block_sums):
    nb = q.shape[0]
    n_blocks = block_sums.shape[1]
    return pl.pallas_call(
        _moba_gate_kernel,
        grid=(nb,),
        in_specs=[pl.BlockSpec((1, MOBA_HEADS, MOBA_HEAD_DIM), lambda b: (b, 0, 0)),
                  pl.BlockSpec((1, n_blocks, MOBA_HEADS, MOBA_HEAD_DIM), lambda b: (b, 0, 0, 0))],
        out_specs=pl.BlockSpec((1, MOBA_HEADS, LANES), lambda b: (b, 0, 0)),
        out_shape=jax.ShapeDtypeStruct((nb, MOBA_HEADS, LANES), jnp.int32),
        compiler_params=_cparams("parallel"),
        name="moba_gate",
    )(q.reshape(nb, MOBA_HEADS, MOBA_HEAD_DIM), block_sums)


def _moba_decode_kernel(tbl_ref, sel_ref, q_ref, kn_ref, vn_ref, kpool_ref, vpool_ref, o_ref, kbuf, vbuf, sem,
                        *, layer, n_table_pages):
    b, h = pl.program_id(0), pl.program_id(1)
    nh = pl.num_programs(1)
    step = b * nh + h
    slot = lax.rem(step, 2)
    n_sel = MOBA_TOPK * PAGES_PER_BLOCK

    def page_copies(bb, hh, sl):
        out = []
        for k in range(MOBA_TOPK):
            blk = sel_ref[(bb * nh + hh) * MOBA_TOPK + k]
            for s in range(PAGES_PER_BLOCK):
                page = tbl_ref[bb * n_table_pages + blk * PAGES_PER_BLOCK + s]
                i = k * PAGES_PER_BLOCK + s
                out.append(pltpu.make_async_copy(kpool_ref.at[layer, page, :, hh, :], kbuf.at[sl, i], sem.at[sl, 0, i]))
                out.append(pltpu.make_async_copy(vpool_ref.at[layer, page, :, hh, :], vbuf.at[sl, i], sem.at[sl, 1, i]))
        return out

    @pl.when(step == 0)
    def _():
        for c in page_copies(b, h, 0):
            c.start()

    @pl.when(step + 1 < pl.num_programs(0) * nh)
    def _():
        wrap = h + 1 == nh
        for c in page_copies(jnp.where(wrap, b + 1, b), jnp.where(wrap, 0, h + 1), 1 - slot):
            c.start()

    for c in page_copies(b, h, slot):
        c.wait()

    scale = MOBA_HEAD_DIM ** -0.5
    q = q_ref[0]
    q8 = jnp.broadcast_to(q, (SUBLANES, MOBA_HEAD_DIM)).astype(BF16)
    k_cat = kbuf[slot].reshape(n_sel * PAGE_SIZE, MOBA_HEAD_DIM).astype(BF16)
    v_cat = vbuf[slot].reshape(n_sel * PAGE_SIZE, MOBA_HEAD_DIM).astype(BF16)
    s = _dot_nt(q8, k_cat) * scale
    s_new = jnp.sum(q * kn_ref[0], axis=1, keepdims=True) * scale
    m = jnp.maximum(jnp.max(s, axis=1, keepdims=True), s_new)
    p = jnp.exp(s - m)
    p_new = jnp.exp(s_new - m)
    denom = jnp.sum(p, axis=1, keepdims=True) + p_new
    o = (_dot(p.astype(BF16), v_cat) + p_new * vn_ref[0]) / denom
    o_ref[0] = o[0:1, :]


def moba_decode(q, k_new, v_new, k_pool, v_pool, layer, table, sel):
    nb = q.shape[0]
    n_pages = table.shape[1]
    hd = MOBA_HEAD_DIM
    n_sel = MOBA_TOPK * PAGES_PER_BLOCK
    tok = pl.BlockSpec((1, 1, hd), lambda b, h, tbl, sl: (b, 0, h))
    hbm = pl.BlockSpec(memory_space=pl.ANY)
    out = pl.pallas_call(
        functools.partial(_moba_decode_kernel, layer=layer, n_table_pages=n_pages),
        grid_spec=pltpu.PrefetchScalarGridSpec(
            num_scalar_prefetch=2,
            grid=(nb, MOBA_HEADS),
            in_specs=[tok, tok, tok, hbm, hbm],
            out_specs=tok,
            scratch_shapes=[pltpu.VMEM((2, n_sel, PAGE_SIZE, hd), F32),
                            pltpu.VMEM((2, n_sel, PAGE_SIZE, hd), F32),
                            pltpu.SemaphoreType.DMA((2, 2, n_sel))]),
        out_shape=jax.ShapeDtypeStruct((nb, 1, D_MODEL), F32),
        compiler_params=_cparams("arbitrary", "arbitrary"),
        name="moba_decode",
    )(table.reshape(-1), sel[:, :, :MOBA_TOPK].reshape(-1),
      q.reshape(nb, 1, D_MODEL), k_new.reshape(nb, 1, D_MODEL), v_new.reshape(nb, 1, D_MODEL), k_pool, v_pool)
    return out.reshape(nb, D_MODEL)


def _prep_weights(W):
    bf = lambda a: a.astype(BF16)
    w_in = W["ssd_w_in"]
    zx = SSD_D_INNER + SSD_CONV_DIM
    return dict(
        ssd_w_z=bf(w_in[:, :, :SSD_D_INNER]),
        ssd_w_xbc=bf(w_in[:, :, SSD_D_INNER:zx]),
        ssd_w_dt=bf(jnp.pad(w_in[:, :, zx:], ((0, 0), (0, 0), (0, LANES - SSD_HEADS)))),
        ssd_w_out=bf(W["ssd_w_out"]),
        moba_w_q=bf(W["moba_w_qkv"][:, :, :D_MODEL]),
        moba_w_k=bf(W["moba_w_qkv"][:, :, D_MODEL:2 * D_MODEL]),
        moba_w_v=bf(W["moba_w_qkv"][:, :, 2 * D_MODEL:]),
        moba_w_o=bf(W["moba_w_o"]),
        lru_w_gate=bf(W["lru_w_in"][:, :, :LRU_WIDTH]),
        lru_w_x=bf(W["lru_w_in"][:, :, LRU_WIDTH:]),
        lru_w_out=bf(W["lru_w_out"]),
        mlp_w_up=bf(W["mlp_w_up"]), mlp_w_down=bf(W["mlp_w_down"]),
        ple_w_proj=bf(W["ple_w_proj"]), ple_w_gate=bf(W["ple_w_gate"]),
    )


def _trunk(h, p, W, Wb, tiles, ssd_fn, moba_fn, lru_fn):
    tm_proj, tn, tm, tf = tiles
    ssd_out, moba_out, lru_out = [], [], []
    xn = rmsnorm_bf16(h, W["norm_mix"][0], tm=tm_proj)
    for i in range(DEPTH):
        j, kind = i // N_MIXERS, i % N_MIXERS
        proj = lambda name: linear(xn, Wb[name][j], tm=tm_proj, tn=min(tn, Wb[name].shape[-1]))
        if kind == 0:
            y, conv, state = ssd_fn(j, proj("ssd_w_z"), proj("ssd_w_xbc"), proj("ssd_w_dt"))
            ssd_out.append((conv, state))
            w_out = Wb["ssd_w_out"][j]
        elif kind == 1:
            q, k, v = proj("moba_w_q"), proj("moba_w_k"), proj("moba_w_v")
            moba_out.append((k, v))
            y, w_out = moba_fn(j, q, k, v), Wb["moba_w_o"][j]
        else:
            y, conv, state = lru_fn(j, proj("lru_w_gate"), proj("lru_w_x"))
            lru_out.append((conv, state))
            w_out = Wb["lru_w_out"][j]
        final = i == DEPTH - 1
        g_next = W["norm_final"] if final else W["norm_mix"][i + 1]
        out = layer_tail(h, y, w_out, W["norm_mlp"][i], Wb["mlp_w_up"][i], Wb["mlp_w_down"][i], p[i],
                         W["norm_ple"][i], Wb["ple_w_gate"][i], Wb["ple_w_proj"][i], g_next,
                         tm=tm, tf=tf, final=final)
        h, xn = (out, None) if final else out
    stack = lambda pairs, n: jnp.stack([pr[n] for pr in pairs])
    return (h, stack(ssd_out, 0), stack(ssd_out, 1), stack(moba_out, 0), stack(moba_out, 1),
            stack(lru_out, 0), stack(lru_out, 1))


PROMPT_TILES = (1024, 1024, 512, 1024)
SAMPLE_TILES = (32, 1024, 32, 1024)


def kernel(x_prompt, x_sample, state_ssd_conv, state_ssd, cache_k, cache_v, page_table, state_lru_conv,
           state_lru, p_prompt, p_sample, norm_mix, norm_mlp, norm_ple, norm_final, ssd_w_in, ssd_conv_w,
           ssd_conv_b, ssd_dt_bias, ssd_a_log, ssd_d, ssd_norm, ssd_w_out, moba_w_qkv, moba_w_o, lru_w_in,
           lru_conv_w, lru_conv_b, lru_w_a, lru_b_a, lru_w_x, lru_b_x, lru_lambda, lru_w_out, mlp_w_up,
           mlp_w_down, ple_w_proj, ple_w_gate):
    W = dict(norm_mix=norm_mix, norm_mlp=norm_mlp, norm_ple=norm_ple, norm_final=norm_final,
             ssd_w_in=ssd_w_in, ssd_w_out=ssd_w_out, moba_w_qkv=moba_w_qkv, moba_w_o=moba_w_o,
             lru_w_in=lru_w_in, lru_w_out=lru_w_out, mlp_w_up=mlp_w_up, mlp_w_down=mlp_w_down,
             ple_w_proj=ple_w_proj, ple_w_gate=ple_w_gate)
    Wb = _prep_weights(W)
    batch, seq, d = x_prompt.shape
    dec_batch, dec_seq, _ = x_sample.shape
    assert dec_seq == 1 and seq % MOBA_BLOCK == 0 and seq % LRU_CHUNK == 0

    ssd_args = lambda j: (ssd_conv_w[j], ssd_conv_b[j], ssd_dt_bias[j], ssd_a_log[j], ssd_d[j], ssd_norm[j])
    lru_args = lambda j: (lru_conv_w[j], lru_conv_b[j], lru_w_a[j], lru_b_a[j].reshape(-1), lru_w_x[j],
                          lru_b_x[j].reshape(-1), lru_lambda[j])

    out_p = _trunk(
        x_prompt.reshape(batch * seq, d), p_prompt.reshape(DEPTH, batch * seq, PLE_DIM), W, Wb, PROMPT_TILES,
        lambda j, z, xbc, dt: ssd_prompt(z, xbc, dt, *ssd_args(j), batch, seq),
        lambda j, q, k, v: moba_prompt(q, k, v, batch, seq),
        lambda j, gate, x: lru_prompt(gate, x, *lru_args(j), batch, seq))

    def moba_sample(j, q, k, v):
        sel = moba_gate(q, moba_block_sums(cache_k, j, page_table))
        return moba_decode(q, k, v, cache_k, cache_v, j, page_table, sel)

    def lru_sample(j, gate, x):
        y, conv, h = lru_step(gate, x, jnp.swapaxes(state_lru_conv[j], 0, 1), state_lru[j], *lru_args(j))
        return y, jnp.swapaxes(conv, 0, 1), h

    out_s = _trunk(
        x_sample.reshape(dec_batch, d), p_sample.reshape(DEPTH, dec_batch, PLE_DIM), W, Wb, SAMPLE_TILES,
        lambda j, z, xbc, dt: ssd_step(z, xbc, dt, state_ssd_conv[j], state_ssd[j], *ssd_args(j)),
        moba_sample, lru_sample)

    def shaped(out, b, s):
        y, ssd_conv, ssd_state, k, v, lru_conv, lru_state = out
        kv_shape = (-1, b, s, MOBA_HEADS, MOBA_HEAD_DIM)
        return (y.reshape(b, s, d), ssd_conv, ssd_state, k.reshape(kv_shape), v.reshape(kv_shape),
                lru_conv, lru_state)

    yp, *rest_p = shaped(out_p, batch, seq)
    ys, *rest_s = shaped(out_s, dec_batch, dec_seq)
    return (yp, ys, *rest_p, *rest_s)
```

```python
import functools
import math

import jax
import jax.numpy as jnp
from jax import lax
from jax.experimental import pallas as pl
from jax.experimental.pallas import tpu as pltpu

D_MODEL = 1024
DEPTH = 4
N_MIXERS = 3
PLE_DIM = 256
D_FF = 4 * D_MODEL
NORM_EPS = 1e-6
CONV_W = 4
PAGE_SIZE = 128

SSD_D_INNER = 2 * D_MODEL
SSD_HEAD_DIM = 64
SSD_HEADS = SSD_D_INNER // SSD_HEAD_DIM
SSD_GROUPS = 8
SSD_HEADS_PER_GROUP = SSD_HEADS // SSD_GROUPS
SSD_STATE = 128
SSD_BC_DIM = SSD_GROUPS * SSD_STATE
SSD_CONV_DIM = SSD_D_INNER + 2 * SSD_BC_DIM
SSD_GROUP_W = SSD_D_INNER // SSD_GROUPS
SSD_CHUNK = 128

MOBA_HEADS = 8
MOBA_HEAD_DIM = D_MODEL // MOBA_HEADS
MOBA_BLOCK = 256
MOBA_TOPK = 3
PAGES_PER_BLOCK = MOBA_BLOCK // PAGE_SIZE

LRU_WIDTH = D_MODEL
LRU_BLOCKS = 8
LRU_BLOCK_W = LRU_WIDTH // LRU_BLOCKS
LRU_C = 8.0
LRU_CHUNK = 256

LANES = 128
SUBLANES = 8
VMEM_LIMIT = 56 * 1024 * 1024
NEG_BIG = -1e30
LOG2E = 1.0 / math.log(2.0)
EXP2_CAP = 126.0

F32 = jnp.float32
BF16 = jnp.bfloat16


def _cparams(*sem):
    return pltpu.CompilerParams(dimension_semantics=sem, vmem_limit_bytes=VMEM_LIMIT)


def _rms(x, g):
    var = jnp.mean(x * x, axis=-1, keepdims=True)
    return (x * lax.rsqrt(var + NORM_EPS)) * g


def _sigmoid(x):
    return 0.5 + 0.5 * jnp.tanh(0.5 * x)


def _silu(x):
    hx = 0.5 * x
    return hx + hx * jnp.tanh(hx)


def _softplus(x):
    return jnp.maximum(x, 0.0) + jnp.log1p(jnp.exp(-jnp.abs(x)))


def _gelu_tanh(x):
    return 0.5 * x * (1.0 + jnp.tanh(math.sqrt(2.0 / math.pi) * (x + 0.044715 * (x * x * x))))


def _split3(x):
    hi = x.astype(BF16)
    r1 = x - hi.astype(F32)
    mid = r1.astype(BF16)
    lo = (r1 - mid.astype(F32)).astype(BF16)
    return hi, mid, lo


def _dot(a, b):
    return jnp.dot(a, b, preferred_element_type=F32)


def _dot_nt(a, b):
    return lax.dot_general(a, b, (((1,), (1,)), ((), ())), preferred_element_type=F32)


def _dot3_nt(a, b):
    ah, am, _ = _split3(a)
    bh, bm, _ = _split3(b)
    return _dot_nt(ah, bh) + (_dot_nt(ah, bm) + _dot_nt(am, bh))


def _rmsnorm_kernel(x_ref, g_ref, o_ref):
    o_ref[...] = _rms(x_ref[...], g_ref[...]).astype(o_ref.dtype)


def rmsnorm_bf16(x, g, *, tm):
    m, d = x.shape
    return pl.pallas_call(
        _rmsnorm_kernel,
        grid=(m // tm,),
        in_specs=[pl.BlockSpec((tm, d), lambda i: (i, 0)), pl.BlockSpec((1, d), lambda i: (0, 0))],
        out_specs=pl.BlockSpec((tm, d), lambda i: (i, 0)),
        out_shape=jax.ShapeDtypeStruct((m, d), BF16),
        compiler_params=_cparams("parallel"),
        name="rmsnorm",
    )(x, g.reshape(1, d))


def _multi_linear_kernel(x_ref, *refs):
    n = len(refs) // 2
    x = x_ref[...]
    for w_ref, o_ref in zip(refs[:n], refs[n:]):
        o_ref[...] = _dot(x, w_ref[...])


def multi_linear(x, weights, *, tm):
    m, d = x.shape
    return pl.pallas_call(
        _multi_linear_kernel,
        grid=(m // tm,),
        in_specs=[pl.BlockSpec((tm, d), lambda i: (i, 0))]
                 + [pl.BlockSpec(w.shape, lambda i: (0, 0), pipeline_mode=pl.Buffered(1)) for w in weights],
        out_specs=[pl.BlockSpec((tm, w.shape[1]), lambda i: (i, 0)) for w in weights],
        out_shape=[jax.ShapeDtypeStruct((m, w.shape[1]), F32) for w in weights],
        compiler_params=_cparams("parallel"),
        name="multi_linear",
    )(x, *weights)


def _layer_tail_kernel(h_ref, y_ref, wo_ref, gm_ref, wu_ref, wd_ref, p_ref, gp_ref, wg_ref, wp_ref, gn_ref,
                       *refs, final):
    outs, (h1_ref, xn_ref, acc_ref) = refs[:-3], refs[-3:]
    f = pl.program_id(1)

    @pl.when(f == 0)
    def _():
        h1 = h_ref[...] + _dot(y_ref[...].astype(BF16), wo_ref[...])
        h1_ref[...] = h1
        xn_ref[...] = _rms(h1, gm_ref[...]).astype(BF16)
        acc_ref[...] = jnp.zeros_like(acc_ref)

    hid = jnp.square(jnp.maximum(_dot(xn_ref[...], wu_ref[...]), 0.0))
    acc_ref[...] += _dot(hid.astype(BF16), wd_ref[...])

    @pl.when(f == pl.num_programs(1) - 1)
    def _():
        h2 = h1_ref[...] + acc_ref[...]
        gate = _sigmoid(_dot(_rms(h2, gp_ref[...]).astype(BF16), wg_ref[...]))
        h3 = h2 + _dot(p_ref[...].astype(BF16), wp_ref[...]) * gate
        if final:
            outs[0][...] = _rms(h3, gn_ref[...])
        else:
            outs[0][...] = h3
            outs[1][...] = _rms(h3, gn_ref[...]).astype(BF16)


def layer_tail(h, y, w_out, g_mlp, w_up, w_down, p, g_ple, w_gate, w_proj, g_next, *, tm, tf, final):
    m, d = h.shape
    k = y.shape[1]
    ff = w_up.shape[1]
    pd = p.shape[1]
    row = lambda i, f: (i, 0)
    fixed = lambda i, f: (0, 0)
    once = pl.Buffered(1)
    out_specs = [pl.BlockSpec((tm, d), row)]
    out_shape = [jax.ShapeDtypeStruct((m, d), F32)]
    if not final:
        out_specs.append(pl.BlockSpec((tm, d), row))
        out_shape.append(jax.ShapeDtypeStruct((m, d), BF16))
    outs = pl.pallas_call(
        functools.partial(_layer_tail_kernel, final=final),
        grid=(m // tm, ff // tf),
        in_specs=[pl.BlockSpec((tm, d), row),
                  pl.BlockSpec((tm, k), row),
                  pl.BlockSpec((k, d), fixed, pipeline_mode=once),
                  pl.BlockSpec((1, d), fixed),
                  pl.BlockSpec((d, tf), lambda i, f: (0, f)),
                  pl.BlockSpec((tf, d), lambda i, f: (f, 0)),
                  pl.BlockSpec((tm, pd), row),
                  pl.BlockSpec((1, d), fixed),
                  pl.BlockSpec((d, d), fixed, pipeline_mode=once),
                  pl.BlockSpec((pd, d), fixed, pipeline_mode=once),
                  pl.BlockSpec((1, d), fixed)],
        out_specs=out_specs,
        out_shape=out_shape,
        scratch_shapes=[pltpu.VMEM((tm, d), F32), pltpu.VMEM((tm, d), BF16), pltpu.VMEM((tm, d), F32)],
        compiler_params=_cparams("parallel", "arbitrary"),
        name="layer_tail",
    )(h, y, w_out, g_mlp.reshape(1, d), w_up, w_down, p, g_ple.reshape(1, d), w_gate, w_proj, g_next.reshape(1, d))
    return outs[0] if final else tuple(outs)


def _ssd_head_select():
    rows = jnp.arange(LANES)[:, None]
    cols = jnp.arange(SSD_HEADS * LANES)[None, :] // LANES
    return ((rows % SSD_HEADS == cols) & (rows < 3 * SSD_HEADS)).astype(BF16)


def _ssd_chunk_kernel(z_ref, xbc_ref, dt_ref, cw_ref, cb_ref, dtb_ref, alog_ref, dskip_ref, ng_ref, sel_ref,
                      y_ref, conv_ref, state_ref, xp_ref, act_ref, acol_ref, st_ref, xbd_ref, sbd_ref):
    c = pl.program_id(1)
    t = SSD_CHUNK
    gw = SSD_GROUP_W
    hpg = SSD_HEADS_PER_GROUP
    hd = SSD_HEAD_DIM

    @pl.when(c == 0)
    def _():
        xp_ref[:, 0:SUBLANES, :] = jnp.zeros((xp_ref.shape[0], SUBLANES, LANES), F32)
        st_ref[...] = jnp.zeros_like(st_ref)
        xbd_ref[...] = jnp.zeros_like(xbd_ref)
        sbd_ref[...] = jnp.zeros_like(sbd_ref)

    for j in range(SSD_CONV_DIM // LANES):
        cols = slice(j * LANES, (j + 1) * LANES)
        xp_ref[j, SUBLANES:SUBLANES + t, :] = xbc_ref[:, cols]
        acc = xp_ref[j, SUBLANES - 3:SUBLANES - 3 + t, :] * cw_ref[0:1, cols]
        for k in range(1, CONV_W):
            acc = acc + xp_ref[j, SUBLANES - 3 + k:SUBLANES - 3 + k + t, :] * cw_ref[k:k + 1, cols]
        act_ref[:, cols] = _silu(acc + cb_ref[:, cols])
        xp_ref[j, 0:SUBLANES, :] = xbc_ref[t - SUBLANES:t, cols]
    conv_ref[0] = xbc_ref[t - SUBLANES:t, :]

    dtv = _softplus(dt_ref[...] + dtb_ref[...])
    da = dtv * (-jnp.exp(alog_ref[...]))
    row = lax.broadcasted_iota(jnp.int32, (t, t), 0)
    col = lax.broadcasted_iota(jnp.int32, (t, t), 1)
    causal = row >= col
    tril = causal.astype(BF16)
    d_hi, d_mid, d_lo = _split3(da)
    a2 = (_dot(tril, d_hi) + (_dot(tril, d_mid) + _dot(tril, d_lo))) * LOG2E
    arow_t = (a2 - jnp.log(dtv) * LOG2E).T

    lane = lax.broadcasted_iota(jnp.int32, (t, LANES), 1)
    a_hi, a_mid, a_lo = _split3(jnp.where(lane < SSD_HEADS, a2, 0.0))
    packed = (a_hi.astype(F32) + pltpu.roll(a_mid.astype(F32), SSD_HEADS, 1)
              + pltpu.roll(a_lo.astype(F32), 2 * SSD_HEADS, 1)).astype(BF16)
    acol_ref[...] = _dot(packed, sel_ref[...])

    lane_g = lax.broadcasted_iota(jnp.int32, (1, gw), 1) // hd
    for g in range(SSD_GROUPS):
        xs_g = act_ref[:, g * gw:(g + 1) * gw]
        b_g = act_ref[:, SSD_D_INNER + g * SSD_STATE:SSD_D_INNER + (g + 1) * SSD_STATE]
        c_g = act_ref[:, SSD_D_INNER + SSD_BC_DIM + g * SSD_STATE:
                      SSD_D_INNER + SSD_BC_DIM + (g + 1) * SSD_STATE]
        cb = jnp.where(causal, _dot_nt(c_g.astype(BF16), b_g.astype(BF16)), 0.0)
        b_gt = b_g.T
        st_g = st_ref[g]
        xs_b = xs_g.astype(BF16)
        for r in range(hpg):
            xbd_ref[g, r * t:(r + 1) * t, r * hd:(r + 1) * hd] = xs_b[:, r * hd:(r + 1) * hd]
        m_parts, d_parts, w_parts = [], [], []
        dec_g = jnp.zeros((1, gw), F32)
        for r in range(hpg):
            h = g * hpg + r
            acol = acol_ref[:, h * LANES:(h + 1) * LANES]
            arow = arow_t[h:h + 1, :]
            aend = acol[t - 1:t, :]
            m_parts.append((cb * jnp.exp2(jnp.minimum(acol - arow, EXP2_CAP))).astype(BF16))
            d_parts.append((c_g * jnp.exp2(acol)).astype(BF16))
            w_parts.append((b_gt * jnp.exp2(aend - arow)).astype(BF16))
            dec_g = jnp.where(lane_g == r, jnp.exp2(jnp.concatenate([aend, aend], axis=1)), dec_g)
        x_bd = xbd_ref[g]
        y_g = (_dot(jnp.concatenate(m_parts, axis=1), x_bd) + _dot(jnp.concatenate(d_parts, axis=1), sbd_ref[g])
               + xs_g * dskip_ref[:, g * gw:(g + 1) * gw])
        st_new = st_g * dec_g + _dot(jnp.concatenate(w_parts, axis=1), x_bd)
        st_ref[g] = st_new
        st_b = st_new.astype(BF16)
        for r in range(hpg):
            sbd_ref[g, r * SSD_STATE:(r + 1) * SSD_STATE, r * hd:(r + 1) * hd] = st_b[:, r * hd:(r + 1) * hd]
        yz = y_g * _silu(z_ref[:, g * gw:(g + 1) * gw])
        y_ref[:, g * gw:(g + 1) * gw] = _rms(yz, ng_ref[:, g * gw:(g + 1) * gw]).astype(y_ref.dtype)

    @pl.when(c == pl.num_programs(1) - 1)
    def _():
        for g in range(SSD_GROUPS):
            state_ref[0, g * hpg:(g + 1) * hpg] = st_ref[g].T.reshape(hpg, SSD_HEAD_DIM, SSD_STATE)


def ssd_prompt(z, xbc, dt, conv_w, conv_b, dt_bias, a_log, d_skip, norm_g, batch, seq):
    t = SSD_CHUNK
    nc = seq // t
    pad = LANES - SSD_HEADS
    rowblk = lambda b, c: (b * nc + c, 0)
    fixed = lambda b, c: (0, 0)
    y, conv_tail, state = pl.pallas_call(
        _ssd_chunk_kernel,
        grid=(batch, nc),
        in_specs=[pl.BlockSpec((t, SSD_D_INNER), rowblk),
                  pl.BlockSpec((t, SSD_CONV_DIM), rowblk),
                  pl.BlockSpec((t, LANES), rowblk),
                  pl.BlockSpec((CONV_W, SSD_CONV_DIM), fixed),
                  pl.BlockSpec((1, SSD_CONV_DIM), fixed),
                  pl.BlockSpec((1, LANES), fixed),
                  pl.BlockSpec((1, LANES), fixed),
                  pl.BlockSpec((1, SSD_D_INNER), fixed),
                  pl.BlockSpec((1, SSD_D_INNER), fixed),
                  pl.BlockSpec((LANES, SSD_HEADS * LANES), fixed)],
        out_specs=[pl.BlockSpec((t, SSD_D_INNER), rowblk),
                   pl.BlockSpec((1, SUBLANES, SSD_CONV_DIM), lambda b, c: (b, 0, 0)),
                   pl.BlockSpec((1, SSD_HEADS, SSD_HEAD_DIM, SSD_STATE), lambda b, c: (b, 0, 0, 0))],
        out_shape=[jax.ShapeDtypeStruct((batch * seq, SSD_D_INNER), BF16),
                   jax.ShapeDtypeStruct((batch, SUBLANES, SSD_CONV_DIM), F32),
                   jax.ShapeDtypeStruct((batch, SSD_HEADS, SSD_HEAD_DIM, SSD_STATE), F32)],
        scratch_shapes=[pltpu.VMEM((SSD_CONV_DIM // LANES, SUBLANES + t, LANES), F32),
                        pltpu.VMEM((t, SSD_CONV_DIM), F32),
                        pltpu.VMEM((t, SSD_HEADS * LANES), F32),
                        pltpu.VMEM((SSD_GROUPS, SSD_STATE, SSD_GROUP_W), F32),
                        pltpu.VMEM((SSD_GROUPS, SSD_HEADS_PER_GROUP * t, SSD_GROUP_W), BF16),
                        pltpu.VMEM((SSD_GROUPS, SSD_HEADS_PER_GROUP * SSD_STATE, SSD_GROUP_W), BF16)],
        compiler_params=_cparams("parallel", "arbitrary"),
        name="ssd_prompt",
    )(z, xbc, dt, conv_w, conv_b.reshape(1, -1), jnp.pad(dt_bias, (0, pad)).reshape(1, LANES),
      jnp.pad(a_log, (0, pad)).reshape(1, LANES), jnp.repeat(d_skip, SSD_HEAD_DIM).reshape(1, -1),
      norm_g.reshape(1, -1), _ssd_head_select())
    return y, conv_tail[:, SUBLANES - (CONV_W - 1):, :], state


MOBA_SPAN = 2 * MOBA_BLOCK
MOBA_HEADS_PER_STEP = 2


def _col_reduce(x, pair_op, reduce_fn):
    n = x.shape[0]
    while n > 8 * SUBLANES and n % 2 == 0:
        n //= 2
        x = pair_op(x[:n], x[n:])
    return reduce_fn(x, axis=0, keepdims=True)


def _moba_prompt_kernel(q_ref, k_ref, v_ref, o_ref, kb_ref, vt_ref, sel_ref, s_ref, p_ref):
    qi = pl.program_id(2)
    blk = MOBA_BLOCK
    seq = k_ref.shape[0]
    nb = seq // blk
    nbp = sel_ref.shape[2]
    hd = MOBA_HEAD_DIM
    heads = q_ref.shape[1] // hd

    def prepare(hh):
        cols = slice(hh * hd, (hh + 1) * hd)
        means = []
        for j in range(nb):
            kj = k_ref[j * blk:(j + 1) * blk, cols]
            means.append(jnp.sum(kj, axis=0, keepdims=True) * (1.0 / blk))
            kb_ref[hh, j * blk:(j + 1) * blk, :] = kj.astype(BF16)
        for j in range(seq // LANES):
            vt_ref[hh, :, j * LANES:(j + 1) * LANES] = v_ref[j * LANES:(j + 1) * LANES, cols].T.astype(BF16)
        if nbp > nb:
            means.append(jnp.zeros((nbp - nb, hd), F32))
        kmean = jnp.concatenate(means, axis=0)
        sub = lax.broadcasted_iota(jnp.int32, (nbp, blk), 0)
        for i in range(nb):
            gm = jnp.where(sub < i, _dot3_nt(kmean, q_ref[i * blk:(i + 1) * blk, cols]), -jnp.inf)
            rank = jnp.zeros((nbp, blk), F32)
            for m in range(i):
                gm_m = gm[m:m + 1, :]
                beats = (gm_m > gm) | ((gm_m == gm) & (sub > m))
                rank = rank + jnp.where(beats, 1.0, 0.0)
            chosen = ((sub < i) & (rank < MOBA_TOPK)) | (sub == i)
            sel_ref[hh, i] = jnp.where(chosen, 1.0, 0.0)

    @pl.when(qi == 0)
    def _():
        for hh in range(heads):
            prepare(hh)

    key_minus_query = (lax.broadcasted_iota(jnp.int32, (blk, blk), 0)
                       - lax.broadcasted_iota(jnp.int32, (blk, blk), 1))

    span = min(MOBA_SPAN, seq)
    per_span = span // blk

    def attend_head(hh, n_keys):
        cols = slice(hh * hd, (hh + 1) * hd)
        q = q_ref[pl.ds(pl.multiple_of(qi * blk, blk), blk), cols]
        qb = (q * (hd ** -0.5)).astype(BF16)
        sel = sel_ref[hh, qi]
        m = None
        for j in range(n_keys // blk):
            rows = slice(j * blk, (j + 1) * blk)
            bias = jnp.where(sel[j:j + 1, :] > 0.0, 0.0, NEG_BIG)
            s = jnp.where(key_minus_query <= (qi - j) * blk, _dot_nt(kb_ref[hh, rows, :], qb), NEG_BIG) + bias
            s_ref[hh, rows, :] = s
            bm = _col_reduce(s, jnp.maximum, jnp.max)
            m = bm if m is None else jnp.maximum(m, bm)
        l = jnp.zeros((1, blk), F32)
        for j in range(n_keys // blk):
            rows = slice(j * blk, (j + 1) * blk)
            p = jnp.exp(s_ref[hh, rows, :] - m)
            l = l + _col_reduce(p, jnp.add, jnp.sum)
            p_ref[hh, rows, :] = p.astype(BF16)
        acc = _dot(vt_ref[hh, :, 0:n_keys], p_ref[hh, 0:n_keys, :])
        o_ref[:, cols] = (acc / l).T.astype(o_ref.dtype)

    def attend(n_keys):
        for hh in range(heads):
            attend_head(hh, n_keys)

    for sp in range(seq // span):
        in_span = (qi >= sp * per_span) & (qi < (sp + 1) * per_span)
        pl.when(in_span)(functools.partial(attend, (sp + 1) * span))


def moba_prompt(q, k, v, batch, seq):
    nb = seq // MOBA_BLOCK
    hd = MOBA_HEAD_DIM
    hps = MOBA_HEADS_PER_STEP
    wide = lambda b, h, i: (b, h)
    return pl.pallas_call(
        _moba_prompt_kernel,
        grid=(batch, MOBA_HEADS // hps, nb),
        in_specs=[pl.BlockSpec((seq, hps * hd), wide),
                  pl.BlockSpec((seq, hps * hd), wide),
                  pl.BlockSpec((seq, hps * hd), wide)],
        out_specs=pl.BlockSpec((MOBA_BLOCK, hps * hd), lambda b, h, i: (b * nb + i, h)),
        out_shape=jax.ShapeDtypeStruct((batch * seq, D_MODEL), BF16),
        scratch_shapes=[pltpu.VMEM((hps, seq, hd), BF16),
                        pltpu.VMEM((hps, hd, seq), BF16),
                        pltpu.VMEM((hps, nb, -(-nb // SUBLANES) * SUBLANES, MOBA_BLOCK), F32),
                        pltpu.VMEM((hps, seq, MOBA_BLOCK), F32),
                        pltpu.VMEM((hps, seq, MOBA_BLOCK), BF16)],
        compiler_params=_cparams("parallel", "parallel", "arbitrary"),
        name="moba_prompt",
    )(q, k, v)


def _lru_gates(xc, wa, ba, wx, bx, lam):
    xb = xc.astype(BF16)
    r = _sigmoid(_dot(xb, wa) + ba)
    i = _sigmoid(_dot(xb, wx) + bx)
    log_a = (-LRU_C) * r * _softplus(-lam)
    a = jnp.exp(log_a)
    u = jnp.sqrt(-jnp.tanh(log_a) * (a * a + 1.0)) * (i * xc)
    return a, u


def _lru_chunk_kernel(gate_ref, x_ref, cw_ref, cb_ref, wa_ref, ba_ref, wx_ref, bx_ref, lam_ref,
                      y_ref, conv_ref, hlast_ref, xp_ref, a_ref, u_ref, h_ref):
    c = pl.program_id(1)
    t = LRU_CHUNK
    w = LRU_BLOCK_W

    @pl.when(c == 0)
    def _():
        xp_ref[:, 0:SUBLANES, :] = jnp.zeros((LRU_BLOCKS, SUBLANES, w), F32)
        h_ref[...] = jnp.zeros_like(h_ref)

    for n in range(LRU_BLOCKS):
        cols = slice(n * w, (n + 1) * w)
        xp_ref[n, SUBLANES:SUBLANES + t, :] = x_ref[:, cols]
        acc = xp_ref[n, SUBLANES - 3:SUBLANES - 3 + t, :] * cw_ref[0:1, cols]
        for k in range(1, CONV_W):
            acc = acc + xp_ref[n, SUBLANES - 3 + k:SUBLANES - 3 + k + t, :] * cw_ref[k:k + 1, cols]
        xc = acc + cb_ref[:, cols]
        a, u = _lru_gates(xc, wa_ref[n], ba_ref[:, cols], wx_ref[n], bx_ref[:, cols], lam_ref[:, cols])
        a_ref[:, cols] = a
        u_ref[:, cols] = u
        xp_ref[n, 0:SUBLANES, :] = x_ref[t - SUBLANES:t, cols]
    conv_ref[0] = x_ref[t - SUBLANES:t, :]

    srow = lax.broadcasted_iota(jnp.int32, (SUBLANES, LRU_WIDTH), 0)

    def tile(i, h):
        rows = pl.ds(pl.multiple_of(i * SUBLANES, SUBLANES), SUBLANES)
        a = a_ref[rows, :]
        u = u_ref[rows, :]
        for d in (1, 2, 4):
            keep = srow >= d
            a_sh = jnp.where(keep, pltpu.roll(a, d, 0), 1.0)
            u_sh = jnp.where(keep, pltpu.roll(u, d, 0), 0.0)
            u = a * u_sh + u
            a = a * a_sh
        hs = a * h + u
        u_ref[rows, :] = hs
        return jnp.broadcast_to(hs[SUBLANES - 1:SUBLANES, :], (SUBLANES, LRU_WIDTH))

    h = lax.fori_loop(0, t // SUBLANES, tile, h_ref[...])
    h_ref[...] = h
    hlast_ref[0] = h
    y_ref[...] = (_gelu_tanh(gate_ref[...]) * u_ref[...]).astype(y_ref.dtype)


def lru_prompt(gate_br, x_br, conv_w, conv_b, w_a, b_a, w_x, b_x, lam, batch, seq):
    t = LRU_CHUNK
    nc = seq // t
    wd = LRU_WIDTH
    rowblk = lambda b, c: (b * nc + c, 0)
    fixed = lambda b, c: (0, 0)
    fixed3 = lambda b, c: (0, 0, 0)
    perb = lambda b, c: (b, 0, 0)
    y, conv_tail, h_last = pl.pallas_call(
        _lru_chunk_kernel,
        grid=(batch, nc),
        in_specs=[pl.BlockSpec((t, wd), rowblk),
                  pl.BlockSpec((t, wd), rowblk),
                  pl.BlockSpec((CONV_W, wd), fixed),
                  pl.BlockSpec((1, wd), fixed),
                  pl.BlockSpec((LRU_BLOCKS, LRU_BLOCK_W, LRU_BLOCK_W), fixed3),
                  pl.BlockSpec((1, wd), fixed),
                  pl.BlockSpec((LRU_BLOCKS, LRU_BLOCK_W, LRU_BLOCK_W), fixed3),
                  pl.BlockSpec((1, wd), fixed),
                  pl.BlockSpec((1, wd), fixed)],
        out_specs=[pl.BlockSpec((t, wd), rowblk),
                   pl.BlockSpec((1, SUBLANES, wd), perb),
                   pl.BlockSpec((1, SUBLANES, wd), perb)],
        out_shape=[jax.ShapeDtypeStruct((batch * seq, wd), BF16),
                   jax.ShapeDtypeStruct((batch, SUBLANES, wd), F32),
                   jax.ShapeDtypeStruct((batch, SUBLANES, wd), F32)],
        scratch_shapes=[pltpu.VMEM((LRU_BLOCKS, SUBLANES + t, LRU_BLOCK_W), F32),
                        pltpu.VMEM((t, wd), F32),
                        pltpu.VMEM((t, wd), F32),
                        pltpu.VMEM((SUBLANES, wd), F32)],
        compiler_params=_cparams("parallel", "arbitrary"),
        name="lru_prompt",
    )(gate_br, x_br, conv_w, conv_b.reshape(1, wd), w_a.astype(BF16), b_a.reshape(1, wd),
      w_x.astype(BF16), b_x.reshape(1, wd), lam.reshape(1, wd))
    return y, conv_tail[:, SUBLANES - (CONV_W - 1):, :], h_last[:, 0, :]


def _ssd_step_kernel(z_ref, xbc_ref, dt_ref, cs_ref, st_ref, cw_ref, cb_ref, dtb_ref, alog_ref, dskip_ref, ng_ref,
                     y_ref, cso_ref, sto_ref):
    gw = SSD_GROUP_W
    hpg = SSD_HEADS_PER_GROUP
    x = xbc_ref[0]
    buf = cs_ref[0]
    conv = x * cw_ref[CONV_W - 1:CONV_W, :]
    for k in range(CONV_W - 1):
        conv = conv + buf[k:k + 1, :] * cw_ref[k:k + 1, :]
    act = _silu(conv + cb_ref[...])
    cso_ref[0, 0:CONV_W - 2, :] = buf[1:CONV_W - 1, :]
    cso_ref[0, CONV_W - 2:CONV_W - 1, :] = x

    dtv = _softplus(dt_ref[0] + dtb_ref[...])
    dec = jnp.exp(dtv * (-jnp.exp(alog_ref[...])))
    lane_g = lax.broadcasted_iota(jnp.int32, (1, gw), 1) // SSD_HEAD_DIM
    first_row = lax.broadcasted_iota(jnp.int32, (SUBLANES, gw), 0) == 0
    ys = []
    for g in range(SSD_GROUPS):
        xs_g = act[:, g * gw:(g + 1) * gw]
        b_g = act[:, SSD_D_INNER + g * SSD_STATE:SSD_D_INNER + (g + 1) * SSD_STATE]
        c_g = act[:, SSD_D_INNER + SSD_BC_DIM + g * SSD_STATE:SSD_D_INNER + SSD_BC_DIM + (g + 1) * SSD_STATE]
        dt_e = jnp.zeros((1, gw), F32)
        dec_rows = []
        for r in range(hpg):
            h = g * hpg + r
            dt_e = jnp.where(lane_g == r, dtv[:, h:h + 1], dt_e)
            dec_rows.append(jnp.broadcast_to(dec[:, h:h + 1], (SSD_HEAD_DIM, SSD_STATE)))
        xdt = xs_g * dt_e
        x8 = jnp.where(first_row, jnp.broadcast_to(xdt, (SUBLANES, gw)), 0.0).astype(BF16)
        b8 = jnp.broadcast_to(b_g, (SUBLANES, SSD_STATE)).astype(BF16)
        outer = lax.dot_general(x8, b8, (((0,), (0,)), ((), ())), preferred_element_type=F32)
        old = st_ref[0, g * hpg:(g + 1) * hpg].reshape(gw, SSD_STATE)
        new = old * jnp.concatenate(dec_rows, axis=0) + outer
        sto_ref[0, g * hpg:(g + 1) * hpg] = new.reshape(hpg, SSD_HEAD_DIM, SSD_STATE)
        c8 = jnp.broadcast_to(c_g, (SUBLANES, SSD_STATE)).astype(BF16)
        y_g = _dot_nt(c8, new.astype(BF16))[0:1, :] + xs_g * dskip_ref[:, g * gw:(g + 1) * gw]
        yz = y_g * _silu(z_ref[0][:, g * gw:(g + 1) * gw])
        ys.append(_rms(yz, ng_ref[:, g * gw:(g + 1) * gw]))
    y_ref[0] = jnp.concatenate(ys, axis=1)


def ssd_step(z, xbc, dt, conv_state, ssm_state, conv_w, conv_b, dt_bias, a_log, d_skip, norm_g):
    nb = z.shape[0]
    pad = LANES - SSD_HEADS
    per3 = lambda b: (b, 0, 0)
    fixed = lambda b: (0, 0)
    y, conv_new, state_new = pl.pallas_call(
        _ssd_step_kernel,
        grid=(nb,),
        in_specs=[pl.BlockSpec((1, 1, SSD_D_INNER), per3),
                  pl.BlockSpec((1, 1, SSD_CONV_DIM), per3),
                  pl.BlockSpec((1, 1, LANES), per3),
                  pl.BlockSpec((1, CONV_W - 1, SSD_CONV_DIM), per3),
                  pl.BlockSpec((1, SSD_HEADS, SSD_HEAD_DIM, SSD_STATE), lambda b: (b, 0, 0, 0)),
                  pl.BlockSpec((CONV_W, SSD_CONV_DIM), fixed),
                  pl.BlockSpec((1, SSD_CONV_DIM), fixed),
                  pl.BlockSpec((1, LANES), fixed),
                  pl.BlockSpec((1, LANES), fixed),
                  pl.BlockSpec((1, SSD_D_INNER), fixed),
                  pl.BlockSpec((1, SSD_D_INNER), fixed)],
        out_specs=[pl.BlockSpec((1, 1, SSD_D_INNER), per3),
                   pl.BlockSpec((1, CONV_W - 1, SSD_CONV_DIM), per3),
                   pl.BlockSpec((1, SSD_HEADS, SSD_HEAD_DIM, SSD_STATE), lambda b: (b, 0, 0, 0))],
        out_shape=[jax.ShapeDtypeStruct((nb, 1, SSD_D_INNER), F32),
                   jax.ShapeDtypeStruct((nb, CONV_W - 1, SSD_CONV_DIM), F32),
                   jax.ShapeDtypeStruct((nb, SSD_HEADS, SSD_HEAD_DIM, SSD_STATE), F32)],
        compiler_params=_cparams("parallel"),
        name="ssd_step",
    )(z.reshape(nb, 1, -1), xbc.reshape(nb, 1, -1), dt.reshape(nb, 1, -1), conv_state, ssm_state,
      conv_w, conv_b.reshape(1, -1), jnp.pad(dt_bias, (0, pad)).reshape(1, LANES),
      jnp.pad(a_log, (0, pad)).reshape(1, LANES), jnp.repeat(d_skip, SSD_HEAD_DIM).reshape(1, -1),
      norm_g.reshape(1, -1))
    return y.reshape(nb, SSD_D_INNER), conv_new, state_new


def _lru_step_kernel(gate_ref, x_ref, cs_ref, h0_ref, cw_ref, cb_ref, wa_ref, ba_ref, wx_ref, bx_ref, lam_ref,
                     y_ref, cso_ref, ho_ref):
    w = LRU_BLOCK_W
    x = x_ref[...]
    xc = x * cw_ref[CONV_W - 1:CONV_W, :] + cb_ref[...]
    for k in range(CONV_W - 1):
        xc = xc + cs_ref[k] * cw_ref[k:k + 1, :]
    for k in range(CONV_W - 2):
        cso_ref[k] = cs_ref[k + 1]
    cso_ref[CONV_W - 2] = x
    for n in range(LRU_BLOCKS):
        cols = slice(n * w, (n + 1) * w)
        a, u = _lru_gates(xc[:, cols], wa_ref[n], ba_ref[:, cols], wx_ref[n], bx_ref[:, cols], lam_ref[:, cols])
        h = a * h0_ref[:, cols] + u
        ho_ref[:, cols] = h
        y_ref[:, cols] = _gelu_tanh(gate_ref[:, cols]) * h


def lru_step(gate_br, x_br, conv_state, h0, conv_w, conv_b, w_a, b_a, w_x, b_x, lam):
    nb, wd = x_br.shape
    return pl.pallas_call(
        _lru_step_kernel,
        out_shape=[jax.ShapeDtypeStruct((nb, wd), F32),
                   jax.ShapeDtypeStruct((CONV_W - 1, nb, wd), F32),
                   jax.ShapeDtypeStruct((nb, wd), F32)],
        compiler_params=pltpu.CompilerParams(vmem_limit_bytes=VMEM_LIMIT),
        name="lru_step",
    )(gate_br, x_br, conv_state, h0, conv_w, conv_b.reshape(1, wd), w_a.astype(BF16), b_a.reshape(1, wd),
      w_x.astype(BF16), b_x.reshape(1, wd), lam.reshape(1, wd))


MOBA_SUM_PAGES = 16


def _moba_block_sums_kernel(tbl_ref, *refs):
    del tbl_ref
    pages, o_ref = refs[:-1], refs[-1]
    for k in range(len(pages) // PAGES_PER_BLOCK):
        s = jnp.sum(pages[PAGES_PER_BLOCK * k][...], axis=0)
        for j in range(1, PAGES_PER_BLOCK):
            s = s + jnp.sum(pages[PAGES_PER_BLOCK * k + j][...], axis=0)
        o_ref[0, k] = s


def moba_block_sums(pool, layer, table):
    nb, n_pages = table.shape
    steps = n_pages // MOBA_SUM_PAGES
    blocks_per_step = MOBA_SUM_PAGES // PAGES_PER_BLOCK
    page_shape = pool.shape[2:]

    def page_spec(k):
        return pl.BlockSpec((None, None) + page_shape,
                            lambda b, c, tbl: (layer, tbl[b * n_pages + c * MOBA_SUM_PAGES + k], 0, 0, 0))

    return pl.pallas_call(
        _moba_block_sums_kernel,
        grid_spec=pltpu.PrefetchScalarGridSpec(
            num_scalar_prefetch=1,
            grid=(nb, steps),
            in_specs=[page_spec(k) for k in range(MOBA_SUM_PAGES)],
            out_specs=pl.BlockSpec((1, blocks_per_step) + page_shape[1:], lambda b, c, tbl: (b, c, 0, 0))),
        out_shape=jax.ShapeDtypeStruct((nb, n_pages // PAGES_PER_BLOCK) + page_shape[1:], F32),
        compiler_params=_cparams("parallel", "arbitrary"),
        name="moba_block_sums",
    )(table.reshape(-1), *([pool] * MOBA_SUM_PAGES))


def _moba_gate_kernel(q_ref, bs_ref, sel_ref):
    n_blocks = bs_ref.shape[1]
    q = q_ref[0]
    gates = [jnp.sum((bs_ref[0, n] * (1.0 / MOBA_BLOCK)) * q, axis=-1, keepdims=True) for n in range(n_blocks)]
    lane = lax.broadcasted_iota(jnp.int32, (MOBA_HEADS, LANES), 1)
    out = jnp.zeros((MOBA_HEADS, LANES), F32)
    for k in range(MOBA_TOPK):
        best = functools.reduce(jnp.maximum, gates)
        idx = jnp.full((MOBA_HEADS, 1), float(n_blocks), F32)
        for n in reversed(range(n_blocks)):
            idx = jnp.where(gates[n] == best, float(n), idx)
        out = jnp.where(lane == k, idx, out)
        gates = [jnp.where(idx == float(n), -jnp.inf, gates[n]) for n in range(n_blocks)]
    sel_ref[0] = out.astype(jnp.int32)


def moba_gate(q, block_sums):
    nb = q.shape[0]
    n_blocks = block_sums.shape[1]
    return pl.pallas_call(
        _moba_gate_kernel,
        grid=(nb,),
        in_specs=[pl.BlockSpec((1, MOBA_HEADS, MOBA_HEAD_DIM), lambda b: (b, 0, 0)),
                  pl.BlockSpec((1, n_blocks, MOBA_HEADS, MOBA_HEAD_DIM), lambda b: (b, 0, 0, 0))],
        out_specs=pl.BlockSpec((1, MOBA_HEADS, LANES), lambda b: (b, 0, 0)),
        out_shape=jax.ShapeDtypeStruct((nb, MOBA_HEADS, LANES), jnp.int32),
        compiler_params=_cparams("parallel"),
        name="moba_gate",
    )(q.reshape(nb, MOBA_HEADS, MOBA_HEAD_DIM), block_sums)


def _moba_decode_kernel(tbl_ref, sel_ref, q_ref, kn_ref, vn_ref, kpool_ref, vpool_ref, o_ref, kbuf, vbuf, sem,
                        *, layer, n_table_pages):
    b, h = pl.program_id(0), pl.program_id(1)
    nh = pl.num_programs(1)
    step = b * nh + h
    slot = lax.rem(step, 2)
    n_sel = MOBA_TOPK * PAGES_PER_BLOCK

    def page_copies(bb, hh, sl):
        out = []
        for k in range(MOBA_TOPK):
            blk = sel_ref[(bb * nh + hh) * MOBA_TOPK + k]
            for s in range(PAGES_PER_BLOCK):
                page = tbl_ref[bb * n_table_pages + blk * PAGES_PER_BLOCK + s]
                i = k * PAGES_PER_BLOCK + s
                out.append(pltpu.make_async_copy(kpool_ref.at[layer, page, :, hh, :], kbuf.at[sl, i], sem.at[sl, 0, i]))
                out.append(pltpu.make_async_copy(vpool_ref.at[layer, page, :, hh, :], vbuf.at[sl, i], sem.at[sl, 1, i]))
        return out

    def start_all(copies):
        for i, c in enumerate(copies):
            c.start(priority=i % 2)

    @pl.when(step == 0)
    def _():
        start_all(page_copies(b, h, 0))

    @pl.when(step + 1 < pl.num_programs(0) * nh)
    def _():
        wrap = h + 1 == nh
        start_all(page_copies(jnp.where(wrap, b + 1, b), jnp.where(wrap, 0, h + 1), 1 - slot))

    for c in page_copies(b, h, slot):
        c.wait()

    scale = MOBA_HEAD_DIM ** -0.5
    q = q_ref[0]
    q8 = jnp.broadcast_to(q, (SUBLANES, MOBA_HEAD_DIM)).astype(BF16)
    k_cat = kbuf[slot].reshape(n_sel * PAGE_SIZE, MOBA_HEAD_DIM).astype(BF16)
    v_cat = vbuf[slot].reshape(n_sel * PAGE_SIZE, MOBA_HEAD_DIM).astype(BF16)
    s = _dot_nt(q8, k_cat) * scale
    s_new = jnp.sum(q * kn_ref[0], axis=1, keepdims=True) * scale
    m = jnp.maximum(jnp.max(s, axis=1, keepdims=True), s_new)
    p = jnp.exp(s - m)
    p_new = jnp.exp(s_new - m)
    denom = jnp.sum(p, axis=1, keepdims=True) + p_new
    o = (_dot(p.astype(BF16), v_cat) + p_new * vn_ref[0]) / denom
    o_ref[0] = o[0:1, :]


def moba_decode(q, k_new, v_new, k_pool, v_pool, layer, table, sel):
    nb = q.shape[0]
    n_pages = table.shape[1]
    hd = MOBA_HEAD_DIM
    n_sel = MOBA_TOPK * PAGES_PER_BLOCK
    tok = pl.BlockSpec((1, 1, hd), lambda b, h, tbl, sl: (b, 0, h))
    hbm = pl.BlockSpec(memory_space=pl.ANY)
    out = pl.pallas_call(
        functools.partial(_moba_decode_kernel, layer=layer, n_table_pages=n_pages),
        grid_spec=pltpu.PrefetchScalarGridSpec(
            num_scalar_prefetch=2,
            grid=(nb, MOBA_HEADS),
            in_specs=[tok, tok, tok, hbm, hbm],
            out_specs=tok,
            scratch_shapes=[pltpu.VMEM((2, n_sel, PAGE_SIZE, hd), F32),
                            pltpu.VMEM((2, n_sel, PAGE_SIZE, hd), F32),
                            pltpu.SemaphoreType.DMA((2, 2, n_sel))]),
        out_shape=jax.ShapeDtypeStruct((nb, 1, D_MODEL), F32),
        compiler_params=_cparams("arbitrary", "arbitrary"),
        name="moba_decode",
    )(table.reshape(-1), sel[:, :, :MOBA_TOPK].reshape(-1),
      q.reshape(nb, 1, D_MODEL), k_new.reshape(nb, 1, D_MODEL), v_new.reshape(nb, 1, D_MODEL), k_pool, v_pool)
    return out.reshape(nb, D_MODEL)


def _prep_weights(W):
    bf = lambda a: a.astype(BF16)
    w_in, w_qkv, w_lru = lax.optimization_barrier((bf(W["ssd_w_in"]), bf(W["moba_w_qkv"]), bf(W["lru_w_in"])))
    zx = SSD_D_INNER + SSD_CONV_DIM
    return dict(
        ssd_w_z=w_in[:, :, :SSD_D_INNER],
        ssd_w_xbc=w_in[:, :, SSD_D_INNER:zx],
        ssd_w_dt=jnp.pad(w_in[:, :, zx:], ((0, 0), (0, 0), (0, LANES - SSD_HEADS))),
        ssd_w_out=bf(W["ssd_w_out"]),
        moba_w_q=w_qkv[:, :, :D_MODEL],
        moba_w_k=w_qkv[:, :, D_MODEL:2 * D_MODEL],
        moba_w_v=w_qkv[:, :, 2 * D_MODEL:],
        moba_w_o=bf(W["moba_w_o"]),
        lru_w_gate=w_lru[:, :, :LRU_WIDTH],
        lru_w_x=w_lru[:, :, LRU_WIDTH:],
        lru_w_out=bf(W["lru_w_out"]),
        mlp_w_up=bf(W["mlp_w_up"]), mlp_w_down=bf(W["mlp_w_down"]),
        ple_w_proj=bf(W["ple_w_proj"]), ple_w_gate=bf(W["ple_w_gate"]),
    )


def _trunk(h, p, W, Wb, tiles, ssd_fn, moba_fn, lru_fn):
    tm, tf = tiles
    ssd_out, moba_out, lru_out = [], [], []
    xn = rmsnorm_bf16(h, W["norm_mix"][0], tm=tm)
    for i in range(DEPTH):
        j, kind = i // N_MIXERS, i % N_MIXERS
        proj = lambda *names: multi_linear(xn, [Wb[n][j] for n in names], tm=tm)
        if kind == 0:
            y, conv, state = ssd_fn(j, *proj("ssd_w_z", "ssd_w_xbc", "ssd_w_dt"))
            ssd_out.append((conv, state))
            w_out = Wb["ssd_w_out"][j]
        elif kind == 1:
            q, k, v = proj("moba_w_q", "moba_w_k", "moba_w_v")
            moba_out.append((k, v))
            y, w_out = moba_fn(j, q, k, v), Wb["moba_w_o"][j]
        else:
            y, conv, state = lru_fn(j, *proj("lru_w_gate", "lru_w_x"))
            lru_out.append((conv, state))
            w_out = Wb["lru_w_out"][j]
        final = i == DEPTH - 1
        g_next = W["norm_final"] if final else W["norm_mix"][i + 1]
        out = layer_tail(h, y, w_out, W["norm_mlp"][i], Wb["mlp_w_up"][i], Wb["mlp_w_down"][i], p[i],
                         W["norm_ple"][i], Wb["ple_w_gate"][i], Wb["ple_w_proj"][i], g_next,
                         tm=tm, tf=tf, final=final)
        h, xn = (out, None) if final else out
    stack = lambda pairs, n: jnp.stack([pr[n] for pr in pairs])
    return (h, stack(ssd_out, 0), stack(ssd_out, 1), stack(moba_out, 0), stack(moba_out, 1),
            stack(lru_out, 0), stack(lru_out, 1))


PROMPT_TILES = (512, 1024)
SAMPLE_TILES = (32, 1024)


def kernel(x_prompt, x_sample, state_ssd_conv, state_ssd, cache_k, cache_v, page_table, state_lru_conv,
           state_lru, p_prompt, p_sample, norm_mix, norm_mlp, norm_ple, norm_final, ssd_w_in, ssd_conv_w,
           ssd_conv_b, ssd_dt_bias, ssd_a_log, ssd_d, ssd_norm, ssd_w_out, moba_w_qkv, moba_w_o, lru_w_in,
           lru_conv_w, lru_conv_b, lru_w_a, lru_b_a, lru_w_x, lru_b_x, lru_lambda, lru_w_out, mlp_w_up,
           mlp_w_down, ple_w_proj, ple_w_gate):
    W = dict(norm_mix=norm_mix, norm_mlp=norm_mlp, norm_ple=norm_ple, norm_final=norm_final,
             ssd_w_in=ssd_w_in, ssd_w_out=ssd_w_out, moba_w_qkv=moba_w_qkv, moba_w_o=moba_w_o,
             lru_w_in=lru_w_in, lru_w_out=lru_w_out, mlp_w_up=mlp_w_up, mlp_w_down=mlp_w_down,
             ple_w_proj=ple_w_proj, ple_w_gate=ple_w_gate)
    Wb = _prep_weights(W)
    batch, seq, d = x_prompt.shape
    dec_batch, dec_seq, _ = x_sample.shape
    assert dec_seq == 1 and seq % MOBA_BLOCK == 0 and seq % LRU_CHUNK == 0

    ssd_args = lambda j: (ssd_conv_w[j], ssd_conv_b[j], ssd_dt_bias[j], ssd_a_log[j], ssd_d[j], ssd_norm[j])
    lru_args = lambda j: (lru_conv_w[j], lru_conv_b[j], lru_w_a[j], lru_b_a[j].reshape(-1), lru_w_x[j],
                          lru_b_x[j].reshape(-1), lru_lambda[j])

    out_p = _trunk(
        x_prompt.reshape(batch * seq, d), p_prompt.reshape(DEPTH, batch * seq, PLE_DIM), W, Wb, PROMPT_TILES,
        lambda j, z, xbc, dt: ssd_prompt(z, xbc, dt, *ssd_args(j), batch, seq),
        lambda j, q, k, v: moba_prompt(q, k, v, batch, seq),
        lambda j, gate, x: lru_prompt(gate, x, *lru_args(j), batch, seq))

    def moba_sample(j, q, k, v):
        sel = moba_gate(q, moba_block_sums(cache_k, j, page_table))
        return moba_decode(q, k, v, cache_k, cache_v, j, page_table, sel)

    def lru_sample(j, gate, x):
        y, conv, h = lru_step(gate, x, jnp.swapaxes(state_lru_conv[j], 0, 1), state_lru[j], *lru_args(j))
        return y, jnp.swapaxes(conv, 0, 1), h

    out_s = _trunk(
        x_sample.reshape(dec_batch, d), p_sample.reshape(DEPTH, dec_batch, PLE_DIM), W, Wb, SAMPLE_TILES,
        lambda j, z, xbc, dt: ssd_step(z, xbc, dt, state_ssd_conv[j], state_ssd[j], *ssd_args(j)),
        moba_sample, lru_sample)

    def shaped(out, b, s):
        y, ssd_conv, ssd_state, k, v, lru_conv, lru_state = out
        kv_shape = (-1, b, s, MOBA_HEADS, MOBA_HEAD_DIM)
        return (y.reshape(b, s, d), ssd_conv, ssd_state, k.reshape(kv_shape), v.reshape(kv_shape),
                lru_conv, lru_state)

    yp, *rest_p = shaped(out_p, batch, seq)
    ys, *rest_s = shaped(out_s, dec_batch, dec_seq)
    return (yp, ys, *rest_p, *rest_s)
```

```python
import functools
import math

import jax
import jax.numpy as jnp
from jax import lax
from jax.experimental import pallas as pl
from jax.experimental.pallas import tpu as pltpu

D_MODEL = 1024
DEPTH = 4
N_MIXERS = 3
PLE_DIM = 256
D_FF = 4 * D_MODEL
NORM_EPS = 1e-6
CONV_W = 4
PAGE_SIZE = 128

SSD_D_INNER = 2 * D_MODEL
SSD_HEAD_DIM = 64
SSD_HEADS = SSD_D_INNER // SSD_HEAD_DIM
SSD_GROUPS = 8
SSD_HEADS_PER_GROUP = SSD_HEADS // SSD_GROUPS
SSD_STATE = 128
SSD_BC_DIM = SSD_GROUPS * SSD_STATE
SSD_CONV_DIM = SSD_D_INNER + 2 * SSD_BC_DIM
SSD_GROUP_W = SSD_D_INNER // SSD_GROUPS
SSD_CHUNK = 128

MOBA_HEADS = 8
MOBA_HEAD_DIM = D_MODEL // MOBA_HEADS
MOBA_BLOCK = 256
MOBA_TOPK = 3
PAGES_PER_BLOCK = MOBA_BLOCK // PAGE_SIZE

LRU_WIDTH = D_MODEL
LRU_BLOCKS = 8
LRU_BLOCK_W = LRU_WIDTH // LRU_BLOCKS
LRU_C = 8.0
LRU_CHUNK = 256

LANES = 128
SUBLANES = 8
VMEM_LIMIT = 56 * 1024 * 1024
NEG_BIG = -1e30
LOG2E = 1.0 / math.log(2.0)
EXP2_CAP = 126.0

F32 = jnp.float32
BF16 = jnp.bfloat16


def _cparams(*sem):
    return pltpu.CompilerParams(dimension_semantics=sem, vmem_limit_bytes=VMEM_LIMIT)


def _rms(x, g):
    var = jnp.mean(x * x, axis=-1, keepdims=True)
    return (x * lax.rsqrt(var + NORM_EPS)) * g


def _sigmoid(x):
    return 0.5 + 0.5 * jnp.tanh(0.5 * x)


def _silu(x):
    hx = 0.5 * x
    return hx + hx * jnp.tanh(hx)


def _softplus(x):
    return jnp.maximum(x, 0.0) + jnp.log1p(jnp.exp(-jnp.abs(x)))


def _gelu_tanh(x):
    return 0.5 * x * (1.0 + jnp.tanh(math.sqrt(2.0 / math.pi) * (x + 0.044715 * (x * x * x))))


def _split3(x):
    hi = x.astype(BF16)
    r1 = x - hi.astype(F32)
    mid = r1.astype(BF16)
    lo = (r1 - mid.astype(F32)).astype(BF16)
    return hi, mid, lo


def _dot(a, b):
    return jnp.dot(a, b, preferred_element_type=F32)


def _dot_nt(a, b):
    return lax.dot_general(a, b, (((1,), (1,)), ((), ())), preferred_element_type=F32)


def _dot3_nt(a, b):
    ah, am, _ = _split3(a)
    bh, bm, _ = _split3(b)
    return _dot_nt(ah, bh) + (_dot_nt(ah, bm) + _dot_nt(am, bh))


def _rmsnorm_kernel(x_ref, g_ref, o_ref):
    o_ref[...] = _rms(x_ref[...], g_ref[...]).astype(o_ref.dtype)


def rmsnorm_bf16(x, g, *, tm):
    m, d = x.shape
    return pl.pallas_call(
        _rmsnorm_kernel,
        grid=(m // tm,),
        in_specs=[pl.BlockSpec((tm, d), lambda i: (i, 0)), pl.BlockSpec((1, d), lambda i: (0, 0))],
        out_specs=pl.BlockSpec((tm, d), lambda i: (i, 0)),
        out_shape=jax.ShapeDtypeStruct((m, d), BF16),
        compiler_params=_cparams("parallel"),
        name="rmsnorm",
    )(x, g.reshape(1, d))


def _multi_linear_kernel(x_ref, *refs):
    n = len(refs) // 2
    x = x_ref[...]
    for w_ref, o_ref in zip(refs[:n], refs[n:]):
        o_ref[...] = _dot(x, w_ref[...])


def multi_linear(x, weights, layer, *, tm):
    m, d = x.shape
    return pl.pallas_call(
        _multi_linear_kernel,
        grid=(m // tm,),
        in_specs=[pl.BlockSpec((tm, d), lambda i: (i, 0))]
                 + [pl.BlockSpec((None,) + w.shape[1:], lambda i: (layer, 0, 0), pipeline_mode=pl.Buffered(1))
                    for w in weights],
        out_specs=[pl.BlockSpec((tm, w.shape[2]), lambda i: (i, 0)) for w in weights],
        out_shape=[jax.ShapeDtypeStruct((m, w.shape[2]), F32) for w in weights],
        compiler_params=_cparams("parallel"),
        name="multi_linear",
    )(x, *weights)


def _layer_tail_kernel(h_ref, y_ref, wo_ref, gm_ref, wu_ref, wd_ref, p_ref, gp_ref, wg_ref, wp_ref, gn_ref,
                       *refs, final):
    outs, (h1_ref, xn_ref, acc_ref) = refs[:-3], refs[-3:]
    f = pl.program_id(1)

    @pl.when(f == 0)
    def _():
        h1 = h_ref[...] + _dot(y_ref[...].astype(BF16), wo_ref[...])
        h1_ref[...] = h1
        xn_ref[...] = _rms(h1, gm_ref[...]).astype(BF16)
        acc_ref[...] = jnp.zeros_like(acc_ref)

    hid = jnp.square(jnp.maximum(_dot(xn_ref[...], wu_ref[...]), 0.0))
    acc_ref[...] += _dot(hid.astype(BF16), wd_ref[...])

    @pl.when(f == pl.num_programs(1) - 1)
    def _():
        h2 = h1_ref[...] + acc_ref[...]
        gate = _sigmoid(_dot(_rms(h2, gp_ref[...]).astype(BF16), wg_ref[...]))
        h3 = h2 + _dot(p_ref[...].astype(BF16), wp_ref[...]) * gate
        if final:
            outs[0][...] = _rms(h3, gn_ref[...])
        else:
            outs[0][...] = h3
            outs[1][...] = _rms(h3, gn_ref[...]).astype(BF16)


def layer_tail(h, y, w_out, mixer_idx, g_mlp, w_up, w_down, p, g_ple, w_gate, w_proj, layer, g_next, *, tm, tf, final):
    m, d = h.shape
    k = y.shape[1]
    ff = w_up.shape[2]
    pd = p.shape[2]
    row = lambda i, f: (i, 0)
    fixed = lambda i, f: (0, 0)
    once = pl.Buffered(1)
    out_specs = [pl.BlockSpec((tm, d), row)]
    out_shape = [jax.ShapeDtypeStruct((m, d), F32)]
    if not final:
        out_specs.append(pl.BlockSpec((tm, d), row))
        out_shape.append(jax.ShapeDtypeStruct((m, d), BF16))
    outs = pl.pallas_call(
        functools.partial(_layer_tail_kernel, final=final),
        grid=(m // tm, ff // tf),
        in_specs=[pl.BlockSpec((tm, d), row),
                  pl.BlockSpec((tm, k), row),
                  pl.BlockSpec((None, k, d), lambda i, f: (mixer_idx, 0, 0), pipeline_mode=once),
                  pl.BlockSpec((1, d), fixed),
                  pl.BlockSpec((None, d, tf), lambda i, f: (layer, 0, f)),
                  pl.BlockSpec((None, tf, d), lambda i, f: (layer, f, 0)),
                  pl.BlockSpec((None, tm, pd), lambda i, f: (layer, i, 0)),
                  pl.BlockSpec((1, d), fixed),
                  pl.BlockSpec((None, d, d), lambda i, f: (layer, 0, 0), pipeline_mode=once),
                  pl.BlockSpec((None, pd, d), lambda i, f: (layer, 0, 0), pipeline_mode=once),
                  pl.BlockSpec((1, d), fixed)],
        out_specs=out_specs,
        out_shape=out_shape,
        scratch_shapes=[pltpu.VMEM((tm, d), F32), pltpu.VMEM((tm, d), BF16), pltpu.VMEM((tm, d), F32)],
        compiler_params=_cparams("parallel", "arbitrary"),
        name="layer_tail",
    )(h, y, w_out, g_mlp.reshape(1, d), w_up, w_down, p, g_ple.reshape(1, d), w_gate, w_proj, g_next.reshape(1, d))
    return outs[0] if final else tuple(outs)


def _ssd_head_select():
    rows = jnp.arange(LANES)[:, None]
    cols = jnp.arange(SSD_HEADS * LANES)[None, :] // LANES
    return ((rows % SSD_HEADS == cols) & (rows < 3 * SSD_HEADS)).astype(BF16)


def _ssd_chunk_kernel(z_ref, xbc_ref, dt_ref, cw_ref, cb_ref, dtb_ref, alog_ref, dskip_ref, ng_ref, sel_ref,
                      y_ref, conv_ref, state_ref, xp_ref, act_ref, acol_ref, st_ref, xbd_ref, sbd_ref):
    c = pl.program_id(1)
    t = SSD_CHUNK
    gw = SSD_GROUP_W
    hpg = SSD_HEADS_PER_GROUP
    hd = SSD_HEAD_DIM

    @pl.when(c == 0)
    def _():
        xp_ref[:, 0:SUBLANES, :] = jnp.zeros((xp_ref.shape[0], SUBLANES, LANES), F32)
        st_ref[...] = jnp.zeros_like(st_ref)
        xbd_ref[...] = jnp.zeros_like(xbd_ref)
        sbd_ref[...] = jnp.zeros_like(sbd_ref)

    for j in range(SSD_CONV_DIM // LANES):
        cols = slice(j * LANES, (j + 1) * LANES)
        xp_ref[j, SUBLANES:SUBLANES + t, :] = xbc_ref[:, cols]
        acc = xp_ref[j, SUBLANES - 3:SUBLANES - 3 + t, :] * cw_ref[0:1, cols]
        for k in range(1, CONV_W):
            acc = acc + xp_ref[j, SUBLANES - 3 + k:SUBLANES - 3 + k + t, :] * cw_ref[k:k + 1, cols]
        act_ref[:, cols] = _silu(acc + cb_ref[:, cols])
        xp_ref[j, 0:SUBLANES, :] = xbc_ref[t - SUBLANES:t, cols]
    conv_ref[0] = xbc_ref[t - SUBLANES:t, :]

    dtv = _softplus(dt_ref[...] + dtb_ref[...])
    da = dtv * (-jnp.exp(alog_ref[...]))
    row = lax.broadcasted_iota(jnp.int32, (t, t), 0)
    col = lax.broadcasted_iota(jnp.int32, (t, t), 1)
    causal = row >= col
    tril = causal.astype(BF16)
    d_hi, d_mid, d_lo = _split3(da)
    a2 = (_dot(tril, d_hi) + (_dot(tril, d_mid) + _dot(tril, d_lo))) * LOG2E
    arow_t = (a2 - jnp.log(dtv) * LOG2E).T

    lane = lax.broadcasted_iota(jnp.int32, (t, LANES), 1)
    a_hi, a_mid, a_lo = _split3(jnp.where(lane < SSD_HEADS, a2, 0.0))
    packed = (a_hi.astype(F32) + pltpu.roll(a_mid.astype(F32), SSD_HEADS, 1)
              + pltpu.roll(a_lo.astype(F32), 2 * SSD_HEADS, 1)).astype(BF16)
    acol_ref[...] = _dot(packed, sel_ref[...])

    lane_g = lax.broadcasted_iota(jnp.int32, (1, gw), 1) // hd
    for g in range(SSD_GROUPS):
        xs_g = act_ref[:, g * gw:(g + 1) * gw]
        b_g = act_ref[:, SSD_D_INNER + g * SSD_STATE:SSD_D_INNER + (g + 1) * SSD_STATE]
        c_g = act_ref[:, SSD_D_INNER + SSD_BC_DIM + g * SSD_STATE:
                      SSD_D_INNER + SSD_BC_DIM + (g + 1) * SSD_STATE]
        cb = jnp.where(causal, _dot_nt(c_g.astype(BF16), b_g.astype(BF16)), 0.0)
        b_gt = b_g.T
        st_g = st_ref[g]
        xs_b = xs_g.astype(BF16)
        for r in range(hpg):
            xbd_ref[g, r * t:(r + 1) * t, r * hd:(r + 1) * hd] = xs_b[:, r * hd:(r + 1) * hd]
        m_parts, d_parts, w_parts = [], [], []
        dec_g = jnp.zeros((1, gw), F32)
        for r in range(hpg):
            h = g * hpg + r
            acol = acol_ref[:, h * LANES:(h + 1) * LANES]
            arow = arow_t[h:h + 1, :]
            aend = acol[t - 1:t, :]
            m_parts.append((cb * jnp.exp2(jnp.minimum(acol - arow, EXP2_CAP))).astype(BF16))
            d_parts.append((c_g * jnp.exp2(acol)).astype(BF16))
            w_parts.append((b_gt * jnp.exp2(aend - arow)).astype(BF16))
            dec_g = jnp.where(lane_g == r, jnp.exp2(jnp.concatenate([aend, aend], axis=1)), dec_g)
        x_bd = xbd_ref[g]
        y_g = (_dot(jnp.concatenate(m_parts, axis=1), x_bd) + _dot(jnp.concatenate(d_parts, axis=1), sbd_ref[g])
               + xs_g * dskip_ref[:, g * gw:(g + 1) * gw])
        st_new = st_g * dec_g + _dot(jnp.concatenate(w_parts, axis=1), x_bd)
        st_ref[g] = st_new
        st_b = st_new.astype(BF16)
        for r in range(hpg):
            sbd_ref[g, r * SSD_STATE:(r + 1) * SSD_STATE, r * hd:(r + 1) * hd] = st_b[:, r * hd:(r + 1) * hd]
        yz = y_g * _silu(z_ref[:, g * gw:(g + 1) * gw])
        y_ref[:, g * gw:(g + 1) * gw] = _rms(yz, ng_ref[:, g * gw:(g + 1) * gw]).astype(y_ref.dtype)

    @pl.when(c == pl.num_programs(1) - 1)
    def _():
        for g in range(SSD_GROUPS):
            state_ref[0, g * hpg:(g + 1) * hpg] = st_ref[g].T.reshape(hpg, SSD_HEAD_DIM, SSD_STATE)


def ssd_prompt(z, xbc, dt, conv_w, conv_b, dt_bias, a_log, d_skip, norm_g, batch, seq):
    t = SSD_CHUNK
    nc = seq // t
    pad = LANES - SSD_HEADS
    rowblk = lambda b, c: (b * nc + c, 0)
    fixed = lambda b, c: (0, 0)
    y, conv_tail, state = pl.pallas_call(
        _ssd_chunk_kernel,
        grid=(batch, nc),
        in_specs=[pl.BlockSpec((t, SSD_D_INNER), rowblk),
                  pl.BlockSpec((t, SSD_CONV_DIM), rowblk),
                  pl.BlockSpec((t, LANES), rowblk),
                  pl.BlockSpec((CONV_W, SSD_CONV_DIM), fixed),
                  pl.BlockSpec((1, SSD_CONV_DIM), fixed),
                  pl.BlockSpec((1, LANES), fixed),
                  pl.BlockSpec((1, LANES), fixed),
                  pl.BlockSpec((1, SSD_D_INNER), fixed),
                  pl.BlockSpec((1, SSD_D_INNER), fixed),
                  pl.BlockSpec((LANES, SSD_HEADS * LANES), fixed)],
        out_specs=[pl.BlockSpec((t, SSD_D_INNER), rowblk),
                   pl.BlockSpec((1, SUBLANES, SSD_CONV_DIM), lambda b, c: (b, 0, 0)),
                   pl.BlockSpec((1, SSD_HEADS, SSD_HEAD_DIM, SSD_STATE), lambda b, c: (b, 0, 0, 0))],
        out_shape=[jax.ShapeDtypeStruct((batch * seq, SSD_D_INNER), BF16),
                   jax.ShapeDtypeStruct((batch, SUBLANES, SSD_CONV_DIM), F32),
                   jax.ShapeDtypeStruct((batch, SSD_HEADS, SSD_HEAD_DIM, SSD_STATE), F32)],
        scratch_shapes=[pltpu.VMEM((SSD_CONV_DIM // LANES, SUBLANES + t, LANES), F32),
                        pltpu.VMEM((t, SSD_CONV_DIM), F32),
                        pltpu.VMEM((t, SSD_HEADS * LANES), F32),
                        pltpu.VMEM((SSD_GROUPS, SSD_STATE, SSD_GROUP_W), F32),
                        pltpu.VMEM((SSD_GROUPS, SSD_HEADS_PER_GROUP * t, SSD_GROUP_W), BF16),
                        pltpu.VMEM((SSD_GROUPS, SSD_HEADS_PER_GROUP * SSD_STATE, SSD_GROUP_W), BF16)],
        compiler_params=_cparams("parallel", "arbitrary"),
        name="ssd_prompt",
    )(z, xbc, dt, conv_w, conv_b.reshape(1, -1), jnp.pad(dt_bias, (0, pad)).reshape(1, LANES),
      jnp.pad(a_log, (0, pad)).reshape(1, LANES), jnp.repeat(d_skip, SSD_HEAD_DIM).reshape(1, -1),
      norm_g.reshape(1, -1), _ssd_head_select())
    return y, conv_tail[:, SUBLANES - (CONV_W - 1):, :], state


MOBA_SPAN = 2 * MOBA_BLOCK
MOBA_HEADS_PER_STEP = 2


def _col_reduce(x, pair_op, reduce_fn):
    n = x.shape[0]
    while n > 8 * SUBLANES and n % 2 == 0:
        n //= 2
        x = pair_op(x[:n], x[n:])
    return reduce_fn(x, axis=0, keepdims=True)


def _moba_prompt_kernel(q_ref, k_ref, v_ref, o_ref, kb_ref, vt_ref, sel_ref, s_ref, p_ref):
    qi = pl.program_id(2)
    blk = MOBA_BLOCK
    seq = k_ref.shape[0]
    nb = seq // blk
    nbp = sel_ref.shape[2]
    hd = MOBA_HEAD_DIM
    heads = q_ref.shape[1] // hd

    def prepare(hh):
        cols = slice(hh * hd, (hh + 1) * hd)
        means = []
        for j in range(nb):
            kj = k_ref[j * blk:(j + 1) * blk, cols]
            means.append(jnp.sum(kj, axis=0, keepdims=True) * (1.0 / blk))
            kb_ref[hh, j * blk:(j + 1) * blk, :] = kj.astype(BF16)
        for j in range(seq // LANES):
            vt_ref[hh, :, j * LANES:(j + 1) * LANES] = v_ref[j * LANES:(j + 1) * LANES, cols].T.astype(BF16)
        if nbp > nb:
            means.append(jnp.zeros((nbp - nb, hd), F32))
        kmean = jnp.concatenate(means, axis=0)
        sub = lax.broadcasted_iota(jnp.int32, (nbp, blk), 0)
        for i in range(nb):
            gm = jnp.where(sub < i, _dot3_nt(kmean, q_ref[i * blk:(i + 1) * blk, cols]), -jnp.inf)
            rank = jnp.zeros((nbp, blk), F32)
            for m in range(i):
                gm_m = gm[m:m + 1, :]
                beats = (gm_m > gm) | ((gm_m == gm) & (sub > m))
                rank = rank + jnp.where(beats, 1.0, 0.0)
            chosen = ((sub < i) & (rank < MOBA_TOPK)) | (sub == i)
            sel_ref[hh, i] = jnp.where(chosen, 1.0, 0.0)

    @pl.when(qi == 0)
    def _():
        for hh in range(heads):
            prepare(hh)

    key_minus_query = (lax.broadcasted_iota(jnp.int32, (blk, blk), 0)
                       - lax.broadcasted_iota(jnp.int32, (blk, blk), 1))

    span = min(MOBA_SPAN, seq)
    per_span = span // blk

    def attend_head(hh, n_keys):
        cols = slice(hh * hd, (hh + 1) * hd)
        q = q_ref[pl.ds(pl.multiple_of(qi * blk, blk), blk), cols]
        qb = (q * (hd ** -0.5)).astype(BF16)
        sel = sel_ref[hh, qi]
        m = None
        for j in range(n_keys // blk):
            rows = slice(j * blk, (j + 1) * blk)
            bias = jnp.where(sel[j:j + 1, :] > 0.0, 0.0, NEG_BIG)
            s = jnp.where(key_minus_query <= (qi - j) * blk, _dot_nt(kb_ref[hh, rows, :], qb), NEG_BIG) + bias
            s_ref[hh, rows, :] = s
            bm = _col_reduce(s, jnp.maximum, jnp.max)
            m = bm if m is None else jnp.maximum(m, bm)
        l = jnp.zeros((1, blk), F32)
        for j in range(n_keys // blk):
            rows = slice(j * blk, (j + 1) * blk)
            p = jnp.exp(s_ref[hh, rows, :] - m)
            l = l + _col_reduce(p, jnp.add, jnp.sum)
            p_ref[hh, rows, :] = p.astype(BF16)
        acc = _dot(vt_ref[hh, :, 0:n_keys], p_ref[hh, 0:n_keys, :])
        o_ref[:, cols] = (acc / l).T.astype(o_ref.dtype)

    def attend(n_keys):
        for hh in range(heads):
            attend_head(hh, n_keys)

    for sp in range(seq // span):
        in_span = (qi >= sp * per_span) & (qi < (sp + 1) * per_span)
        pl.when(in_span)(functools.partial(attend, (sp + 1) * span))


def moba_prompt(q, k, v, batch, seq):
    nb = seq // MOBA_BLOCK
    hd = MOBA_HEAD_DIM
    hps = MOBA_HEADS_PER_STEP
    wide = lambda b, h, i: (b, h)
    return pl.pallas_call(
        _moba_prompt_kernel,
        grid=(batch, MOBA_HEADS // hps, nb),
        in_specs=[pl.BlockSpec((seq, hps * hd), wide),
                  pl.BlockSpec((seq, hps * hd), wide),
                  pl.BlockSpec((seq, hps * hd), wide)],
        out_specs=pl.BlockSpec((MOBA_BLOCK, hps * hd), lambda b, h, i: (b * nb + i, h)),
        out_shape=jax.ShapeDtypeStruct((batch * seq, D_MODEL), BF16),
        scratch_shapes=[pltpu.VMEM((hps, seq, hd), BF16),
                        pltpu.VMEM((hps, hd, seq), BF16),
                        pltpu.VMEM((hps, nb, -(-nb // SUBLANES) * SUBLANES, MOBA_BLOCK), F32),
                        pltpu.VMEM((hps, seq, MOBA_BLOCK), F32),
                        pltpu.VMEM((hps, seq, MOBA_BLOCK), BF16)],
        compiler_params=_cparams("parallel", "parallel", "arbitrary"),
        name="moba_prompt",
    )(q, k, v)


def _lru_gates(xc, wa, ba, wx, bx, lam):
    xb = xc.astype(BF16)
    r = _sigmoid(_dot(xb, wa) + ba)
    i = _sigmoid(_dot(xb, wx) + bx)
    log_a = (-LRU_C) * r * _softplus(-lam)
    a = jnp.exp(log_a)
    u = jnp.sqrt(-jnp.tanh(log_a) * (a * a + 1.0)) * (i * xc)
    return a, u


def _lru_chunk_kernel(gate_ref, x_ref, cw_ref, cb_ref, wa_ref, ba_ref, wx_ref, bx_ref, lam_ref,
                      y_ref, conv_ref, hlast_ref, xp_ref, a_ref, u_ref, h_ref):
    c = pl.program_id(1)
    t = LRU_CHUNK
    w = LRU_BLOCK_W

    @pl.when(c == 0)
    def _():
        xp_ref[:, 0:SUBLANES, :] = jnp.zeros((LRU_BLOCKS, SUBLANES, w), F32)
        h_ref[...] = jnp.zeros_like(h_ref)

    for n in range(LRU_BLOCKS):
        cols = slice(n * w, (n + 1) * w)
        xp_ref[n, SUBLANES:SUBLANES + t, :] = x_ref[:, cols]
        acc = xp_ref[n, SUBLANES - 3:SUBLANES - 3 + t, :] * cw_ref[0:1, cols]
        for k in range(1, CONV_W):
            acc = acc + xp_ref[n, SUBLANES - 3 + k:SUBLANES - 3 + k + t, :] * cw_ref[k:k + 1, cols]
        xc = acc + cb_ref[:, cols]
        a, u = _lru_gates(xc, wa_ref[n], ba_ref[:, cols], wx_ref[n], bx_ref[:, cols], lam_ref[:, cols])
        a_ref[:, cols] = a
        u_ref[:, cols] = u
        xp_ref[n, 0:SUBLANES, :] = x_ref[t - SUBLANES:t, cols]
    conv_ref[0] = x_ref[t - SUBLANES:t, :]

    srow = lax.broadcasted_iota(jnp.int32, (SUBLANES, LRU_WIDTH), 0)

    def tile(i, h):
        rows = pl.ds(pl.multiple_of(i * SUBLANES, SUBLANES), SUBLANES)
        a = a_ref[rows, :]
        u = u_ref[rows, :]
        for d in (1, 2, 4):
            keep = srow >= d
            a_sh = jnp.where(keep, pltpu.roll(a, d, 0), 1.0)
            u_sh = jnp.where(keep, pltpu.roll(u, d, 0), 0.0)
            u = a * u_sh + u
            a = a * a_sh
        hs = a * h + u
        u_ref[rows, :] = hs
        return jnp.broadcast_to(hs[SUBLANES - 1:SUBLANES, :], (SUBLANES, LRU_WIDTH))

    h = lax.fori_loop(0, t // SUBLANES, tile, h_ref[...])
    h_ref[...] = h
    hlast_ref[0] = h
    y_ref[...] = (_gelu_tanh(gate_ref[...]) * u_ref[...]).astype(y_ref.dtype)


def lru_prompt(gate_br, x_br, conv_w, conv_b, w_a, b_a, w_x, b_x, lam, batch, seq):
    t = LRU_CHUNK
    nc = seq // t
    wd = LRU_WIDTH
    rowblk = lambda b, c: (b * nc + c, 0)
    fixed = lambda b, c: (0, 0)
    fixed3 = lambda b, c: (0, 0, 0)
    perb = lambda b, c: (b, 0, 0)
    y, conv_tail, h_last = pl.pallas_call(
        _lru_chunk_kernel,
        grid=(batch, nc),
        in_specs=[pl.BlockSpec((t, wd), rowblk),
                  pl.BlockSpec((t, wd), rowblk),
                  pl.BlockSpec((CONV_W, wd), fixed),
                  pl.BlockSpec((1, wd), fixed),
                  pl.BlockSpec((LRU_BLOCKS, LRU_BLOCK_W, LRU_BLOCK_W), fixed3),
                  pl.BlockSpec((1, wd), fixed),
                  pl.BlockSpec((LRU_BLOCKS, LRU_BLOCK_W, LRU_BLOCK_W), fixed3),
                  pl.BlockSpec((1, wd), fixed),
                  pl.BlockSpec((1, wd), fixed)],
        out_specs=[pl.BlockSpec((t, wd), rowblk),
                   pl.BlockSpec((1, SUBLANES, wd), perb),
                   pl.BlockSpec((1, SUBLANES, wd), perb)],
        out_shape=[jax.ShapeDtypeStruct((batch * seq, wd), BF16),
                   jax.ShapeDtypeStruct((batch, SUBLANES, wd), F32),
                   jax.ShapeDtypeStruct((batch, SUBLANES, wd), F32)],
        scratch_shapes=[pltpu.VMEM((LRU_BLOCKS, SUBLANES + t, LRU_BLOCK_W), F32),
                        pltpu.VMEM((t, wd), F32),
                        pltpu.VMEM((t, wd), F32),
                        pltpu.VMEM((SUBLANES, wd), F32)],
        compiler_params=_cparams("parallel", "arbitrary"),
        name="lru_prompt",
    )(gate_br, x_br, conv_w, conv_b.reshape(1, wd), w_a.astype(BF16), b_a.reshape(1, wd),
      w_x.astype(BF16), b_x.reshape(1, wd), lam.reshape(1, wd))
    return y, conv_tail[:, SUBLANES - (CONV_W - 1):, :], h_last[:, 0, :]


def _ssd_step_kernel(z_ref, xbc_ref, dt_ref, cs_ref, st_ref, cw_ref, cb_ref, dtb_ref, alog_ref, dskip_ref, ng_ref,
                     y_ref, cso_ref, sto_ref):
    gw = SSD_GROUP_W
    hpg = SSD_HEADS_PER_GROUP
    x = xbc_ref[0]
    buf = cs_ref[0]
    conv = x * cw_ref[CONV_W - 1:CONV_W, :]
    for k in range(CONV_W - 1):
        conv = conv + buf[k:k + 1, :] * cw_ref[k:k + 1, :]
    act = _silu(conv + cb_ref[...])
    cso_ref[0, 0:CONV_W - 2, :] = buf[1:CONV_W - 1, :]
    cso_ref[0, CONV_W - 2:CONV_W - 1, :] = x

    dtv = _softplus(dt_ref[0] + dtb_ref[...])
    dec = jnp.exp(dtv * (-jnp.exp(alog_ref[...])))
    lane_g = lax.broadcasted_iota(jnp.int32, (1, gw), 1) // SSD_HEAD_DIM
    first_row = lax.broadcasted_iota(jnp.int32, (SUBLANES, gw), 0) == 0
    ys = []
    for g in range(SSD_GROUPS):
        xs_g = act[:, g * gw:(g + 1) * gw]
        b_g = act[:, SSD_D_INNER + g * SSD_STATE:SSD_D_INNER + (g + 1) * SSD_STATE]
        c_g = act[:, SSD_D_INNER + SSD_BC_DIM + g * SSD_STATE:SSD_D_INNER + SSD_BC_DIM + (g + 1) * SSD_STATE]
        dt_e = jnp.zeros((1, gw), F32)
        dec_rows = []
        for r in range(hpg):
            h = g * hpg + r
            dt_e = jnp.where(lane_g == r, dtv[:, h:h + 1], dt_e)
            dec_rows.append(jnp.broadcast_to(dec[:, h:h + 1], (SSD_HEAD_DIM, SSD_STATE)))
        xdt = xs_g * dt_e
        x8 = jnp.where(first_row, jnp.broadcast_to(xdt, (SUBLANES, gw)), 0.0).astype(BF16)
        b8 = jnp.broadcast_to(b_g, (SUBLANES, SSD_STATE)).astype(BF16)
        outer = lax.dot_general(x8, b8, (((0,), (0,)), ((), ())), preferred_element_type=F32)
        old = st_ref[0, g * hpg:(g + 1) * hpg].reshape(gw, SSD_STATE)
        new = old * jnp.concatenate(dec_rows, axis=0) + outer
        sto_ref[0, g * hpg:(g + 1) * hpg] = new.reshape(hpg, SSD_HEAD_DIM, SSD_STATE)
        c8 = jnp.broadcast_to(c_g, (SUBLANES, SSD_STATE)).astype(BF16)
        y_g = _dot_nt(c8, new.astype(BF16))[0:1, :] + xs_g * dskip_ref[:, g * gw:(g + 1) * gw]
        yz = y_g * _silu(z_ref[0][:, g * gw:(g + 1) * gw])
        ys.append(_rms(yz, ng_ref[:, g * gw:(g + 1) * gw]))
    y_ref[0] = jnp.concatenate(ys, axis=1)


def ssd_step(z, xbc, dt, conv_state, ssm_state, layer, conv_w, conv_b, dt_bias, a_log, d_skip, norm_g):
    nb = z.shape[0]
    pad = LANES - SSD_HEADS
    per3 = lambda b: (b, 0, 0)
    fixed = lambda b: (0, 0)
    y, conv_new, state_new = pl.pallas_call(
        _ssd_step_kernel,
        grid=(nb,),
        in_specs=[pl.BlockSpec((1, 1, SSD_D_INNER), per3),
                  pl.BlockSpec((1, 1, SSD_CONV_DIM), per3),
                  pl.BlockSpec((1, 1, LANES), per3),
                  pl.BlockSpec((None, 1, CONV_W - 1, SSD_CONV_DIM), lambda b: (layer, b, 0, 0)),
                  pl.BlockSpec((None, 1, SSD_HEADS, SSD_HEAD_DIM, SSD_STATE), lambda b: (layer, b, 0, 0, 0)),
                  pl.BlockSpec((CONV_W, SSD_CONV_DIM), fixed),
                  pl.BlockSpec((1, SSD_CONV_DIM), fixed),
                  pl.BlockSpec((1, LANES), fixed),
                  pl.BlockSpec((1, LANES), fixed),
                  pl.BlockSpec((1, SSD_D_INNER), fixed),
                  pl.BlockSpec((1, SSD_D_INNER), fixed)],
        out_specs=[pl.BlockSpec((1, 1, SSD_D_INNER), per3),
                   pl.BlockSpec((1, CONV_W - 1, SSD_CONV_DIM), per3),
                   pl.BlockSpec((1, SSD_HEADS, SSD_HEAD_DIM, SSD_STATE), lambda b: (b, 0, 0, 0))],
        out_shape=[jax.ShapeDtypeStruct((nb, 1, SSD_D_INNER), F32),
                   jax.ShapeDtypeStruct((nb, CONV_W - 1, SSD_CONV_DIM), F32),
                   jax.ShapeDtypeStruct((nb, SSD_HEADS, SSD_HEAD_DIM, SSD_STATE), F32)],
        compiler_params=_cparams("parallel"),
        name="ssd_step",
    )(z.reshape(nb, 1, -1), xbc.reshape(nb, 1, -1), dt.reshape(nb, 1, -1), conv_state, ssm_state,
      conv_w, conv_b.reshape(1, -1), jnp.pad(dt_bias, (0, pad)).reshape(1, LANES),
      jnp.pad(a_log, (0, pad)).reshape(1, LANES), jnp.repeat(d_skip, SSD_HEAD_DIM).reshape(1, -1),
      norm_g.reshape(1, -1))
    return y.reshape(nb, SSD_D_INNER), conv_new, state_new


def _lru_step_kernel(gate_ref, x_ref, cs_ref, h0_ref, cw_ref, cb_ref, wa_ref, ba_ref, wx_ref, bx_ref, lam_ref,
                     y_ref, cso_ref, ho_ref):
    w = LRU_BLOCK_W
    x = x_ref[...]
    xc = x * cw_ref[CONV_W - 1:CONV_W, :] + cb_ref[...]
    for k in range(CONV_W - 1):
        xc = xc + cs_ref[k] * cw_ref[k:k + 1, :]
    for k in range(CONV_W - 2):
        cso_ref[k] = cs_ref[k + 1]
    cso_ref[CONV_W - 2] = x
    for n in range(LRU_BLOCKS):
        cols = slice(n * w, (n + 1) * w)
        a, u = _lru_gates(xc[:, cols], wa_ref[n], ba_ref[:, cols], wx_ref[n], bx_ref[:, cols], lam_ref[:, cols])
        h = a * h0_ref[:, cols] + u
        ho_ref[:, cols] = h
        y_ref[:, cols] = _gelu_tanh(gate_ref[:, cols]) * h


def lru_step(gate_br, x_br, conv_state, h0, conv_w, conv_b, w_a, b_a, w_x, b_x, lam):
    nb, wd = x_br.shape
    return pl.pallas_call(
        _lru_step_kernel,
        out_shape=[jax.ShapeDtypeStruct((nb, wd), F32),
                   jax.ShapeDtypeStruct((CONV_W - 1, nb, wd), F32),
                   jax.ShapeDtypeStruct((nb, wd), F32)],
        compiler_params=pltpu.CompilerParams(vmem_limit_bytes=VMEM_LIMIT),
        name="lru_step",
    )(gate_br, x_br, conv_state, h0, conv_w, conv_b.reshape(1, wd), w_a.astype(BF16), b_a.reshape(1, wd),
      w_x.astype(BF16), b_x.reshape(1, wd), lam.reshape(1, wd))


MOBA_SUM_PAGES = 16


def _moba_block_sums_kernel(tbl_ref, *refs):
    del tbl_ref
    pages, o_ref = refs[:-1], refs[-1]
    for k in range(len(pages) // PAGES_PER_BLOCK):
        s = jnp.sum(pages[PAGES_PER_BLOCK * k][...], axis=0)
        for j in range(1, PAGES_PER_BLOCK):
            s = s + jnp.sum(pages[PAGES_PER_BLOCK * k + j][...], axis=0)
        o_ref[0, k] = s


def moba_block_sums(pool, layer, table):
    nb, n_pages = table.shape
    steps = n_pages // MOBA_SUM_PAGES
    blocks_per_step = MOBA_SUM_PAGES // PAGES_PER_BLOCK
    page_shape = pool.shape[2:]

    def page_spec(k):
        return pl.BlockSpec((None, None) + page_shape,
                            lambda b, c, tbl: (layer, tbl[b * n_pages + c * MOBA_SUM_PAGES + k], 0, 0, 0))

    return pl.pallas_call(
        _moba_block_sums_kernel,
        grid_spec=pltpu.PrefetchScalarGridSpec(
            num_scalar_prefetch=1,
            grid=(nb, steps),
            in_specs=[page_spec(k) for k in range(MOBA_SUM_PAGES)],
            out_specs=pl.BlockSpec((1, blocks_per_step) + page_shape[1:], lambda b, c, tbl: (b, c, 0, 0))),
        out_shape=jax.ShapeDtypeStruct((nb, n_pages // PAGES_PER_BLOCK) + page_shape[1:], F32),
        compiler_params=_cparams("parallel", "arbitrary"),
        name="moba_block_sums",
    )(table.reshape(-1), *([pool] * MOBA_SUM_PAGES))


def _moba_gate_kernel(q_ref, bs_ref, sel_ref):
    n_blocks = bs_ref.shape[1]
    q = q_ref[0]
    gates = [jnp.sum((bs_ref[0, n] * (1.0 / MOBA_BLOCK)) * q, axis=-1, keepdims=True) for n in range(n_blocks)]
    lane = lax.broadcasted_iota(jnp.int32, (MOBA_HEADS, LANES), 1)
    out = jnp.zeros((MOBA_HEADS, LANES), F32)
    for k in range(MOBA_TOPK):
        best = functools.reduce(jnp.maximum, gates)
        idx = jnp.full((MOBA_HEADS, 1), float(n_blocks), F32)
        for n in reversed(range(n_blocks)):
            idx = jnp.where(gates[n] == best, float(n), idx)
        out = jnp.where(lane == k, idx, out)
        gates = [jnp.where(idx == float(n), -jnp.inf, gates[n]) for n in range(n_blocks)]
    sel_ref[0] = out.astype(jnp.int32)


def moba_gate(q, block_sums):
    nb = q.shape[0]
    n_blocks = block_sums.shape[1]
    return pl.pallas_call(
        _moba_gate_kernel,
        grid=(nb,),
        in_specs=[pl.BlockSpec((1, MOBA_HEADS, MOBA_HEAD_DIM), lambda b: (b, 0, 0)),
                  pl.BlockSpec((1, n_blocks, MOBA_HEADS, MOBA_HEAD_DIM), lambda b: (b, 0, 0, 0))],
        out_specs=pl.BlockSpec((1, MOBA_HEADS, LANES), lambda b: (b, 0, 0)),
        out_shape=jax.ShapeDtypeStruct((nb, MOBA_HEADS, LANES), jnp.int32),
        compiler_params=_cparams("parallel"),
        name="moba_gate",
    )(q.reshape(nb, MOBA_HEADS, MOBA_HEAD_DIM), block_sums)


def _moba_decode_kernel(tbl_ref, sel_ref, q_ref, kn_ref, vn_ref, kpool_ref, vpool_ref, o_ref, kbuf, vbuf, sem,
                        *, layer, n_table_pages):
    b, h = pl.program_id(0), pl.program_id(1)
    nh = pl.num_programs(1)
    step = b * nh + h
    slot = lax.rem(step, 2)
    n_sel = MOBA_TOPK * PAGES_PER_BLOCK

    def page_copies(bb, hh, sl):
        out = []
        for k in range(MOBA_TOPK):
            blk = sel_ref[(bb * nh + hh) * MOBA_TOPK + k]
            for s in range(PAGES_PER_BLOCK):
                page = tbl_ref[bb * n_table_pages + blk * PAGES_PER_BLOCK + s]
                i = k * PAGES_PER_BLOCK + s
                out.append(pltpu.make_async_copy(kpool_ref.at[layer, page, :, hh, :], kbuf.at[sl, i], sem.at[sl, 0, i]))
                out.append(pltpu.make_async_copy(vpool_ref.at[layer, page, :, hh, :], vbuf.at[sl, i], sem.at[sl, 1, i]))
        return out

    def start_all(copies):
        for i, c in enumerate(copies):
            c.start(priority=i % 2)

    @pl.when(step == 0)
    def _():
        start_all(page_copies(b, h, 0))

    @pl.when(step + 1 < pl.num_programs(0) * nh)
    def _():
        wrap = h + 1 == nh
        start_all(page_copies(jnp.where(wrap, b + 1, b), jnp.where(wrap, 0, h + 1), 1 - slot))

    for c in page_copies(b, h, slot):
        c.wait()

    scale = MOBA_HEAD_DIM ** -0.5
    q = q_ref[0]
    q8 = jnp.broadcast_to(q, (SUBLANES, MOBA_HEAD_DIM)).astype(BF16)
    k_cat = kbuf[slot].reshape(n_sel * PAGE_SIZE, MOBA_HEAD_DIM).astype(BF16)
    v_cat = vbuf[slot].reshape(n_sel * PAGE_SIZE, MOBA_HEAD_DIM).astype(BF16)
    s = _dot_nt(q8, k_cat) * scale
    s_new = jnp.sum(q * kn_ref[0], axis=1, keepdims=True) * scale
    m = jnp.maximum(jnp.max(s, axis=1, keepdims=True), s_new)
    p = jnp.exp(s - m)
    p_new = jnp.exp(s_new - m)
    denom = jnp.sum(p, axis=1, keepdims=True) + p_new
    o = (_dot(p.astype(BF16), v_cat) + p_new * vn_ref[0]) / denom
    o_ref[0] = o[0:1, :]


def moba_decode(q, k_new, v_new, k_pool, v_pool, layer, table, sel):
    nb = q.shape[0]
    n_pages = table.shape[1]
    hd = MOBA_HEAD_DIM
    n_sel = MOBA_TOPK * PAGES_PER_BLOCK
    tok = pl.BlockSpec((1, 1, hd), lambda b, h, tbl, sl: (b, 0, h))
    hbm = pl.BlockSpec(memory_space=pl.ANY)
    out = pl.pallas_call(
        functools.partial(_moba_decode_kernel, layer=layer, n_table_pages=n_pages),
        grid_spec=pltpu.PrefetchScalarGridSpec(
            num_scalar_prefetch=2,
            grid=(nb, MOBA_HEADS),
            in_specs=[tok, tok, tok, hbm, hbm],
            out_specs=tok,
            scratch_shapes=[pltpu.VMEM((2, n_sel, PAGE_SIZE, hd), F32),
                            pltpu.VMEM((2, n_sel, PAGE_SIZE, hd), F32),
                            pltpu.SemaphoreType.DMA((2, 2, n_sel))]),
        out_shape=jax.ShapeDtypeStruct((nb, 1, D_MODEL), F32),
        compiler_params=_cparams("arbitrary", "arbitrary"),
        name="moba_decode",
    )(table.reshape(-1), sel[:, :, :MOBA_TOPK].reshape(-1),
      q.reshape(nb, 1, D_MODEL), k_new.reshape(nb, 1, D_MODEL), v_new.reshape(nb, 1, D_MODEL), k_pool, v_pool)
    return out.reshape(nb, D_MODEL)


def _prep_weights(W):
    bf = lambda a: a.astype(BF16)
    w_in, w_qkv, w_lru = lax.optimization_barrier((bf(W["ssd_w_in"]), bf(W["moba_w_qkv"]), bf(W["lru_w_in"])))
    zx = SSD_D_INNER + SSD_CONV_DIM
    return dict(
        ssd_w_z=w_in[:, :, :SSD_D_INNER],
        ssd_w_xbc=w_in[:, :, SSD_D_INNER:zx],
        ssd_w_dt=jnp.pad(w_in[:, :, zx:], ((0, 0), (0, 0), (0, LANES - SSD_HEADS))),
        ssd_w_out=bf(W["ssd_w_out"]),
        moba_w_q=w_qkv[:, :, :D_MODEL],
        moba_w_k=w_qkv[:, :, D_MODEL:2 * D_MODEL],
        moba_w_v=w_qkv[:, :, 2 * D_MODEL:],
        moba_w_o=bf(W["moba_w_o"]),
        lru_w_gate=w_lru[:, :, :LRU_WIDTH],
        lru_w_x=w_lru[:, :, LRU_WIDTH:],
        lru_w_out=bf(W["lru_w_out"]),
        mlp_w_up=bf(W["mlp_w_up"]), mlp_w_down=bf(W["mlp_w_down"]),
        ple_w_proj=bf(W["ple_w_proj"]), ple_w_gate=bf(W["ple_w_gate"]),
    )


def _trunk(h, p, W, Wb, tiles, ssd_fn, moba_fn, lru_fn):
    tm, tf = tiles
    ssd_out, moba_out, lru_out = [], [], []
    xn = rmsnorm_bf16(h, W["norm_mix"][0], tm=tm)
    for i in range(DEPTH):
        j, kind = i // N_MIXERS, i % N_MIXERS
        proj = lambda *names: multi_linear(xn, [Wb[n] for n in names], j, tm=tm)
        if kind == 0:
            y, conv, state = ssd_fn(j, *proj("ssd_w_z", "ssd_w_xbc", "ssd_w_dt"))
            ssd_out.append((conv, state))
            w_out = Wb["ssd_w_out"]
        elif kind == 1:
            q, k, v = proj("moba_w_q", "moba_w_k", "moba_w_v")
            moba_out.append((k, v))
            y, w_out = moba_fn(j, q, k, v), Wb["moba_w_o"]
        else:
            y, conv, state = lru_fn(j, *proj("lru_w_gate", "lru_w_x"))
            lru_out.append((conv, state))
            w_out = Wb["lru_w_out"]
        final = i == DEPTH - 1
        g_next = W["norm_final"] if final else W["norm_mix"][i + 1]
        out = layer_tail(h, y, w_out, j, W["norm_mlp"][i], Wb["mlp_w_up"], Wb["mlp_w_down"], p,
                         W["norm_ple"][i], Wb["ple_w_gate"], Wb["ple_w_proj"], i, g_next,
                         tm=tm, tf=tf, final=final)
        h, xn = (out, None) if final else out
    stack = lambda pairs, n: jnp.stack([pr[n] for pr in pairs])
    return (h, stack(ssd_out, 0), stack(ssd_out, 1), stack(moba_out, 0), stack(moba_out, 1),
            stack(lru_out, 0), stack(lru_out, 1))


PROMPT_TILES = (512, 1024)
SAMPLE_TILES = (32, 1024)


def kernel(x_prompt, x_sample, state_ssd_conv, state_ssd, cache_k, cache_v, page_table, state_lru_conv,
           state_lru, p_prompt, p_sample, norm_mix, norm_mlp, norm_ple, norm_final, ssd_w_in, ssd_conv_w,
           ssd_conv_b, ssd_dt_bias, ssd_a_log, ssd_d, ssd_norm, ssd_w_out, moba_w_qkv, moba_w_o, lru_w_in,
           lru_conv_w, lru_conv_b, lru_w_a, lru_b_a, lru_w_x, lru_b_x, lru_lambda, lru_w_out, mlp_w_up,
           mlp_w_down, ple_w_proj, ple_w_gate):
    W = dict(norm_mix=norm_mix, norm_mlp=norm_mlp, norm_ple=norm_ple, norm_final=norm_final,
             ssd_w_in=ssd_w_in, ssd_w_out=ssd_w_out, moba_w_qkv=moba_w_qkv, moba_w_o=moba_w_o,
             lru_w_in=lru_w_in, lru_w_out=lru_w_out, mlp_w_up=mlp_w_up, mlp_w_down=mlp_w_down,
             ple_w_proj=ple_w_proj, ple_w_gate=ple_w_gate)
    Wb = _prep_weights(W)
    batch, seq, d = x_prompt.shape
    dec_batch, dec_seq, _ = x_sample.shape
    assert dec_seq == 1 and seq % MOBA_BLOCK == 0 and seq % LRU_CHUNK == 0

    ssd_args = lambda j: (ssd_conv_w[j], ssd_conv_b[j], ssd_dt_bias[j], ssd_a_log[j], ssd_d[j], ssd_norm[j])
    lru_args = lambda j: (lru_conv_w[j], lru_conv_b[j], lru_w_a[j], lru_b_a[j].reshape(-1), lru_w_x[j],
                          lru_b_x[j].reshape(-1), lru_lambda[j])

    out_p = _trunk(
        x_prompt.reshape(batch * seq, d), p_prompt.reshape(DEPTH, batch * seq, PLE_DIM), W, Wb, PROMPT_TILES,
        lambda j, z, xbc, dt: ssd_prompt(z, xbc, dt, *ssd_args(j), batch, seq),
        lambda j, q, k, v: moba_prompt(q, k, v, batch, seq),
        lambda j, gate, x: lru_prompt(gate, x, *lru_args(j), batch, seq))

    def moba_sample(j, q, k, v):
        sel = moba_gate(q, moba_block_sums(cache_k, j, page_table))
        return moba_decode(q, k, v, cache_k, cache_v, j, page_table, sel)

    def lru_sample(j, gate, x):
        y, conv, h = lru_step(gate, x, jnp.swapaxes(state_lru_conv[j], 0, 1), state_lru[j], *lru_args(j))
        return y, jnp.swapaxes(conv, 0, 1), h

    out_s = _trunk(
        x_sample.reshape(dec_batch, d), p_sample.reshape(DEPTH, dec_batch, PLE_DIM), W, Wb, SAMPLE_TILES,
        lambda j, z, xbc, dt: ssd_step(z, xbc, dt, state_ssd_conv, state_ssd, j, *ssd_args(j)),
        moba_sample, lru_sample)

    def shaped(out, b, s):
        y, ssd_conv, ssd_state, k, v, lru_conv, lru_state = out
        kv_shape = (-1, b, s, MOBA_HEADS, MOBA_HEAD_DIM)
        return (y.reshape(b, s, d), ssd_conv, ssd_state, k.reshape(kv_shape), v.reshape(kv_shape),
                lru_conv, lru_state)

    yp, *rest_p = shaped(out_p, batch, seq)
    ys, *rest_s = shaped(out_s, dec_batch, dec_seq)
    return (yp, ys, *rest_p, *rest_s)
```

```python
import functools
import math

import jax
import jax.numpy as jnp
from jax import lax
from jax.experimental import pallas as pl
from jax.experimental.pallas import tpu as pltpu

D_MODEL = 1024
DEPTH = 4
N_MIXERS = 3
PLE_DIM = 256
D_FF = 4 * D_MODEL
NORM_EPS = 1e-6
CONV_W = 4
PAGE_SIZE = 128

SSD_D_INNER = 2 * D_MODEL
SSD_HEAD_DIM = 64
SSD_HEADS = SSD_D_INNER // SSD_HEAD_DIM
SSD_GROUPS = 8
SSD_HEADS_PER_GROUP = SSD_HEADS // SSD_GROUPS
SSD_STATE = 128
SSD_BC_DIM = SSD_GROUPS * SSD_STATE
SSD_CONV_DIM = SSD_D_INNER + 2 * SSD_BC_DIM
SSD_GROUP_W = SSD_D_INNER // SSD_GROUPS
SSD_CHUNK = 128

MOBA_HEADS = 8
MOBA_HEAD_DIM = D_MODEL // MOBA_HEADS
MOBA_BLOCK = 256
MOBA_TOPK = 3
PAGES_PER_BLOCK = MOBA_BLOCK // PAGE_SIZE

LRU_WIDTH = D_MODEL
LRU_BLOCKS = 8
LRU_BLOCK_W = LRU_WIDTH // LRU_BLOCKS
LRU_C = 8.0
LRU_CHUNK = 256

LANES = 128
SUBLANES = 8
VMEM_LIMIT = 56 * 1024 * 1024
NEG_BIG = -1e30
LOG2E = 1.0 / math.log(2.0)
EXP2_CAP = 126.0

F32 = jnp.float32
BF16 = jnp.bfloat16


def _cparams(*sem):
    return pltpu.CompilerParams(dimension_semantics=sem, vmem_limit_bytes=VMEM_LIMIT)


def _rms(x, g):
    var = jnp.mean(x * x, axis=-1, keepdims=True)
    return (x * lax.rsqrt(var + NORM_EPS)) * g


def _sigmoid(x):
    return 0.5 + 0.5 * jnp.tanh(0.5 * x)


def _silu(x):
    hx = 0.5 * x
    return hx + hx * jnp.tanh(hx)


def _softplus(x):
    return jnp.maximum(x, 0.0) + jnp.log1p(jnp.exp(-jnp.abs(x)))


def _gelu_tanh(x):
    return 0.5 * x * (1.0 + jnp.tanh(math.sqrt(2.0 / math.pi) * (x + 0.044715 * (x * x * x))))


def _split3(x):
    hi = x.astype(BF16)
    r1 = x - hi.astype(F32)
    mid = r1.astype(BF16)
    lo = (r1 - mid.astype(F32)).astype(BF16)
    return hi, mid, lo


def _dot(a, b):
    return jnp.dot(a, b, preferred_element_type=F32)


def _dot_nt(a, b):
    return lax.dot_general(a, b, (((1,), (1,)), ((), ())), preferred_element_type=F32)


def _dot3_nt(a, b):
    ah, am, _ = _split3(a)
    bh, bm, _ = _split3(b)
    return _dot_nt(ah, bh) + (_dot_nt(ah, bm) + _dot_nt(am, bh))


def _rmsnorm_kernel(x_ref, g_ref, o_ref):
    o_ref[...] = _rms(x_ref[...], g_ref[...]).astype(o_ref.dtype)


def rmsnorm_bf16(x, g, *, tm):
    m, d = x.shape
    return pl.pallas_call(
        _rmsnorm_kernel,
        grid=(m // tm,),
        in_specs=[pl.BlockSpec((tm, d), lambda i: (i, 0)), pl.BlockSpec((1, d), lambda i: (0, 0))],
        out_specs=pl.BlockSpec((tm, d), lambda i: (i, 0)),
        out_shape=jax.ShapeDtypeStruct((m, d), BF16),
        compiler_params=_cparams("parallel"),
        name="rmsnorm",
    )(x, g.reshape(1, d))


def _multi_linear_kernel(x_ref, *refs):
    n = len(refs) // 2
    x = x_ref[...]
    for w_ref, o_ref in zip(refs[:n], refs[n:]):
        o_ref[...] = _dot(x, w_ref[...])


def multi_linear(x, weights, layer, *, tm):
    m, d = x.shape
    return pl.pallas_call(
        _multi_linear_kernel,
        grid=(m // tm,),
        in_specs=[pl.BlockSpec((tm, d), lambda i: (i, 0))]
                 + [pl.BlockSpec((None,) + w.shape[1:], lambda i: (layer, 0, 0), pipeline_mode=pl.Buffered(1))
                    for w in weights],
        out_specs=[pl.BlockSpec((tm, w.shape[2]), lambda i: (i, 0)) for w in weights],
        out_shape=[jax.ShapeDtypeStruct((m, w.shape[2]), F32) for w in weights],
        compiler_params=_cparams("parallel"),
        name="multi_linear",
    )(x, *weights)


LAYER_TAIL_CHAINS = 2


def _layer_tail_kernel(h_ref, y_ref, wo_ref, gm_ref, wu_ref, wd_ref, p_ref, gp_ref, wg_ref, wp_ref, gn_ref,
                       *outs, final, chains, tf):
    rows_per = h_ref.shape[0] // chains
    ff = wu_ref.shape[1]
    for c in range(chains):
        rows = slice(c * rows_per, (c + 1) * rows_per)
        h1 = h_ref[rows, :] + _dot(y_ref[rows, :].astype(BF16), wo_ref[...])
        xn = _rms(h1, gm_ref[...]).astype(BF16)
        acc = None
        for f in range(ff // tf):
            hid = jnp.square(jnp.maximum(_dot(xn, wu_ref[:, f * tf:(f + 1) * tf]), 0.0))
            part = _dot(hid.astype(BF16), wd_ref[f * tf:(f + 1) * tf, :])
            acc = part if acc is None else acc + part
        h2 = h1 + acc
        gate = _sigmoid(_dot(_rms(h2, gp_ref[...]).astype(BF16), wg_ref[...]))
        h3 = h2 + _dot(p_ref[rows, :].astype(BF16), wp_ref[...]) * gate
        if final:
            outs[0][rows, :] = _rms(h3, gn_ref[...])
        else:
            outs[0][rows, :] = h3
            outs[1][rows, :] = _rms(h3, gn_ref[...]).astype(BF16)


def layer_tail(h, y, w_out, mixer_idx, g_mlp, w_up, w_down, p, g_ple, w_gate, w_proj, layer, g_next, *, tm, tf, final):
    m, d = h.shape
    k = y.shape[1]
    ff = w_up.shape[2]
    pd = p.shape[2]
    row = lambda i: (i, 0)
    fixed = lambda i: (0, 0)
    once = pl.Buffered(1)
    chains = LAYER_TAIL_CHAINS if tm % (LAYER_TAIL_CHAINS * LANES) == 0 else 1
    out_specs = [pl.BlockSpec((tm, d), row)]
    out_shape = [jax.ShapeDtypeStruct((m, d), F32)]
    if not final:
        out_specs.append(pl.BlockSpec((tm, d), row))
        out_shape.append(jax.ShapeDtypeStruct((m, d), BF16))
    outs = pl.pallas_call(
        functools.partial(_layer_tail_kernel, final=final, chains=chains, tf=tf),
        grid=(m // tm,),
        in_specs=[pl.BlockSpec((tm, d), row),
                  pl.BlockSpec((tm, k), row),
                  pl.BlockSpec((None, k, d), lambda i: (mixer_idx, 0, 0), pipeline_mode=once),
                  pl.BlockSpec((1, d), fixed),
                  pl.BlockSpec((None, d, ff), lambda i: (layer, 0, 0), pipeline_mode=once),
                  pl.BlockSpec((None, ff, d), lambda i: (layer, 0, 0), pipeline_mode=once),
                  pl.BlockSpec((None, tm, pd), lambda i: (layer, i, 0)),
                  pl.BlockSpec((1, d), fixed),
                  pl.BlockSpec((None, d, d), lambda i: (layer, 0, 0), pipeline_mode=once),
                  pl.BlockSpec((None, pd, d), lambda i: (layer, 0, 0), pipeline_mode=once),
                  pl.BlockSpec((1, d), fixed)],
        out_specs=out_specs,
        out_shape=out_shape,
        compiler_params=_cparams("parallel"),
        name="layer_tail",
    )(h, y, w_out, g_mlp.reshape(1, d), w_up, w_down, p, g_ple.reshape(1, d), w_gate, w_proj, g_next.reshape(1, d))
    return outs[0] if final else tuple(outs)


def _ssd_head_select():
    rows = jnp.arange(LANES)[:, None]
    cols = jnp.arange(SSD_HEADS * LANES)[None, :] // LANES
    return ((rows % SSD_HEADS == cols) & (rows < 3 * SSD_HEADS)).astype(BF16)


def _ssd_chunk_kernel(z_ref, xbc_ref, dt_ref, cw_ref, cb_ref, dtb_ref, alog_ref, dskip_ref, ng_ref, sel_ref,
                      y_ref, conv_ref, state_ref, xp_ref, act_ref, acol_ref, st_ref, xbd_ref, sbd_ref):
    c = pl.program_id(1)
    t = SSD_CHUNK
    gw = SSD_GROUP_W
    hpg = SSD_HEADS_PER_GROUP
    hd = SSD_HEAD_DIM

    @pl.when(c == 0)
    def _():
        xp_ref[:, 0:SUBLANES, :] = jnp.zeros((xp_ref.shape[0], SUBLANES, LANES), F32)
        st_ref[...] = jnp.zeros_like(st_ref)
        xbd_ref[...] = jnp.zeros_like(xbd_ref)
        sbd_ref[...] = jnp.zeros_like(sbd_ref)

    for j in range(SSD_CONV_DIM // LANES):
        cols = slice(j * LANES, (j + 1) * LANES)
        xp_ref[j, SUBLANES:SUBLANES + t, :] = xbc_ref[:, cols]
        acc = xp_ref[j, SUBLANES - 3:SUBLANES - 3 + t, :] * cw_ref[0:1, cols]
        for k in range(1, CONV_W):
            acc = acc + xp_ref[j, SUBLANES - 3 + k:SUBLANES - 3 + k + t, :] * cw_ref[k:k + 1, cols]
        act_ref[:, cols] = _silu(acc + cb_ref[:, cols])
        xp_ref[j, 0:SUBLANES, :] = xbc_ref[t - SUBLANES:t, cols]
    conv_ref[0] = xbc_ref[t - SUBLANES:t, :]

    dtv = _softplus(dt_ref[...] + dtb_ref[...])
    da = dtv * (-jnp.exp(alog_ref[...]))
    row = lax.broadcasted_iota(jnp.int32, (t, t), 0)
    col = lax.broadcasted_iota(jnp.int32, (t, t), 1)
    causal = row >= col
    tril = causal.astype(BF16)
    d_hi, d_mid, d_lo = _split3(da)
    a2 = (_dot(tril, d_hi) + (_dot(tril, d_mid) + _dot(tril, d_lo))) * LOG2E
    arow_t = (a2 - jnp.log(dtv) * LOG2E).T

    lane = lax.broadcasted_iota(jnp.int32, (t, LANES), 1)
    a_hi, a_mid, a_lo = _split3(jnp.where(lane < SSD_HEADS, a2, 0.0))
    packed = (a_hi.astype(F32) + pltpu.roll(a_mid.astype(F32), SSD_HEADS, 1)
              + pltpu.roll(a_lo.astype(F32), 2 * SSD_HEADS, 1)).astype(BF16)
    acol_ref[...] = _dot(packed, sel_ref[...])

    lane_g = lax.broadcasted_iota(jnp.int32, (1, gw), 1) // hd
    for g in range(SSD_GROUPS):
        xs_g = act_ref[:, g * gw:(g + 1) * gw]
        b_g = act_ref[:, SSD_D_INNER + g * SSD_STATE:SSD_D_INNER + (g + 1) * SSD_STATE]
        c_g = act_ref[:, SSD_D_INNER + SSD_BC_DIM + g * SSD_STATE:
                      SSD_D_INNER + SSD_BC_DIM + (g + 1) * SSD_STATE]
        cb = jnp.where(causal, _dot_nt(c_g.astype(BF16), b_g.astype(BF16)), 0.0)
        b_gt = b_g.T
        st_g = st_ref[g]
        xs_b = xs_g.astype(BF16)
        for r in range(hpg):
            xbd_ref[g, r * t:(r + 1) * t, r * hd:(r + 1) * hd] = xs_b[:, r * hd:(r + 1) * hd]
        m_parts, d_parts, w_parts = [], [], []
        dec_g = jnp.zeros((1, gw), F32)
        for r in range(hpg):
            h = g * hpg + r
            acol = acol_ref[:, h * LANES:(h + 1) * LANES]
            arow = arow_t[h:h + 1, :]
            aend = acol[t - 1:t, :]
            m_parts.append((cb * jnp.exp2(jnp.minimum(acol - arow, EXP2_CAP))).astype(BF16))
            d_parts.append((c_g * jnp.exp2(acol)).astype(BF16))
            w_parts.append((b_gt * jnp.exp2(aend - arow)).astype(BF16))
            dec_g = jnp.where(lane_g == r, jnp.exp2(jnp.concatenate([aend, aend], axis=1)), dec_g)
        x_bd = xbd_ref[g]
        y_g = (_dot(jnp.concatenate(m_parts, axis=1), x_bd) + _dot(jnp.concatenate(d_parts, axis=1), sbd_ref[g])
               + xs_g * dskip_ref[:, g * gw:(g + 1) * gw])
        st_new = st_g * dec_g + _dot(jnp.concatenate(w_parts, axis=1), x_bd)
        st_ref[g] = st_new
        st_b = st_new.astype(BF16)
        for r in range(hpg):
            sbd_ref[g, r * SSD_STATE:(r + 1) * SSD_STATE, r * hd:(r + 1) * hd] = st_b[:, r * hd:(r + 1) * hd]
        yz = y_g * _silu(z_ref[:, g * gw:(g + 1) * gw])
        y_ref[:, g * gw:(g + 1) * gw] = _rms(yz, ng_ref[:, g * gw:(g + 1) * gw]).astype(y_ref.dtype)

    @pl.when(c == pl.num_programs(1) - 1)
    def _():
        for g in range(SSD_GROUPS):
            state_ref[0, g * hpg:(g + 1) * hpg] = st_ref[g].T.reshape(hpg, SSD_HEAD_DIM, SSD_STATE)


def ssd_prompt(z, xbc, dt, conv_w, conv_b, dt_bias, a_log, d_skip, norm_g, batch, seq):
    t = SSD_CHUNK
    nc = seq // t
    pad = LANES - SSD_HEADS
    rowblk = lambda b, c: (b * nc + c, 0)
    fixed = lambda b, c: (0, 0)
    y, conv_tail, state = pl.pallas_call(
        _ssd_chunk_kernel,
        grid=(batch, nc),
        in_specs=[pl.BlockSpec((t, SSD_D_INNER), rowblk),
                  pl.BlockSpec((t, SSD_CONV_DIM), rowblk),
                  pl.BlockSpec((t, LANES), rowblk),
                  pl.BlockSpec((CONV_W, SSD_CONV_DIM), fixed),
                  pl.BlockSpec((1, SSD_CONV_DIM), fixed),
                  pl.BlockSpec((1, LANES), fixed),
                  pl.BlockSpec((1, LANES), fixed),
                  pl.BlockSpec((1, SSD_D_INNER), fixed),
                  pl.BlockSpec((1, SSD_D_INNER), fixed),
                  pl.BlockSpec((LANES, SSD_HEADS * LANES), fixed)],
        out_specs=[pl.BlockSpec((t, SSD_D_INNER), rowblk),
                   pl.BlockSpec((1, SUBLANES, SSD_CONV_DIM), lambda b, c: (b, 0, 0)),
                   pl.BlockSpec((1, SSD_HEADS, SSD_HEAD_DIM, SSD_STATE), lambda b, c: (b, 0, 0, 0))],
        out_shape=[jax.ShapeDtypeStruct((batch * seq, SSD_D_INNER), BF16),
                   jax.ShapeDtypeStruct((batch, SUBLANES, SSD_CONV_DIM), F32),
                   jax.ShapeDtypeStruct((batch, SSD_HEADS, SSD_HEAD_DIM, SSD_STATE), F32)],
        scratch_shapes=[pltpu.VMEM((SSD_CONV_DIM // LANES, SUBLANES + t, LANES), F32),
                        pltpu.VMEM((t, SSD_CONV_DIM), F32),
                        pltpu.VMEM((t, SSD_HEADS * LANES), F32),
                        pltpu.VMEM((SSD_GROUPS, SSD_STATE, SSD_GROUP_W), F32),
                        pltpu.VMEM((SSD_GROUPS, SSD_HEADS_PER_GROUP * t, SSD_GROUP_W), BF16),
                        pltpu.VMEM((SSD_GROUPS, SSD_HEADS_PER_GROUP * SSD_STATE, SSD_GROUP_W), BF16)],
        compiler_params=_cparams("parallel", "arbitrary"),
        name="ssd_prompt",
    )(z, xbc, dt, conv_w, conv_b.reshape(1, -1), jnp.pad(dt_bias, (0, pad)).reshape(1, LANES),
      jnp.pad(a_log, (0, pad)).reshape(1, LANES), jnp.repeat(d_skip, SSD_HEAD_DIM).reshape(1, -1),
      norm_g.reshape(1, -1), _ssd_head_select())
    return y, conv_tail[:, SUBLANES - (CONV_W - 1):, :], state


MOBA_SPAN = 2 * MOBA_BLOCK
MOBA_HEADS_PER_STEP = 4


def _col_reduce(x, pair_op, reduce_fn):
    n = x.shape[0]
    while n > 8 * SUBLANES and n % 2 == 0:
        n //= 2
        x = pair_op(x[:n], x[n:])
    return reduce_fn(x, axis=0, keepdims=True)


def _moba_prompt_kernel(q_ref, k_ref, v_ref, o_ref, kb_ref, vt_ref, sel_ref, s_ref, p_ref):
    qi = pl.program_id(2)
    blk = MOBA_BLOCK
    seq = k_ref.shape[0]
    nb = seq // blk
    nbp = sel_ref.shape[2]
    hd = MOBA_HEAD_DIM
    heads = q_ref.shape[1] // hd

    def prepare(hh):
        cols = slice(hh * hd, (hh + 1) * hd)
        means = []
        for j in range(nb):
            kj = k_ref[j * blk:(j + 1) * blk, cols]
            means.append(jnp.sum(kj, axis=0, keepdims=True) * (1.0 / blk))
            kb_ref[hh, j * blk:(j + 1) * blk, :] = kj.astype(BF16)
        for j in range(seq // LANES):
            vt_ref[hh, :, j * LANES:(j + 1) * LANES] = v_ref[j * LANES:(j + 1) * LANES, cols].T.astype(BF16)
        if nbp > nb:
            means.append(jnp.zeros((nbp - nb, hd), F32))
        kmean = jnp.concatenate(means, axis=0)
        sub = lax.broadcasted_iota(jnp.int32, (nbp, blk), 0)
        for i in range(nb):
            gm = jnp.where(sub < i, _dot3_nt(kmean, q_ref[i * blk:(i + 1) * blk, cols]), -jnp.inf)
            rank = jnp.zeros((nbp, blk), F32)
            for m in range(i):
                gm_m = gm[m:m + 1, :]
                beats = (gm_m > gm) | ((gm_m == gm) & (sub > m))
                rank = rank + jnp.where(beats, 1.0, 0.0)
            chosen = ((sub < i) & (rank < MOBA_TOPK)) | (sub == i)
            sel_ref[hh, i] = jnp.where(chosen, 1.0, 0.0)

    @pl.when(qi == 0)
    def _():
        for hh in range(heads):
            prepare(hh)

    key_minus_query = (lax.broadcasted_iota(jnp.int32, (blk, blk), 0)
                       - lax.broadcasted_iota(jnp.int32, (blk, blk), 1))

    span = min(MOBA_SPAN, seq)
    per_span = span // blk

    def attend_head(hh, n_keys):
        cols = slice(hh * hd, (hh + 1) * hd)
        q = q_ref[pl.ds(pl.multiple_of(qi * blk, blk), blk), cols]
        qb = (q * (hd ** -0.5)).astype(BF16)
        sel = sel_ref[hh, qi]
        m = None
        for j in range(n_keys // blk):
            rows = slice(j * blk, (j + 1) * blk)
            bias = jnp.where(sel[j:j + 1, :] > 0.0, 0.0, NEG_BIG)
            s = jnp.where(key_minus_query <= (qi - j) * blk, _dot_nt(kb_ref[hh, rows, :], qb), NEG_BIG) + bias
            s_ref[hh, rows, :] = s
            bm = _col_reduce(s, jnp.maximum, jnp.max)
            m = bm if m is None else jnp.maximum(m, bm)
        l = jnp.zeros((1, blk), F32)
        for j in range(n_keys // blk):
            rows = slice(j * blk, (j + 1) * blk)
            p = jnp.exp(s_ref[hh, rows, :] - m)
            l = l + _col_reduce(p, jnp.add, jnp.sum)
            p_ref[hh, rows, :] = p.astype(BF16)
        acc = _dot(vt_ref[hh, :, 0:n_keys], p_ref[hh, 0:n_keys, :])
        o_ref[:, cols] = (acc / l).T.astype(o_ref.dtype)

    def attend(n_keys):
        for hh in range(heads):
            attend_head(hh, n_keys)

    for sp in range(seq // span):
        in_span = (qi >= sp * per_span) & (qi < (sp + 1) * per_span)
        pl.when(in_span)(functools.partial(attend, (sp + 1) * span))


def moba_prompt(q, k, v, batch, seq):
    nb = seq // MOBA_BLOCK
    hd = MOBA_HEAD_DIM
    hps = MOBA_HEADS_PER_STEP
    wide = lambda b, h, i: (b, h)
    return pl.pallas_call(
        _moba_prompt_kernel,
        grid=(batch, MOBA_HEADS // hps, nb),
        in_specs=[pl.BlockSpec((seq, hps * hd), wide),
                  pl.BlockSpec((seq, hps * hd), wide),
                  pl.BlockSpec((seq, hps * hd), wide)],
        out_specs=pl.BlockSpec((MOBA_BLOCK, hps * hd), lambda b, h, i: (b * nb + i, h)),
        out_shape=jax.ShapeDtypeStruct((batch * seq, D_MODEL), BF16),
        scratch_shapes=[pltpu.VMEM((hps, seq, hd), BF16),
                        pltpu.VMEM((hps, hd, seq), BF16),
                        pltpu.VMEM((hps, nb, -(-nb // SUBLANES) * SUBLANES, MOBA_BLOCK), F32),
                        pltpu.VMEM((hps, seq, MOBA_BLOCK), F32),
                        pltpu.VMEM((hps, seq, MOBA_BLOCK), BF16)],
        compiler_params=_cparams("parallel", "parallel", "arbitrary"),
        name="moba_prompt",
    )(q, k, v)


def _lru_gates(xc, wa, ba, wx, bx, lam):
    xb = xc.astype(BF16)
    r = _sigmoid(_dot(xb, wa) + ba)
    i = _sigmoid(_dot(xb, wx) + bx)
    log_a = (-LRU_C) * r * _softplus(-lam)
    a = jnp.exp(log_a)
    u = jnp.sqrt(-jnp.tanh(log_a) * (a * a + 1.0)) * (i * xc)
    return a, u


def _lru_chunk_kernel(gate_ref, x_ref, cw_ref, cb_ref, wa_ref, ba_ref, wx_ref, bx_ref, lam_ref,
                      y_ref, conv_ref, hlast_ref, xp_ref, a_ref, u_ref, h_ref):
    c = pl.program_id(1)
    t = LRU_CHUNK
    w = LRU_BLOCK_W

    @pl.when(c == 0)
    def _():
        xp_ref[:, 0:SUBLANES, :] = jnp.zeros((LRU_BLOCKS, SUBLANES, w), F32)
        h_ref[...] = jnp.zeros_like(h_ref)

    for n in range(LRU_BLOCKS):
        cols = slice(n * w, (n + 1) * w)
        xp_ref[n, SUBLANES:SUBLANES + t, :] = x_ref[:, cols]
        acc = xp_ref[n, SUBLANES - 3:SUBLANES - 3 + t, :] * cw_ref[0:1, cols]
        for k in range(1, CONV_W):
            acc = acc + xp_ref[n, SUBLANES - 3 + k:SUBLANES - 3 + k + t, :] * cw_ref[k:k + 1, cols]
        xc = acc + cb_ref[:, cols]
        a, u = _lru_gates(xc, wa_ref[n], ba_ref[:, cols], wx_ref[n], bx_ref[:, cols], lam_ref[:, cols])
        a_ref[:, cols] = a
        u_ref[:, cols] = u
        xp_ref[n, 0:SUBLANES, :] = x_ref[t - SUBLANES:t, cols]
    conv_ref[0] = x_ref[t - SUBLANES:t, :]

    srow = lax.broadcasted_iota(jnp.int32, (SUBLANES, LRU_WIDTH), 0)

    def tile(i, h):
        rows = pl.ds(pl.multiple_of(i * SUBLANES, SUBLANES), SUBLANES)
        a = a_ref[rows, :]
        u = u_ref[rows, :]
        for d in (1, 2, 4):
            keep = srow >= d
            a_sh = jnp.where(keep, pltpu.roll(a, d, 0), 1.0)
            u_sh = jnp.where(keep, pltpu.roll(u, d, 0), 0.0)
            u = a * u_sh + u
            a = a * a_sh
        hs = a * h + u
        u_ref[rows, :] = hs
        return jnp.broadcast_to(hs[SUBLANES - 1:SUBLANES, :], (SUBLANES, LRU_WIDTH))

    h = lax.fori_loop(0, t // SUBLANES, tile, h_ref[...])
    h_ref[...] = h
    hlast_ref[0] = h
    y_ref[...] = (_gelu_tanh(gate_ref[...]) * u_ref[...]).astype(y_ref.dtype)


def lru_prompt(gate_br, x_br, conv_w, conv_b, w_a, b_a, w_x, b_x, lam, batch, seq):
    t = LRU_CHUNK
    nc = seq // t
    wd = LRU_WIDTH
    rowblk = lambda b, c: (b * nc + c, 0)
    fixed = lambda b, c: (0, 0)
    fixed3 = lambda b, c: (0, 0, 0)
    perb = lambda b, c: (b, 0, 0)
    y, conv_tail, h_last = pl.pallas_call(
        _lru_chunk_kernel,
        grid=(batch, nc),
        in_specs=[pl.BlockSpec((t, wd), rowblk),
                  pl.BlockSpec((t, wd), rowblk),
                  pl.BlockSpec((CONV_W, wd), fixed),
                  pl.BlockSpec((1, wd), fixed),
                  pl.BlockSpec((LRU_BLOCKS, LRU_BLOCK_W, LRU_BLOCK_W), fixed3),
                  pl.BlockSpec((1, wd), fixed),
                  pl.BlockSpec((LRU_BLOCKS, LRU_BLOCK_W, LRU_BLOCK_W), fixed3),
                  pl.BlockSpec((1, wd), fixed),
                  pl.BlockSpec((1, wd), fixed)],
        out_specs=[pl.BlockSpec((t, wd), rowblk),
                   pl.BlockSpec((1, SUBLANES, wd), perb),
                   pl.BlockSpec((1, SUBLANES, wd), perb)],
        out_shape=[jax.ShapeDtypeStruct((batch * seq, wd), BF16),
                   jax.ShapeDtypeStruct((batch, SUBLANES, wd), F32),
                   jax.ShapeDtypeStruct((batch, SUBLANES, wd), F32)],
        scratch_shapes=[pltpu.VMEM((LRU_BLOCKS, SUBLANES + t, LRU_BLOCK_W), F32),
                        pltpu.VMEM((t, wd), F32),
                        pltpu.VMEM((t, wd), F32),
                        pltpu.VMEM((SUBLANES, wd), F32)],
        compiler_params=_cparams("parallel", "arbitrary"),
        name="lru_prompt",
    )(gate_br, x_br, conv_w, conv_b.reshape(1, wd), w_a.astype(BF16), b_a.reshape(1, wd),
      w_x.astype(BF16), b_x.reshape(1, wd), lam.reshape(1, wd))
    return y, conv_tail[:, SUBLANES - (CONV_W - 1):, :], h_last[:, 0, :]


def _ssd_step_kernel(z_ref, xbc_ref, dt_ref, cs_ref, st_ref, cw_ref, cb_ref, dtb_ref, alog_ref, dskip_ref, ng_ref,
                     y_ref, cso_ref, sto_ref):
    gw = SSD_GROUP_W
    hpg = SSD_HEADS_PER_GROUP
    x = xbc_ref[0]
    buf = cs_ref[0]
    conv = x * cw_ref[CONV_W - 1:CONV_W, :]
    for k in range(CONV_W - 1):
        conv = conv + buf[k:k + 1, :] * cw_ref[k:k + 1, :]
    act = _silu(conv + cb_ref[...])
    cso_ref[0, 0:CONV_W - 2, :] = buf[1:CONV_W - 1, :]
    cso_ref[0, CONV_W - 2:CONV_W - 1, :] = x

    dtv = _softplus(dt_ref[0] + dtb_ref[...])
    dec = jnp.exp(dtv * (-jnp.exp(alog_ref[...])))
    lane_g = lax.broadcasted_iota(jnp.int32, (1, gw), 1) // SSD_HEAD_DIM
    first_row = lax.broadcasted_iota(jnp.int32, (SUBLANES, gw), 0) == 0
    ys = []
    for g in range(SSD_GROUPS):
        xs_g = act[:, g * gw:(g + 1) * gw]
        b_g = act[:, SSD_D_INNER + g * SSD_STATE:SSD_D_INNER + (g + 1) * SSD_STATE]
        c_g = act[:, SSD_D_INNER + SSD_BC_DIM + g * SSD_STATE:SSD_D_INNER + SSD_BC_DIM + (g + 1) * SSD_STATE]
        dt_e = jnp.zeros((1, gw), F32)
        dec_rows = []
        for r in range(hpg):
            h = g * hpg + r
            dt_e = jnp.where(lane_g == r, dtv[:, h:h + 1], dt_e)
            dec_rows.append(jnp.broadcast_to(dec[:, h:h + 1], (SSD_HEAD_DIM, SSD_STATE)))
        xdt = xs_g * dt_e
        x8 = jnp.where(first_row, jnp.broadcast_to(xdt, (SUBLANES, gw)), 0.0).astype(BF16)
        b8 = jnp.broadcast_to(b_g, (SUBLANES, SSD_STATE)).astype(BF16)
        outer = lax.dot_general(x8, b8, (((0,), (0,)), ((), ())), preferred_element_type=F32)
        old = st_ref[0, g * hpg:(g + 1) * hpg].reshape(gw, SSD_STATE)
        new = old * jnp.concatenate(dec_rows, axis=0) + outer
        sto_ref[0, g * hpg:(g + 1) * hpg] = new.reshape(hpg, SSD_HEAD_DIM, SSD_STATE)
        c8 = jnp.broadcast_to(c_g, (SUBLANES, SSD_STATE)).astype(BF16)
        y_g = _dot_nt(c8, new.astype(BF16))[0:1, :] + xs_g * dskip_ref[:, g * gw:(g + 1) * gw]
        yz = y_g * _silu(z_ref[0][:, g * gw:(g + 1) * gw])
        ys.append(_rms(yz, ng_ref[:, g * gw:(g + 1) * gw]))
    y_ref[0] = jnp.concatenate(ys, axis=1)


def ssd_step(z, xbc, dt, conv_state, ssm_state, layer, conv_w, conv_b, dt_bias, a_log, d_skip, norm_g):
    nb = z.shape[0]
    pad = LANES - SSD_HEADS
    per3 = lambda b: (b, 0, 0)
    fixed = lambda b: (0, 0)
    y, conv_new, state_new = pl.pallas_call(
        _ssd_step_kernel,
        grid=(nb,),
        in_specs=[pl.BlockSpec((1, 1, SSD_D_INNER), per3),
                  pl.BlockSpec((1, 1, SSD_CONV_DIM), per3),
                  pl.BlockSpec((1, 1, LANES), per3),
                  pl.BlockSpec((None, 1, CONV_W - 1, SSD_CONV_DIM), lambda b: (layer, b, 0, 0)),
                  pl.BlockSpec((None, 1, SSD_HEADS, SSD_HEAD_DIM, SSD_STATE), lambda b: (layer, b, 0, 0, 0)),
                  pl.BlockSpec((CONV_W, SSD_CONV_DIM), fixed),
                  pl.BlockSpec((1, SSD_CONV_DIM), fixed),
                  pl.BlockSpec((1, LANES), fixed),
                  pl.BlockSpec((1, LANES), fixed),
                  pl.BlockSpec((1, SSD_D_INNER), fixed),
                  pl.BlockSpec((1, SSD_D_INNER), fixed)],
        out_specs=[pl.BlockSpec((1, 1, SSD_D_INNER), per3),
                   pl.BlockSpec((1, CONV_W - 1, SSD_CONV_DIM), per3),
                   pl.BlockSpec((1, SSD_HEADS, SSD_HEAD_DIM, SSD_STATE), lambda b: (b, 0, 0, 0))],
        out_shape=[jax.ShapeDtypeStruct((nb, 1, SSD_D_INNER), F32),
                   jax.ShapeDtypeStruct((nb, CONV_W - 1, SSD_CONV_DIM), F32),
                   jax.ShapeDtypeStruct((nb, SSD_HEADS, SSD_HEAD_DIM, SSD_STATE), F32)],
        compiler_params=_cparams("parallel"),
        name="ssd_step",
    )(z.reshape(nb, 1, -1), xbc.reshape(nb, 1, -1), dt.reshape(nb, 1, -1), conv_state, ssm_state,
      conv_w, conv_b.reshape(1, -1), jnp.pad(dt_bias, (0, pad)).reshape(1, LANES),
      jnp.pad(a_log, (0, pad)).reshape(1, LANES), jnp.repeat(d_skip, SSD_HEAD_DIM).reshape(1, -1),
      norm_g.reshape(1, -1))
    return y.reshape(nb, SSD_D_INNER), conv_new, state_new


def _lru_step_kernel(gate_ref, x_ref, cs_ref, h0_ref, cw_ref, cb_ref, wa_ref, ba_ref, wx_ref, bx_ref, lam_ref,
                     y_ref, cso_ref, ho_ref):
    w = LRU_BLOCK_W
    x = x_ref[...]
    xc = x * cw_ref[CONV_W - 1:CONV_W, :] + cb_ref[...]
    for k in range(CONV_W - 1):
        xc = xc + cs_ref[k] * cw_ref[k:k + 1, :]
    for k in range(CONV_W - 2):
        cso_ref[k] = cs_ref[k + 1]
    cso_ref[CONV_W - 2] = x
    for n in range(LRU_BLOCKS):
        cols = slice(n * w, (n + 1) * w)
        a, u = _lru_gates(xc[:, cols], wa_ref[n], ba_ref[:, cols], wx_ref[n], bx_ref[:, cols], lam_ref[:, cols])
        h = a * h0_ref[:, cols] + u
        ho_ref[:, cols] = h
        y_ref[:, cols] = _gelu_tanh(gate_ref[:, cols]) * h


def lru_step(gate_br, x_br, conv_state, h0, conv_w, conv_b, w_a, b_a, w_x, b_x, lam):
    nb, wd = x_br.shape
    return pl.pallas_call(
        _lru_step_kernel,
        out_shape=[jax.ShapeDtypeStruct((nb, wd), F32),
                   jax.ShapeDtypeStruct((CONV_W - 1, nb, wd), F32),
                   jax.ShapeDtypeStruct((nb, wd), F32)],
        compiler_params=pltpu.CompilerParams(vmem_limit_bytes=VMEM_LIMIT),
        name="lru_step",
    )(gate_br, x_br, conv_state, h0, conv_w, conv_b.reshape(1, wd), w_a.astype(BF16), b_a.reshape(1, wd),
      w_x.astype(BF16), b_x.reshape(1, wd), lam.reshape(1, wd))


MOBA_SUM_PAGES = 16


def _moba_block_sums_kernel(tbl_ref, *refs):
    del tbl_ref
    pages, o_ref = refs[:-1], refs[-1]
    for k in range(len(pages) // PAGES_PER_BLOCK):
        s = jnp.sum(pages[PAGES_PER_BLOCK * k][...], axis=0)
        for j in range(1, PAGES_PER_BLOCK):
            s = s + jnp.sum(pages[PAGES_PER_BLOCK * k + j][...], axis=0)
        o_ref[0, k] = s


def moba_block_sums(pool, layer, table):
    nb, n_pages = table.shape
    steps = n_pages // MOBA_SUM_PAGES
    blocks_per_step = MOBA_SUM_PAGES // PAGES_PER_BLOCK
    page_shape = pool.shape[2:]

    def page_spec(k):
        return pl.BlockSpec((None, None) + page_shape,
                            lambda b, c, tbl: (layer, tbl[b * n_pages + c * MOBA_SUM_PAGES + k], 0, 0, 0))

    return pl.pallas_call(
        _moba_block_sums_kernel,
        grid_spec=pltpu.PrefetchScalarGridSpec(
            num_scalar_prefetch=1,
            grid=(nb, steps),
            in_specs=[page_spec(k) for k in range(MOBA_SUM_PAGES)],
            out_specs=pl.BlockSpec((1, blocks_per_step) + page_shape[1:], lambda b, c, tbl: (b, c, 0, 0))),
        out_shape=jax.ShapeDtypeStruct((nb, n_pages // PAGES_PER_BLOCK) + page_shape[1:], F32),
        compiler_params=_cparams("parallel", "arbitrary"),
        name="moba_block_sums",
    )(table.reshape(-1), *([pool] * MOBA_SUM_PAGES))


def _moba_gate_kernel(q_ref, bs_ref, sel_ref):
    n_blocks = bs_ref.shape[1]
    q = q_ref[0]
    gates = [jnp.sum((bs_ref[0, n] * (1.0 / MOBA_BLOCK)) * q, axis=-1, keepdims=True) for n in range(n_blocks)]
    lane = lax.broadcasted_iota(jnp.int32, (MOBA_HEADS, LANES), 1)
    out = jnp.zeros((MOBA_HEADS, LANES), F32)
    for k in range(MOBA_TOPK):
        best = functools.reduce(jnp.maximum, gates)
        idx = jnp.full((MOBA_HEADS, 1), float(n_blocks), F32)
        for n in reversed(range(n_blocks)):
            idx = jnp.where(gates[n] == best, float(n), idx)
        out = jnp.where(lane == k, idx, out)
        gates = [jnp.where(idx == float(n), -jnp.inf, gates[n]) for n in range(n_blocks)]
    sel_ref[0] = out.astype(jnp.int32)


def moba_gate(q, block_sums):
    nb = q.shape[0]
    n_blocks = block_sums.shape[1]
    return pl.pallas_call(
        _moba_gate_kernel,
        grid=(nb,),
        in_specs=[pl.BlockSpec((1, MOBA_HEADS, MOBA_HEAD_DIM), lambda b: (b, 0, 0)),
                  pl.BlockSpec((1, n_blocks, MOBA_HEADS, MOBA_HEAD_DIM), lambda b: (b, 0, 0, 0))],
        out_specs=pl.BlockSpec((1, MOBA_HEADS, LANES), lambda b: (b, 0, 0)),
        out_shape=jax.ShapeDtypeStruct((nb, MOBA_HEADS, LANES), jnp.int32),
        compiler_params=_cparams("parallel"),
        name="moba_gate",
    )(q.reshape(nb, MOBA_HEADS, MOBA_HEAD_DIM), block_sums)


def _moba_decode_kernel(tbl_ref, sel_ref, q_ref, kn_ref, vn_ref, kpool_ref, vpool_ref, o_ref, kbuf, vbuf, sem,
                        *, layer, n_table_pages):
    b, h = pl.program_id(0), pl.program_id(1)
    nh = pl.num_programs(1)
    step = b * nh + h
    slot = lax.rem(step, 2)
    n_sel = MOBA_TOPK * PAGES_PER_BLOCK

    def page_copies(bb, hh, sl):
        out = []
        for k in range(MOBA_TOPK):
            blk = sel_ref[(bb * nh + hh) * MOBA_TOPK + k]
            for s in range(PAGES_PER_BLOCK):
                page = tbl_ref[bb * n_table_pages + blk * PAGES_PER_BLOCK + s]
                i = k * PAGES_PER_BLOCK + s
                out.append(pltpu.make_async_copy(kpool_ref.at[layer, page, :, hh, :], kbuf.at[sl, i], sem.at[sl, 0, i]))
                out.append(pltpu.make_async_copy(vpool_ref.at[layer, page, :, hh, :], vbuf.at[sl, i], sem.at[sl, 1, i]))
        return out

    def start_all(copies):
        for i, c in enumerate(copies):
            c.start(priority=i % 2)

    @pl.when(step == 0)
    def _():
        start_all(page_copies(b, h, 0))

    @pl.when(step + 1 < pl.num_programs(0) * nh)
    def _():
        wrap = h + 1 == nh
        start_all(page_copies(jnp.where(wrap, b + 1, b), jnp.where(wrap, 0, h + 1), 1 - slot))

    for c in page_copies(b, h, slot):
        c.wait()

    scale = MOBA_HEAD_DIM ** -0.5
    q = q_ref[0]
    q8 = jnp.broadcast_to(q, (SUBLANES, MOBA_HEAD_DIM)).astype(BF16)
    k_cat = kbuf[slot].reshape(n_sel * PAGE_SIZE, MOBA_HEAD_DIM).astype(BF16)
    v_cat = vbuf[slot].reshape(n_sel * PAGE_SIZE, MOBA_HEAD_DIM).astype(BF16)
    s = _dot_nt(q8, k_cat) * scale
    s_new = jnp.sum(q * kn_ref[0], axis=1, keepdims=True) * scale
    m = jnp.maximum(jnp.max(s, axis=1, keepdims=True), s_new)
    p = jnp.exp(s - m)
    p_new = jnp.exp(s_new - m)
    denom = jnp.sum(p, axis=1, keepdims=True) + p_new
    o = (_dot(p.astype(BF16), v_cat) + p_new * vn_ref[0]) / denom
    o_ref[0] = o[0:1, :]


def moba_decode(q, k_new, v_new, k_pool, v_pool, layer, table, sel):
    nb = q.shape[0]
    n_pages = table.shape[1]
    hd = MOBA_HEAD_DIM
    n_sel = MOBA_TOPK * PAGES_PER_BLOCK
    tok = pl.BlockSpec((1, 1, hd), lambda b, h, tbl, sl: (b, 0, h))
    hbm = pl.BlockSpec(memory_space=pl.ANY)
    out = pl.pallas_call(
        functools.partial(_moba_decode_kernel, layer=layer, n_table_pages=n_pages),
        grid_spec=pltpu.PrefetchScalarGridSpec(
            num_scalar_prefetch=2,
            grid=(nb, MOBA_HEADS),
            in_specs=[tok, tok, tok, hbm, hbm],
            out_specs=tok,
            scratch_shapes=[pltpu.VMEM((2, n_sel, PAGE_SIZE, hd), F32),
                            pltpu.VMEM((2, n_sel, PAGE_SIZE, hd), F32),
                            pltpu.SemaphoreType.DMA((2, 2, n_sel))]),
        out_shape=jax.ShapeDtypeStruct((nb, 1, D_MODEL), F32),
        compiler_params=_cparams("arbitrary", "arbitrary"),
        name="moba_decode",
    )(table.reshape(-1), sel[:, :, :MOBA_TOPK].reshape(-1),
      q.reshape(nb, 1, D_MODEL), k_new.reshape(nb, 1, D_MODEL), v_new.reshape(nb, 1, D_MODEL), k_pool, v_pool)
    return out.reshape(nb, D_MODEL)


def _prep_weights(W):
    bf = lambda a: a.astype(BF16)
    w_in, w_qkv, w_lru = lax.optimization_barrier((bf(W["ssd_w_in"]), bf(W["moba_w_qkv"]), bf(W["lru_w_in"])))
    zx = SSD_D_INNER + SSD_CONV_DIM
    return dict(
        ssd_w_z=w_in[:, :, :SSD_D_INNER],
        ssd_w_xbc=w_in[:, :, SSD_D_INNER:zx],
        ssd_w_dt=jnp.pad(w_in[:, :, zx:], ((0, 0), (0, 0), (0, LANES - SSD_HEADS))),
        ssd_w_out=bf(W["ssd_w_out"]),
        moba_w_q=w_qkv[:, :, :D_MODEL],
        moba_w_k=w_qkv[:, :, D_MODEL:2 * D_MODEL],
        moba_w_v=w_qkv[:, :, 2 * D_MODEL:],
        moba_w_o=bf(W["moba_w_o"]),
        lru_w_gate=w_lru[:, :, :LRU_WIDTH],
        lru_w_x=w_lru[:, :, LRU_WIDTH:],
        lru_w_out=bf(W["lru_w_out"]),
        mlp_w_up=bf(W["mlp_w_up"]), mlp_w_down=bf(W["mlp_w_down"]),
        ple_w_proj=bf(W["ple_w_proj"]), ple_w_gate=bf(W["ple_w_gate"]),
    )


def _trunk(h, p, W, Wb, tiles, ssd_fn, moba_fn, lru_fn):
    tm, tf = tiles
    ssd_out, moba_out, lru_out = [], [], []
    xn = rmsnorm_bf16(h, W["norm_mix"][0], tm=tm)
    for i in range(DEPTH):
        j, kind = i // N_MIXERS, i % N_MIXERS
        proj = lambda *names: multi_linear(xn, [Wb[n] for n in names], j, tm=tm)
        if kind == 0:
            y, conv, state = ssd_fn(j, *proj("ssd_w_z", "ssd_w_xbc", "ssd_w_dt"))
            ssd_out.append((conv, state))
            w_out = Wb["ssd_w_out"]
        elif kind == 1:
            q, k, v = proj("moba_w_q", "moba_w_k", "moba_w_v")
            moba_out.append((k, v))
            y, w_out = moba_fn(j, q, k, v), Wb["moba_w_o"]
        else:
            y, conv, state = lru_fn(j, *proj("lru_w_gate", "lru_w_x"))
            lru_out.append((conv, state))
            w_out = Wb["lru_w_out"]
        final = i == DEPTH - 1
        g_next = W["norm_final"] if final else W["norm_mix"][i + 1]
        out = layer_tail(h, y, w_out, j, W["norm_mlp"][i], Wb["mlp_w_up"], Wb["mlp_w_down"], p,
                         W["norm_ple"][i], Wb["ple_w_gate"], Wb["ple_w_proj"], i, g_next,
                         tm=tm, tf=tf, final=final)
        h, xn = (out, None) if final else out
    stack = lambda pairs, n: jnp.stack([pr[n] for pr in pairs])
    return (h, stack(ssd_out, 0), stack(ssd_out, 1), stack(moba_out, 0), stack(moba_out, 1),
            stack(lru_out, 0), stack(lru_out, 1))


PROMPT_TILES = (512, 1024)
SAMPLE_TILES = (32, 1024)


def kernel(x_prompt, x_sample, state_ssd_conv, state_ssd, cache_k, cache_v, page_table, state_lru_conv,
           state_lru, p_prompt, p_sample, norm_mix, norm_mlp, norm_ple, norm_final, ssd_w_in, ssd_conv_w,
           ssd_conv_b, ssd_dt_bias, ssd_a_log, ssd_d, ssd_norm, ssd_w_out, moba_w_qkv, moba_w_o, lru_w_in,
           lru_conv_w, lru_conv_b, lru_w_a, lru_b_a, lru_w_x, lru_b_x, lru_lambda, lru_w_out, mlp_w_up,
           mlp_w_down, ple_w_proj, ple_w_gate):
    W = dict(norm_mix=norm_mix, norm_mlp=norm_mlp, norm_ple=norm_ple, norm_final=norm_final,
             ssd_w_in=ssd_w_in, ssd_w_out=ssd_w_out, moba_w_qkv=moba_w_qkv, moba_w_o=moba_w_o,
             lru_w_in=lru_w_in, lru_w_out=lru_w_out, mlp_w_up=mlp_w_up, mlp_w_down=mlp_w_down,
             ple_w_proj=ple_w_proj, ple_w_gate=ple_w_gate)
    Wb = _prep_weights(W)
    batch, seq, d = x_prompt.shape
    dec_batch, dec_seq, _ = x_sample.shape
    assert dec_seq == 1 and seq % MOBA_BLOCK == 0 and seq % LRU_CHUNK == 0

    ssd_args = lambda j: (ssd_conv_w[j], ssd_conv_b[j], ssd_dt_bias[j], ssd_a_log[j], ssd_d[j], ssd_norm[j])
    lru_args = lambda j: (lru_conv_w[j], lru_conv_b[j], lru_w_a[j], lru_b_a[j].reshape(-1), lru_w_x[j],
                          lru_b_x[j].reshape(-1), lru_lambda[j])

    out_p = _trunk(
        x_prompt.reshape(batch * seq, d), p_prompt.reshape(DEPTH, batch * seq, PLE_DIM), W, Wb, PROMPT_TILES,
        lambda j, z, xbc, dt: ssd_prompt(z, xbc, dt, *ssd_args(j), batch, seq),
        lambda j, q, k, v: moba_prompt(q, k, v, batch, seq),
        lambda j, gate, x: lru_prompt(gate, x, *lru_args(j), batch, seq))

    def moba_sample(j, q, k, v):
        sel = moba_gate(q, moba_block_sums(cache_k, j, page_table))
        return moba_decode(q, k, v, cache_k, cache_v, j, page_table, sel)

    def lru_sample(j, gate, x):
        y, conv, h = lru_step(gate, x, jnp.swapaxes(state_lru_conv[j], 0, 1), state_lru[j], *lru_args(j))
        return y, jnp.swapaxes(conv, 0, 1), h

    out_s = _trunk(
        x_sample.reshape(dec_batch, d), p_sample.reshape(DEPTH, dec_batch, PLE_DIM), W, Wb, SAMPLE_TILES,
        lambda j, z, xbc, dt: ssd_step(z, xbc, dt, state_ssd_conv, state_ssd, j, *ssd_args(j)),
        moba_sample, lru_sample)

    def shaped(out, b, s):
        y, ssd_conv, ssd_state, k, v, lru_conv, lru_state = out
        kv_shape = (-1, b, s, MOBA_HEADS, MOBA_HEAD_DIM)
        return (y.reshape(b, s, d), ssd_conv, ssd_state, k.reshape(kv_shape), v.reshape(kv_shape),
                lru_conv, lru_state)

    yp, *rest_p = shaped(out_p, batch, seq)
    ys, *rest_s = shaped(out_s, dec_batch, dec_seq)
    return (yp, ys, *rest_p, *rest_s)
```

```python
import functools
import math

import jax
import jax.numpy as jnp
from jax import lax
from jax.experimental import pallas as pl
from jax.experimental.pallas import tpu as pltpu

D_MODEL = 1024
DEPTH = 4
N_MIXERS = 3
PLE_DIM = 256
D_FF = 4 * D_MODEL
NORM_EPS = 1e-6
CONV_W = 4
PAGE_SIZE = 128

SSD_D_INNER = 2 * D_MODEL
SSD_HEAD_DIM = 64
SSD_HEADS = SSD_D_INNER // SSD_HEAD_DIM
SSD_GROUPS = 8
SSD_HEADS_PER_GROUP = SSD_HEADS // SSD_GROUPS
SSD_STATE = 128
SSD_BC_DIM = SSD_GROUPS * SSD_STATE
SSD_CONV_DIM = SSD_D_INNER + 2 * SSD_BC_DIM
SSD_GROUP_W = SSD_D_INNER // SSD_GROUPS
SSD_CHUNK = 128

MOBA_HEADS = 8
MOBA_HEAD_DIM = D_MODEL // MOBA_HEADS
MOBA_BLOCK = 256
MOBA_TOPK = 3
PAGES_PER_BLOCK = MOBA_BLOCK // PAGE_SIZE

LRU_WIDTH = D_MODEL
LRU_BLOCKS = 8
LRU_BLOCK_W = LRU_WIDTH // LRU_BLOCKS
LRU_C = 8.0
LRU_CHUNK = 256

LANES = 128
SUBLANES = 8
VMEM_LIMIT = 56 * 1024 * 1024
NEG_BIG = -1e30
LOG2E = 1.0 / math.log(2.0)
EXP2_CAP = 126.0

F32 = jnp.float32
BF16 = jnp.bfloat16


def _cparams(*sem):
    return pltpu.CompilerParams(dimension_semantics=sem, vmem_limit_bytes=VMEM_LIMIT)


def _rms(x, g):
    var = jnp.mean(x * x, axis=-1, keepdims=True)
    return (x * lax.rsqrt(var + NORM_EPS)) * g


def _sigmoid(x):
    return 0.5 + 0.5 * jnp.tanh(0.5 * x)


def _silu(x):
    hx = 0.5 * x
    return hx + hx * jnp.tanh(hx)


def _softplus(x):
    return jnp.maximum(x, 0.0) + jnp.log1p(jnp.exp(-jnp.abs(x)))


def _gelu_tanh(x):
    return 0.5 * x * (1.0 + jnp.tanh(math.sqrt(2.0 / math.pi) * (x + 0.044715 * (x * x * x))))


def _split3(x):
    hi = x.astype(BF16)
    r1 = x - hi.astype(F32)
    mid = r1.astype(BF16)
    lo = (r1 - mid.astype(F32)).astype(BF16)
    return hi, mid, lo


def _dot(a, b):
    return jnp.dot(a, b, preferred_element_type=F32)


def _dot_nt(a, b):
    return lax.dot_general(a, b, (((1,), (1,)), ((), ())), preferred_element_type=F32)


def _dot3_nt(a, b):
    ah, am, _ = _split3(a)
    bh, bm, _ = _split3(b)
    return _dot_nt(ah, bh) + (_dot_nt(ah, bm) + _dot_nt(am, bh))


def _rmsnorm_kernel(x_ref, g_ref, o_ref):
    o_ref[...] = _rms(x_ref[...], g_ref[...]).astype(o_ref.dtype)


def rmsnorm_bf16(x, g, *, tm):
    m, d = x.shape
    return pl.pallas_call(
        _rmsnorm_kernel,
        grid=(m // tm,),
        in_specs=[pl.BlockSpec((tm, d), lambda i: (i, 0)), pl.BlockSpec((1, d), lambda i: (0, 0))],
        out_specs=pl.BlockSpec((tm, d), lambda i: (i, 0)),
        out_shape=jax.ShapeDtypeStruct((m, d), BF16),
        compiler_params=_cparams("parallel"),
        name="rmsnorm",
    )(x, g.reshape(1, d))


def _multi_linear_kernel(x_ref, *refs):
    n = len(refs) // 2
    x = x_ref[...]
    for w_ref, o_ref in zip(refs[:n], refs[n:]):
        o_ref[...] = _dot(x, w_ref[...])


def multi_linear(x, weights, layer, *, tm):
    m, d = x.shape
    return pl.pallas_call(
        _multi_linear_kernel,
        grid=(m // tm,),
        in_specs=[pl.BlockSpec((tm, d), lambda i: (i, 0))]
                 + [pl.BlockSpec((None,) + w.shape[1:], lambda i: (layer, 0, 0), pipeline_mode=pl.Buffered(1))
                    for w in weights],
        out_specs=[pl.BlockSpec((tm, w.shape[2]), lambda i: (i, 0)) for w in weights],
        out_shape=[jax.ShapeDtypeStruct((m, w.shape[2]), F32) for w in weights],
        compiler_params=_cparams("parallel"),
        name="multi_linear",
    )(x, *weights)


LAYER_TAIL_CHAINS = 2


def _layer_tail_kernel(h_ref, y_ref, wo_ref, gm_ref, wu_ref, wd_ref, p_ref, gp_ref, wg_ref, wp_ref, gn_ref,
                       *outs, final, chains, tf):
    rows_per = h_ref.shape[0] // chains
    ff = wu_ref.shape[1]
    for c in range(chains):
        rows = slice(c * rows_per, (c + 1) * rows_per)
        h1 = h_ref[rows, :] + _dot(y_ref[rows, :].astype(BF16), wo_ref[...])
        xn = _rms(h1, gm_ref[...]).astype(BF16)
        acc = None
        for f in range(ff // tf):
            hid = jnp.square(jnp.maximum(_dot(xn, wu_ref[:, f * tf:(f + 1) * tf]), 0.0))
            part = _dot(hid.astype(BF16), wd_ref[f * tf:(f + 1) * tf, :])
            acc = part if acc is None else acc + part
        h2 = h1 + acc
        gate = _sigmoid(_dot(_rms(h2, gp_ref[...]).astype(BF16), wg_ref[...]))
        h3 = h2 + _dot(p_ref[rows, :].astype(BF16), wp_ref[...]) * gate
        if final:
            outs[0][rows, :] = _rms(h3, gn_ref[...])
        else:
            outs[0][rows, :] = h3
            outs[1][rows, :] = _rms(h3, gn_ref[...]).astype(BF16)


def layer_tail(h, y, w_out, mixer_idx, g_mlp, w_up, w_down, p, g_ple, w_gate, w_proj, layer, g_next, *, tm, tf, final):
    m, d = h.shape
    k = y.shape[1]
    ff = w_up.shape[2]
    pd = p.shape[2]
    row = lambda i: (i, 0)
    fixed = lambda i: (0, 0)
    once = pl.Buffered(1)
    chains = LAYER_TAIL_CHAINS if tm % (LAYER_TAIL_CHAINS * LANES) == 0 else 1
    out_specs = [pl.BlockSpec((tm, d), row)]
    out_shape = [jax.ShapeDtypeStruct((m, d), F32)]
    if not final:
        out_specs.append(pl.BlockSpec((tm, d), row))
        out_shape.append(jax.ShapeDtypeStruct((m, d), BF16))
    outs = pl.pallas_call(
        functools.partial(_layer_tail_kernel, final=final, chains=chains, tf=tf),
        grid=(m // tm,),
        in_specs=[pl.BlockSpec((tm, d), row),
                  pl.BlockSpec((tm, k), row),
                  pl.BlockSpec((None, k, d), lambda i: (mixer_idx, 0, 0), pipeline_mode=once),
                  pl.BlockSpec((1, d), fixed),
                  pl.BlockSpec((None, d, ff), lambda i: (layer, 0, 0), pipeline_mode=once),
                  pl.BlockSpec((None, ff, d), lambda i: (layer, 0, 0), pipeline_mode=once),
                  pl.BlockSpec((None, tm, pd), lambda i: (layer, i, 0)),
                  pl.BlockSpec((1, d), fixed),
                  pl.BlockSpec((None, d, d), lambda i: (layer, 0, 0), pipeline_mode=once),
                  pl.BlockSpec((None, pd, d), lambda i: (layer, 0, 0), pipeline_mode=once),
                  pl.BlockSpec((1, d), fixed)],
        out_specs=out_specs,
        out_shape=out_shape,
        compiler_params=_cparams("parallel"),
        name="layer_tail",
    )(h, y, w_out, g_mlp.reshape(1, d), w_up, w_down, p, g_ple.reshape(1, d), w_gate, w_proj, g_next.reshape(1, d))
    return outs[0] if final else tuple(outs)


def _ssd_head_select():
    rows = jnp.arange(LANES)[:, None]
    cols = jnp.arange(SSD_HEADS * LANES)[None, :] // LANES
    return ((rows % SSD_HEADS == cols) & (rows < 3 * SSD_HEADS)).astype(BF16)


def _ssd_chunk_kernel(z_ref, xbc_ref, dt_ref, cw_ref, cb_ref, dtb_ref, alog_ref, dskip_ref, ng_ref, sel_ref,
                      y_ref, conv_ref, state_ref, xp_ref, act_ref, acol_ref, st_ref, xbd_ref, sbd_ref):
    c = pl.program_id(1)
    t = SSD_CHUNK
    gw = SSD_GROUP_W
    hpg = SSD_HEADS_PER_GROUP
    hd = SSD_HEAD_DIM

    @pl.when(c == 0)
    def _():
        xp_ref[:, 0:SUBLANES, :] = jnp.zeros((xp_ref.shape[0], SUBLANES, LANES), F32)
        st_ref[...] = jnp.zeros_like(st_ref)
        xbd_ref[...] = jnp.zeros_like(xbd_ref)
        sbd_ref[...] = jnp.zeros_like(sbd_ref)

    for j in range(SSD_CONV_DIM // LANES):
        cols = slice(j * LANES, (j + 1) * LANES)
        xp_ref[j, SUBLANES:SUBLANES + t, :] = xbc_ref[:, cols]
        acc = xp_ref[j, SUBLANES - 3:SUBLANES - 3 + t, :] * cw_ref[0:1, cols]
        for k in range(1, CONV_W):
            acc = acc + xp_ref[j, SUBLANES - 3 + k:SUBLANES - 3 + k + t, :] * cw_ref[k:k + 1, cols]
        act_ref[:, cols] = _silu(acc + cb_ref[:, cols])
        xp_ref[j, 0:SUBLANES, :] = xbc_ref[t - SUBLANES:t, cols]
    conv_ref[0] = xbc_ref[t - SUBLANES:t, :]

    dtv = _softplus(dt_ref[...] + dtb_ref[...])
    da = dtv * (-jnp.exp(alog_ref[...]))
    row = lax.broadcasted_iota(jnp.int32, (t, t), 0)
    col = lax.broadcasted_iota(jnp.int32, (t, t), 1)
    causal = row >= col
    tril = causal.astype(BF16)
    d_hi, d_mid, d_lo = _split3(da)
    a2 = (_dot(tril, d_hi) + (_dot(tril, d_mid) + _dot(tril, d_lo))) * LOG2E
    arow_t = (a2 - jnp.log(dtv) * LOG2E).T

    lane = lax.broadcasted_iota(jnp.int32, (t, LANES), 1)
    a_hi, a_mid, a_lo = _split3(jnp.where(lane < SSD_HEADS, a2, 0.0))
    packed = (a_hi.astype(F32) + pltpu.roll(a_mid.astype(F32), SSD_HEADS, 1)
              + pltpu.roll(a_lo.astype(F32), 2 * SSD_HEADS, 1)).astype(BF16)
    acol_ref[...] = _dot(packed, sel_ref[...])

    lane_g = lax.broadcasted_iota(jnp.int32, (1, gw), 1) // hd
    for g in range(SSD_GROUPS):
        xs_g = act_ref[:, g * gw:(g + 1) * gw]
        b_g = act_ref[:, SSD_D_INNER + g * SSD_STATE:SSD_D_INNER + (g + 1) * SSD_STATE]
        c_g = act_ref[:, SSD_D_INNER + SSD_BC_DIM + g * SSD_STATE:
                      SSD_D_INNER + SSD_BC_DIM + (g + 1) * SSD_STATE]
        cb = jnp.where(causal, _dot_nt(c_g.astype(BF16), b_g.astype(BF16)), 0.0)
        b_gt = b_g.T
        st_g = st_ref[g]
        xs_b = xs_g.astype(BF16)
        for r in range(hpg):
            xbd_ref[g, r * t:(r + 1) * t, r * hd:(r + 1) * hd] = xs_b[:, r * hd:(r + 1) * hd]
        m_parts, d_parts, w_parts = [], [], []
        dec_g = jnp.zeros((1, gw), F32)
        for r in range(hpg):
            h = g * hpg + r
            acol = acol_ref[:, h * LANES:(h + 1) * LANES]
            arow = arow_t[h:h + 1, :]
            aend = acol[t - 1:t, :]
            m_parts.append((cb * jnp.exp2(jnp.minimum(acol - arow, EXP2_CAP))).astype(BF16))
            d_parts.append((c_g * jnp.exp2(acol)).astype(BF16))
            w_parts.append((b_gt * jnp.exp2(aend - arow)).astype(BF16))
            dec_g = jnp.where(lane_g == r, jnp.exp2(jnp.concatenate([aend, aend], axis=1)), dec_g)
        x_bd = xbd_ref[g]
        y_g = (_dot(jnp.concatenate(m_parts, axis=1), x_bd) + _dot(jnp.concatenate(d_parts, axis=1), sbd_ref[g])
               + xs_g * dskip_ref[:, g * gw:(g + 1) * gw])
        st_new = st_g * dec_g + _dot(jnp.concatenate(w_parts, axis=1), x_bd)
        st_ref[g] = st_new
        st_b = st_new.astype(BF16)
        for r in range(hpg):
            sbd_ref[g, r * SSD_STATE:(r + 1) * SSD_STATE, r * hd:(r + 1) * hd] = st_b[:, r * hd:(r + 1) * hd]
        yz = y_g * _silu(z_ref[:, g * gw:(g + 1) * gw])
        y_ref[:, g * gw:(g + 1) * gw] = _rms(yz, ng_ref[:, g * gw:(g + 1) * gw]).astype(y_ref.dtype)

    @pl.when(c == pl.num_programs(1) - 1)
    def _():
        for g in range(SSD_GROUPS):
            state_ref[0, g * hpg:(g + 1) * hpg] = st_ref[g].T.reshape(hpg, SSD_HEAD_DIM, SSD_STATE)


def ssd_prompt(z, xbc, dt, conv_w, conv_b, dt_bias, a_log, d_skip, norm_g, batch, seq):
    t = SSD_CHUNK
    nc = seq // t
    pad = LANES - SSD_HEADS
    rowblk = lambda b, c: (b * nc + c, 0)
    fixed = lambda b, c: (0, 0)
    y, conv_tail, state = pl.pallas_call(
        _ssd_chunk_kernel,
        grid=(batch, nc),
        in_specs=[pl.BlockSpec((t, SSD_D_INNER), rowblk),
                  pl.BlockSpec((t, SSD_CONV_DIM), rowblk),
                  pl.BlockSpec((t, LANES), rowblk),
                  pl.BlockSpec((CONV_W, SSD_CONV_DIM), fixed),
                  pl.BlockSpec((1, SSD_CONV_DIM), fixed),
                  pl.BlockSpec((1, LANES), fixed),
                  pl.BlockSpec((1, LANES), fixed),
                  pl.BlockSpec((1, SSD_D_INNER), fixed),
                  pl.BlockSpec((1, SSD_D_INNER), fixed),
                  pl.BlockSpec((LANES, SSD_HEADS * LANES), fixed)],
        out_specs=[pl.BlockSpec((t, SSD_D_INNER), rowblk),
                   pl.BlockSpec((1, SUBLANES, SSD_CONV_DIM), lambda b, c: (b, 0, 0)),
                   pl.BlockSpec((1, SSD_HEADS, SSD_HEAD_DIM, SSD_STATE), lambda b, c: (b, 0, 0, 0))],
        out_shape=[jax.ShapeDtypeStruct((batch * seq, SSD_D_INNER), BF16),
                   jax.ShapeDtypeStruct((batch, SUBLANES, SSD_CONV_DIM), F32),
                   jax.ShapeDtypeStruct((batch, SSD_HEADS, SSD_HEAD_DIM, SSD_STATE), F32)],
        scratch_shapes=[pltpu.VMEM((SSD_CONV_DIM // LANES, SUBLANES + t, LANES), F32),
                        pltpu.VMEM((t, SSD_CONV_DIM), F32),
                        pltpu.VMEM((t, SSD_HEADS * LANES), F32),
                        pltpu.VMEM((SSD_GROUPS, SSD_STATE, SSD_GROUP_W), F32),
                        pltpu.VMEM((SSD_GROUPS, SSD_HEADS_PER_GROUP * t, SSD_GROUP_W), BF16),
                        pltpu.VMEM((SSD_GROUPS, SSD_HEADS_PER_GROUP * SSD_STATE, SSD_GROUP_W), BF16)],
        compiler_params=_cparams("parallel", "arbitrary"),
        name="ssd_prompt",
    )(z, xbc, dt, conv_w, conv_b.reshape(1, -1), jnp.pad(dt_bias, (0, pad)).reshape(1, LANES),
      jnp.pad(a_log, (0, pad)).reshape(1, LANES), jnp.repeat(d_skip, SSD_HEAD_DIM).reshape(1, -1),
      norm_g.reshape(1, -1), _ssd_head_select())
    return y, conv_tail[:, SUBLANES - (CONV_W - 1):, :], state


MOBA_HEADS_PER_STEP = 4


def _col_reduce(x, pair_op, reduce_fn):
    n = x.shape[0]
    while n > 8 * SUBLANES and n % 2 == 0:
        n //= 2
        x = pair_op(x[:n], x[n:])
    return reduce_fn(x, axis=0, keepdims=True)


def _moba_prompt_kernel(q_ref, k_ref, v_ref, o_ref, kb_ref, vt_ref, sel_ref, s_ref, p_ref):
    qi = pl.program_id(2)
    blk = MOBA_BLOCK
    seq = k_ref.shape[0]
    nb = seq // blk
    nbp = sel_ref.shape[2]
    hd = MOBA_HEAD_DIM
    heads = q_ref.shape[1] // hd

    def prepare(hh):
        cols = slice(hh * hd, (hh + 1) * hd)
        means = []
        for j in range(nb):
            kj = k_ref[j * blk:(j + 1) * blk, cols]
            means.append(jnp.sum(kj, axis=0, keepdims=True) * (1.0 / blk))
            kb_ref[hh, j * blk:(j + 1) * blk, :] = kj.astype(BF16)
        for j in range(seq // LANES):
            vt_ref[hh, :, j * LANES:(j + 1) * LANES] = v_ref[j * LANES:(j + 1) * LANES, cols].T.astype(BF16)
        if nbp > nb:
            means.append(jnp.zeros((nbp - nb, hd), F32))
        kmean = jnp.concatenate(means, axis=0)
        sub = lax.broadcasted_iota(jnp.int32, (nbp, blk), 0)
        for i in range(nb):
            gm = jnp.where(sub < i, _dot3_nt(kmean, q_ref[i * blk:(i + 1) * blk, cols]), -jnp.inf)
            rank = jnp.zeros((nbp, blk), F32)
            for m in range(i):
                gm_m = gm[m:m + 1, :]
                beats = (gm_m > gm) | ((gm_m == gm) & (sub > m))
                rank = rank + jnp.where(beats, 1.0, 0.0)
            chosen = ((sub < i) & (rank < MOBA_TOPK)) | (sub == i)
            sel_ref[hh, i] = jnp.where(chosen, 1.0, 0.0)

    @pl.when(qi == 0)
    def _():
        for hh in range(heads):
            prepare(hh)

    causal = (lax.broadcasted_iota(jnp.int32, (blk, blk), 0)
              <= lax.broadcasted_iota(jnp.int32, (blk, blk), 1))

    def attend_head(hh, i):
        cols = slice(hh * hd, (hh + 1) * hd)
        qb = (q_ref[i * blk:(i + 1) * blk, cols] * (hd ** -0.5)).astype(BF16)
        sel = sel_ref[hh, i]
        n_keys = (i + 1) * blk
        m = None
        for j in range(i + 1):
            rows = slice(j * blk, (j + 1) * blk)
            s = _dot_nt(kb_ref[hh, rows, :], qb)
            if j < i:
                s = s + jnp.where(sel[j:j + 1, :] > 0.0, 0.0, NEG_BIG)
            else:
                s = jnp.where(causal, s, NEG_BIG)
            s_ref[hh, rows, :] = s
            bm = _col_reduce(s, jnp.maximum, jnp.max)
            m = bm if m is None else jnp.maximum(m, bm)
        l = jnp.zeros((1, blk), F32)
        for j in range(i + 1):
            rows = slice(j * blk, (j + 1) * blk)
            p = jnp.exp(s_ref[hh, rows, :] - m)
            l = l + _col_reduce(p, jnp.add, jnp.sum)
            p_ref[hh, rows, :] = p.astype(BF16)
        acc = _dot(vt_ref[hh, :, 0:n_keys], p_ref[hh, 0:n_keys, :])
        o_ref[:, cols] = (acc / l).T.astype(o_ref.dtype)

    def attend(i):
        for hh in range(heads):
            attend_head(hh, i)

    for i in range(nb):
        pl.when(qi == i)(functools.partial(attend, i))


def moba_prompt(q, k, v, batch, seq):
    nb = seq // MOBA_BLOCK
    hd = MOBA_HEAD_DIM
    hps = MOBA_HEADS_PER_STEP
    wide = lambda b, h, i: (b, h)
    return pl.pallas_call(
        _moba_prompt_kernel,
        grid=(batch, MOBA_HEADS // hps, nb),
        in_specs=[pl.BlockSpec((seq, hps * hd), wide),
                  pl.BlockSpec((seq, hps * hd), wide),
                  pl.BlockSpec((seq, hps * hd), wide)],
        out_specs=pl.BlockSpec((MOBA_BLOCK, hps * hd), lambda b, h, i: (b * nb + i, h)),
        out_shape=jax.ShapeDtypeStruct((batch * seq, D_MODEL), BF16),
        scratch_shapes=[pltpu.VMEM((hps, seq, hd), BF16),
                        pltpu.VMEM((hps, hd, seq), BF16),
                        pltpu.VMEM((hps, nb, -(-nb // SUBLANES) * SUBLANES, MOBA_BLOCK), F32),
                        pltpu.VMEM((hps, seq, MOBA_BLOCK), F32),
                        pltpu.VMEM((hps, seq, MOBA_BLOCK), BF16)],
        compiler_params=_cparams("parallel", "parallel", "arbitrary"),
        name="moba_prompt",
    )(q, k, v)


def _lru_gates(xc, wa, ba, wx, bx, lam):
    xb = xc.astype(BF16)
    r = _sigmoid(_dot(xb, wa) + ba)
    i = _sigmoid(_dot(xb, wx) + bx)
    log_a = (-LRU_C) * r * _softplus(-lam)
    a = jnp.exp(log_a)
    u = jnp.sqrt(-jnp.tanh(log_a) * (a * a + 1.0)) * (i * xc)
    return a, u


def _lru_chunk_kernel(gate_ref, x_ref, cw_ref, cb_ref, wa_ref, ba_ref, wx_ref, bx_ref, lam_ref,
                      y_ref, conv_ref, hlast_ref, xp_ref, a_ref, u_ref, h_ref):
    c = pl.program_id(1)
    t = LRU_CHUNK
    w = LRU_BLOCK_W

    @pl.when(c == 0)
    def _():
        xp_ref[:, 0:SUBLANES, :] = jnp.zeros((LRU_BLOCKS, SUBLANES, w), F32)
        h_ref[...] = jnp.zeros_like(h_ref)

    for n in range(LRU_BLOCKS):
        cols = slice(n * w, (n + 1) * w)
        xp_ref[n, SUBLANES:SUBLANES + t, :] = x_ref[:, cols]
        acc = xp_ref[n, SUBLANES - 3:SUBLANES - 3 + t, :] * cw_ref[0:1, cols]
        for k in range(1, CONV_W):
            acc = acc + xp_ref[n, SUBLANES - 3 + k:SUBLANES - 3 + k + t, :] * cw_ref[k:k + 1, cols]
        xc = acc + cb_ref[:, cols]
        a, u = _lru_gates(xc, wa_ref[n], ba_ref[:, cols], wx_ref[n], bx_ref[:, cols], lam_ref[:, cols])
        a_ref[:, cols] = a
        u_ref[:, cols] = u
        xp_ref[n, 0:SUBLANES, :] = x_ref[t - SUBLANES:t, cols]
    conv_ref[0] = x_ref[t - SUBLANES:t, :]

    srow = lax.broadcasted_iota(jnp.int32, (SUBLANES, LRU_WIDTH), 0)

    def tile(i, h):
        rows = pl.ds(pl.multiple_of(i * SUBLANES, SUBLANES), SUBLANES)
        a = a_ref[rows, :]
        u = u_ref[rows, :]
        for d in (1, 2, 4):
            keep = srow >= d
            a_sh = jnp.where(keep, pltpu.roll(a, d, 0), 1.0)
            u_sh = jnp.where(keep, pltpu.roll(u, d, 0), 0.0)
            u = a * u_sh + u
            a = a * a_sh
        hs = a * h + u
        u_ref[rows, :] = hs
        return jnp.broadcast_to(hs[SUBLANES - 1:SUBLANES, :], (SUBLANES, LRU_WIDTH))

    h = lax.fori_loop(0, t // SUBLANES, tile, h_ref[...])
    h_ref[...] = h
    hlast_ref[0] = h
    y_ref[...] = (_gelu_tanh(gate_ref[...]) * u_ref[...]).astype(y_ref.dtype)


def lru_prompt(gate_br, x_br, conv_w, conv_b, w_a, b_a, w_x, b_x, lam, batch, seq):
    t = LRU_CHUNK
    nc = seq // t
    wd = LRU_WIDTH
    rowblk = lambda b, c: (b * nc + c, 0)
    fixed = lambda b, c: (0, 0)
    fixed3 = lambda b, c: (0, 0, 0)
    perb = lambda b, c: (b, 0, 0)
    y, conv_tail, h_last = pl.pallas_call(
        _lru_chunk_kernel,
        grid=(batch, nc),
        in_specs=[pl.BlockSpec((t, wd), rowblk),
                  pl.BlockSpec((t, wd), rowblk),
                  pl.BlockSpec((CONV_W, wd), fixed),
                  pl.BlockSpec((1, wd), fixed),
                  pl.BlockSpec((LRU_BLOCKS, LRU_BLOCK_W, LRU_BLOCK_W), fixed3),
                  pl.BlockSpec((1, wd), fixed),
                  pl.BlockSpec((LRU_BLOCKS, LRU_BLOCK_W, LRU_BLOCK_W), fixed3),
                  pl.BlockSpec((1, wd), fixed),
                  pl.BlockSpec((1, wd), fixed)],
        out_specs=[pl.BlockSpec((t, wd), rowblk),
                   pl.BlockSpec((1, SUBLANES, wd), perb),
                   pl.BlockSpec((1, SUBLANES, wd), perb)],
        out_shape=[jax.ShapeDtypeStruct((batch * seq, wd), BF16),
                   jax.ShapeDtypeStruct((batch, SUBLANES, wd), F32),
                   jax.ShapeDtypeStruct((batch, SUBLANES, wd), F32)],
        scratch_shapes=[pltpu.VMEM((LRU_BLOCKS, SUBLANES + t, LRU_BLOCK_W), F32),
                        pltpu.VMEM((t, wd), F32),
                        pltpu.VMEM((t, wd), F32),
                        pltpu.VMEM((SUBLANES, wd), F32)],
        compiler_params=_cparams("parallel", "arbitrary"),
        name="lru_prompt",
    )(gate_br, x_br, conv_w, conv_b.reshape(1, wd), w_a.astype(BF16), b_a.reshape(1, wd),
      w_x.astype(BF16), b_x.reshape(1, wd), lam.reshape(1, wd))
    return y, conv_tail[:, SUBLANES - (CONV_W - 1):, :], h_last[:, 0, :]


def _ssd_step_kernel(z_ref, xbc_ref, dt_ref, cs_ref, st_ref, cw_ref, cb_ref, dtb_ref, alog_ref, dskip_ref, ng_ref,
                     y_ref, cso_ref, sto_ref):
    gw = SSD_GROUP_W
    hpg = SSD_HEADS_PER_GROUP
    x = xbc_ref[0]
    buf = cs_ref[0]
    conv = x * cw_ref[CONV_W - 1:CONV_W, :]
    for k in range(CONV_W - 1):
        conv = conv + buf[k:k + 1, :] * cw_ref[k:k + 1, :]
    act = _silu(conv + cb_ref[...])
    cso_ref[0, 0:CONV_W - 2, :] = buf[1:CONV_W - 1, :]
    cso_ref[0, CONV_W - 2:CONV_W - 1, :] = x

    dtv = _softplus(dt_ref[0] + dtb_ref[...])
    dec = jnp.exp(dtv * (-jnp.exp(alog_ref[...])))
    lane_g = lax.broadcasted_iota(jnp.int32, (1, gw), 1) // SSD_HEAD_DIM
    first_row = lax.broadcasted_iota(jnp.int32, (SUBLANES, gw), 0) == 0
    ys = []
    for g in range(SSD_GROUPS):
        xs_g = act[:, g * gw:(g + 1) * gw]
        b_g = act[:, SSD_D_INNER + g * SSD_STATE:SSD_D_INNER + (g + 1) * SSD_STATE]
        c_g = act[:, SSD_D_INNER + SSD_BC_DIM + g * SSD_STATE:SSD_D_INNER + SSD_BC_DIM + (g + 1) * SSD_STATE]
        dt_e = jnp.zeros((1, gw), F32)
        dec_rows = []
        for r in range(hpg):
            h = g * hpg + r
            dt_e = jnp.where(lane_g == r, dtv[:, h:h + 1], dt_e)
            dec_rows.append(jnp.broadcast_to(dec[:, h:h + 1], (SSD_HEAD_DIM, SSD_STATE)))
        xdt = xs_g * dt_e
        x8 = jnp.where(first_row, jnp.broadcast_to(xdt, (SUBLANES, gw)), 0.0).astype(BF16)
        b8 = jnp.broadcast_to(b_g, (SUBLANES, SSD_STATE)).astype(BF16)
        outer = lax.dot_general(x8, b8, (((0,), (0,)), ((), ())), preferred_element_type=F32)
        old = st_ref[0, g * hpg:(g + 1) * hpg].reshape(gw, SSD_STATE)
        new = old * jnp.concatenate(dec_rows, axis=0) + outer
        sto_ref[0, g * hpg:(g + 1) * hpg] = new.reshape(hpg, SSD_HEAD_DIM, SSD_STATE)
        c8 = jnp.broadcast_to(c_g, (SUBLANES, SSD_STATE)).astype(BF16)
        y_g = _dot_nt(c8, new.astype(BF16))[0:1, :] + xs_g * dskip_ref[:, g * gw:(g + 1) * gw]
        yz = y_g * _silu(z_ref[0][:, g * gw:(g + 1) * gw])
        ys.append(_rms(yz, ng_ref[:, g * gw:(g + 1) * gw]))
    y_ref[0] = jnp.concatenate(ys, axis=1)


def ssd_step(z, xbc, dt, conv_state, ssm_state, layer, conv_w, conv_b, dt_bias, a_log, d_skip, norm_g):
    nb = z.shape[0]
    pad = LANES - SSD_HEADS
    per3 = lambda b: (b, 0, 0)
    fixed = lambda b: (0, 0)
    y, conv_new, state_new = pl.pallas_call(
        _ssd_step_kernel,
        grid=(nb,),
        in_specs=[pl.BlockSpec((1, 1, SSD_D_INNER), per3),
                  pl.BlockSpec((1, 1, SSD_CONV_DIM), per3),
                  pl.BlockSpec((1, 1, LANES), per3),
                  pl.BlockSpec((None, 1, CONV_W - 1, SSD_CONV_DIM), lambda b: (layer, b, 0, 0)),
                  pl.BlockSpec((None, 1, SSD_HEADS, SSD_HEAD_DIM, SSD_STATE), lambda b: (layer, b, 0, 0, 0)),
                  pl.BlockSpec((CONV_W, SSD_CONV_DIM), fixed),
                  pl.BlockSpec((1, SSD_CONV_DIM), fixed),
                  pl.BlockSpec((1, LANES), fixed),
                  pl.BlockSpec((1, LANES), fixed),
                  pl.BlockSpec((1, SSD_D_INNER), fixed),
                  pl.BlockSpec((1, SSD_D_INNER), fixed)],
        out_specs=[pl.BlockSpec((1, 1, SSD_D_INNER), per3),
                   pl.BlockSpec((1, CONV_W - 1, SSD_CONV_DIM), per3),
                   pl.BlockSpec((1, SSD_HEADS, SSD_HEAD_DIM, SSD_STATE), lambda b: (b, 0, 0, 0))],
        out_shape=[jax.ShapeDtypeStruct((nb, 1, SSD_D_INNER), F32),
                   jax.ShapeDtypeStruct((nb, CONV_W - 1, SSD_CONV_DIM), F32),
                   jax.ShapeDtypeStruct((nb, SSD_HEADS, SSD_HEAD_DIM, SSD_STATE), F32)],
        compiler_params=_cparams("parallel"),
        name="ssd_step",
    )(z.reshape(nb, 1, -1), xbc.reshape(nb, 1, -1), dt.reshape(nb, 1, -1), conv_state, ssm_state,
      conv_w, conv_b.reshape(1, -1), jnp.pad(dt_bias, (0, pad)).reshape(1, LANES),
      jnp.pad(a_log, (0, pad)).reshape(1, LANES), jnp.repeat(d_skip, SSD_HEAD_DIM).reshape(1, -1),
      norm_g.reshape(1, -1))
    return y.reshape(nb, SSD_D_INNER), conv_new, state_new


def _lru_step_kernel(gate_ref, x_ref, cs_ref, h0_ref, cw_ref, cb_ref, wa_ref, ba_ref, wx_ref, bx_ref, lam_ref,
                     y_ref, cso_ref, ho_ref):
    w = LRU_BLOCK_W
    x = x_ref[...]
    xc = x * cw_ref[CONV_W - 1:CONV_W, :] + cb_ref[...]
    for k in range(CONV_W - 1):
        xc = xc + cs_ref[k] * cw_ref[k:k + 1, :]
    for k in range(CONV_W - 2):
        cso_ref[k] = cs_ref[k + 1]
    cso_ref[CONV_W - 2] = x
    for n in range(LRU_BLOCKS):
        cols = slice(n * w, (n + 1) * w)
        a, u = _lru_gates(xc[:, cols], wa_ref[n], ba_ref[:, cols], wx_ref[n], bx_ref[:, cols], lam_ref[:, cols])
        h = a * h0_ref[:, cols] + u
        ho_ref[:, cols] = h
        y_ref[:, cols] = _gelu_tanh(gate_ref[:, cols]) * h


def lru_step(gate_br, x_br, conv_state, h0, conv_w, conv_b, w_a, b_a, w_x, b_x, lam):
    nb, wd = x_br.shape
    return pl.pallas_call(
        _lru_step_kernel,
        out_shape=[jax.ShapeDtypeStruct((nb, wd), F32),
                   jax.ShapeDtypeStruct((CONV_W - 1, nb, wd), F32),
                   jax.ShapeDtypeStruct((nb, wd), F32)],
        compiler_params=pltpu.CompilerParams(vmem_limit_bytes=VMEM_LIMIT),
        name="lru_step",
    )(gate_br, x_br, conv_state, h0, conv_w, conv_b.reshape(1, wd), w_a.astype(BF16), b_a.reshape(1, wd),
      w_x.astype(BF16), b_x.reshape(1, wd), lam.reshape(1, wd))


MOBA_SUM_PAGES = 16


def _moba_block_sums_kernel(tbl_ref, *refs):
    del tbl_ref
    pages, o_ref = refs[:-1], refs[-1]
    for k in range(len(pages) // PAGES_PER_BLOCK):
        s = jnp.sum(pages[PAGES_PER_BLOCK * k][...], axis=0)
        for j in range(1, PAGES_PER_BLOCK):
            s = s + jnp.sum(pages[PAGES_PER_BLOCK * k + j][...], axis=0)
        o_ref[0, k] = s


def moba_block_sums(pool, layer, table):
    nb, n_pages = table.shape
    steps = n_pages // MOBA_SUM_PAGES
    blocks_per_step = MOBA_SUM_PAGES // PAGES_PER_BLOCK
    page_shape = pool.shape[2:]

    def page_spec(k):
        return pl.BlockSpec((None, None) + page_shape,
                            lambda b, c, tbl: (layer, tbl[b * n_pages + c * MOBA_SUM_PAGES + k], 0, 0, 0))

    return pl.pallas_call(
        _moba_block_sums_kernel,
        grid_spec=pltpu.PrefetchScalarGridSpec(
            num_scalar_prefetch=1,
            grid=(nb, steps),
            in_specs=[page_spec(k) for k in range(MOBA_SUM_PAGES)],
            out_specs=pl.BlockSpec((1, blocks_per_step) + page_shape[1:], lambda b, c, tbl: (b, c, 0, 0))),
        out_shape=jax.ShapeDtypeStruct((nb, n_pages // PAGES_PER_BLOCK) + page_shape[1:], F32),
        compiler_params=_cparams("parallel", "arbitrary"),
        name="moba_block_sums",
    )(table.reshape(-1), *([pool] * MOBA_SUM_PAGES))


def _moba_gate_kernel(q_ref, bs_ref, sel_ref):
    n_blocks = bs_ref.shape[1]
    q = q_ref[0]
    gates = [jnp.sum((bs_ref[0, n] * (1.0 / MOBA_BLOCK)) * q, axis=-1, keepdims=True) for n in range(n_blocks)]
    lane = lax.broadcasted_iota(jnp.int32, (MOBA_HEADS, LANES), 1)
    out = jnp.zeros((MOBA_HEADS, LANES), F32)
    for k in range(MOBA_TOPK):
        best = functools.reduce(jnp.maximum, gates)
        idx = jnp.full((MOBA_HEADS, 1), float(n_blocks), F32)
        for n in reversed(range(n_blocks)):
            idx = jnp.where(gates[n] == best, float(n), idx)
        out = jnp.where(lane == k, idx, out)
        gates = [jnp.where(idx == float(n), -jnp.inf, gates[n]) for n in range(n_blocks)]
    sel_ref[0] = out.astype(jnp.int32)


def moba_gate(q, block_sums):
    nb = q.shape[0]
    n_blocks = block_sums.shape[1]
    return pl.pallas_call(
        _moba_gate_kernel,
        grid=(nb,),
        in_specs=[pl.BlockSpec((1, MOBA_HEADS, MOBA_HEAD_DIM), lambda b: (b, 0, 0)),
                  pl.BlockSpec((1, n_blocks, MOBA_HEADS, MOBA_HEAD_DIM), lambda b: (b, 0, 0, 0))],
        out_specs=pl.BlockSpec((1, MOBA_HEADS, LANES), lambda b: (b, 0, 0)),
        out_shape=jax.ShapeDtypeStruct((nb, MOBA_HEADS, LANES), jnp.int32),
        compiler_params=_cparams("parallel"),
        name="moba_gate",
    )(q.reshape(nb, MOBA_HEADS, MOBA_HEAD_DIM), block_sums)


def _moba_decode_kernel(tbl_ref, sel_ref, q_ref, kn_ref, vn_ref, kpool_ref, vpool_ref, o_ref, kbuf, vbuf, sem,
                        *, layer, n_table_pages):
    b = pl.program_id(0)
    slot = lax.rem(b, 2)
    n_sel = MOBA_TOPK * PAGES_PER_BLOCK
    hd = MOBA_HEAD_DIM

    def page_copies(bb, sl):
        out = []
        for hh in range(MOBA_HEADS):
            for k in range(MOBA_TOPK):
                blk = sel_ref[(bb * MOBA_HEADS + hh) * MOBA_TOPK + k]
                for s in range(PAGES_PER_BLOCK):
                    page = tbl_ref[bb * n_table_pages + blk * PAGES_PER_BLOCK + s]
                    i = k * PAGES_PER_BLOCK + s
                    out.append(pltpu.make_async_copy(kpool_ref.at[layer, page, :, hh, :], kbuf.at[sl, hh, i],
                                                     sem.at[sl, 0, hh, i]))
                    out.append(pltpu.make_async_copy(vpool_ref.at[layer, page, :, hh, :], vbuf.at[sl, hh, i],
                                                     sem.at[sl, 1, hh, i]))
        return out

    def start_all(copies):
        for i, c in enumerate(copies):
            c.start(priority=i % 2)

    @pl.when(b == 0)
    def _():
        start_all(page_copies(b, 0))

    @pl.when(b + 1 < pl.num_programs(0))
    def _():
        start_all(page_copies(b + 1, 1 - slot))

    for c in page_copies(b, slot):
        c.wait()

    scale = hd ** -0.5
    outs = []
    for hh in range(MOBA_HEADS):
        cols = slice(hh * hd, (hh + 1) * hd)
        q = q_ref[0][:, cols]
        q8 = jnp.broadcast_to(q, (SUBLANES, hd)).astype(BF16)
        k_cat = kbuf[slot, hh].reshape(n_sel * PAGE_SIZE, hd).astype(BF16)
        v_cat = vbuf[slot, hh].reshape(n_sel * PAGE_SIZE, hd).astype(BF16)
        s = _dot_nt(q8, k_cat) * scale
        s_new = jnp.sum(q * kn_ref[0][:, cols], axis=1, keepdims=True) * scale
        m = jnp.maximum(jnp.max(s, axis=1, keepdims=True), s_new)
        p = jnp.exp(s - m)
        p_new = jnp.exp(s_new - m)
        denom = jnp.sum(p, axis=1, keepdims=True) + p_new
        o = (_dot(p.astype(BF16), v_cat) + p_new * vn_ref[0][:, cols]) / denom
        outs.append(o[0:1, :])
    o_ref[0] = jnp.concatenate(outs, axis=1)


def moba_decode(q, k_new, v_new, k_pool, v_pool, layer, table, sel):
    nb = q.shape[0]
    n_pages = table.shape[1]
    hd = MOBA_HEAD_DIM
    n_sel = MOBA_TOPK * PAGES_PER_BLOCK
    tok = pl.BlockSpec((1, 1, D_MODEL), lambda b, tbl, sl: (b, 0, 0))
    hbm = pl.BlockSpec(memory_space=pl.ANY)
    out = pl.pallas_call(
        functools.partial(_moba_decode_kernel, layer=layer, n_table_pages=n_pages),
        grid_spec=pltpu.PrefetchScalarGridSpec(
            num_scalar_prefetch=2,
            grid=(nb,),
            in_specs=[tok, tok, tok, hbm, hbm],
            out_specs=tok,
            scratch_shapes=[pltpu.VMEM((2, MOBA_HEADS, n_sel, PAGE_SIZE, hd), F32),
                            pltpu.VMEM((2, MOBA_HEADS, n_sel, PAGE_SIZE, hd), F32),
                            pltpu.SemaphoreType.DMA((2, 2, MOBA_HEADS, n_sel))]),
        out_shape=jax.ShapeDtypeStruct((nb, 1, D_MODEL), F32),
        compiler_params=_cparams("arbitrary"),
        name="moba_decode",
    )(table.reshape(-1), sel[:, :, :MOBA_TOPK].reshape(-1),
      q.reshape(nb, 1, D_MODEL), k_new.reshape(nb, 1, D_MODEL), v_new.reshape(nb, 1, D_MODEL), k_pool, v_pool)
    return out.reshape(nb, D_MODEL)


def _prep_weights(W):
    bf = lambda a: a.astype(BF16)
    w_in, w_qkv, w_lru = lax.optimization_barrier((bf(W["ssd_w_in"]), bf(W["moba_w_qkv"]), bf(W["lru_w_in"])))
    zx = SSD_D_INNER + SSD_CONV_DIM
    return dict(
        ssd_w_z=w_in[:, :, :SSD_D_INNER],
        ssd_w_xbc=w_in[:, :, SSD_D_INNER:zx],
        ssd_w_dt=jnp.pad(w_in[:, :, zx:], ((0, 0), (0, 0), (0, LANES - SSD_HEADS))),
        ssd_w_out=bf(W["ssd_w_out"]),
        moba_w_q=w_qkv[:, :, :D_MODEL],
        moba_w_k=w_qkv[:, :, D_MODEL:2 * D_MODEL],
        moba_w_v=w_qkv[:, :, 2 * D_MODEL:],
        moba_w_o=bf(W["moba_w_o"]),
        lru_w_gate=w_lru[:, :, :LRU_WIDTH],
        lru_w_x=w_lru[:, :, LRU_WIDTH:],
        lru_w_out=bf(W["lru_w_out"]),
        mlp_w_up=bf(W["mlp_w_up"]), mlp_w_down=bf(W["mlp_w_down"]),
        ple_w_proj=bf(W["ple_w_proj"]), ple_w_gate=bf(W["ple_w_gate"]),
    )


def _trunk(h, p, W, Wb, tiles, ssd_fn, moba_fn, lru_fn):
    tm, tf = tiles
    ssd_out, moba_out, lru_out = [], [], []
    xn = rmsnorm_bf16(h, W["norm_mix"][0], tm=tm)
    for i in range(DEPTH):
        j, kind = i // N_MIXERS, i % N_MIXERS
        proj = lambda *names: multi_linear(xn, [Wb[n] for n in names], j, tm=tm)
        if kind == 0:
            y, conv, state = ssd_fn(j, *proj("ssd_w_z", "ssd_w_xbc", "ssd_w_dt"))
            ssd_out.append((conv, state))
            w_out = Wb["ssd_w_out"]
        elif kind == 1:
            q, k, v = proj("moba_w_q", "moba_w_k", "moba_w_v")
            moba_out.append((k, v))
            y, w_out = moba_fn(j, q, k, v), Wb["moba_w_o"]
        else:
            y, conv, state = lru_fn(j, *proj("lru_w_gate", "lru_w_x"))
            lru_out.append((conv, state))
            w_out = Wb["lru_w_out"]
        final = i == DEPTH - 1
        g_next = W["norm_final"] if final else W["norm_mix"][i + 1]
        out = layer_tail(h, y, w_out, j, W["norm_mlp"][i], Wb["mlp_w_up"], Wb["mlp_w_down"], p,
                         W["norm_ple"][i], Wb["ple_w_gate"], Wb["ple_w_proj"], i, g_next,
                         tm=tm, tf=tf, final=final)
        h, xn = (out, None) if final else out
    stack = lambda pairs, n: jnp.stack([pr[n] for pr in pairs])
    return (h, stack(ssd_out, 0), stack(ssd_out, 1), stack(moba_out, 0), stack(moba_out, 1),
            stack(lru_out, 0), stack(lru_out, 1))


PROMPT_TILES = (512, 1024)
SAMPLE_TILES = (32, 1024)


def kernel(x_prompt, x_sample, state_ssd_conv, state_ssd, cache_k, cache_v, page_table, state_lru_conv,
           state_lru, p_prompt, p_sample, norm_mix, norm_mlp, norm_ple, norm_final, ssd_w_in, ssd_conv_w,
           ssd_conv_b, ssd_dt_bias, ssd_a_log, ssd_d, ssd_norm, ssd_w_out, moba_w_qkv, moba_w_o, lru_w_in,
           lru_conv_w, lru_conv_b, lru_w_a, lru_b_a, lru_w_x, lru_b_x, lru_lambda, lru_w_out, mlp_w_up,
           mlp_w_down, ple_w_proj, ple_w_gate):
    W = dict(norm_mix=norm_mix, norm_mlp=norm_mlp, norm_ple=norm_ple, norm_final=norm_final,
             ssd_w_in=ssd_w_in, ssd_w_out=ssd_w_out, moba_w_qkv=moba_w_qkv, moba_w_o=moba_w_o,
             lru_w_in=lru_w_in, lru_w_out=lru_w_out, mlp_w_up=mlp_w_up, mlp_w_down=mlp_w_down,
             ple_w_proj=ple_w_proj, ple_w_gate=ple_w_gate)
    Wb = _prep_weights(W)
    batch, seq, d = x_prompt.shape
    dec_batch, dec_seq, _ = x_sample.shape
    assert dec_seq == 1 and seq % MOBA_BLOCK == 0 and seq % LRU_CHUNK == 0

    ssd_args = lambda j: (ssd_conv_w[j], ssd_conv_b[j], ssd_dt_bias[j], ssd_a_log[j], ssd_d[j], ssd_norm[j])
    lru_args = lambda j: (lru_conv_w[j], lru_conv_b[j], lru_w_a[j], lru_b_a[j].reshape(-1), lru_w_x[j],
                          lru_b_x[j].reshape(-1), lru_lambda[j])

    out_p = _trunk(
        x_prompt.reshape(batch * seq, d), p_prompt.reshape(DEPTH, batch * seq, PLE_DIM), W, Wb, PROMPT_TILES,
        lambda j, z, xbc, dt: ssd_prompt(z, xbc, dt, *ssd_args(j), batch, seq),
        lambda j, q, k, v: moba_prompt(q, k, v, batch, seq),
        lambda j, gate, x: lru_prompt(gate, x, *lru_args(j), batch, seq))

    def moba_sample(j, q, k, v):
        sel = moba_gate(q, moba_block_sums(cache_k, j, page_table))
        return moba_decode(q, k, v, cache_k, cache_v, j, page_table, sel)

    def lru_sample(j, gate, x):
        y, conv, h = lru_step(gate, x, jnp.swapaxes(state_lru_conv[j], 0, 1), state_lru[j], *lru_args(j))
        return y, jnp.swapaxes(conv, 0, 1), h

    out_s = _trunk(
        x_sample.reshape(dec_batch, d), p_sample.reshape(DEPTH, dec_batch, PLE_DIM), W, Wb, SAMPLE_TILES,
        lambda j, z, xbc, dt: ssd_step(z, xbc, dt, state_ssd_conv, state_ssd, j, *ssd_args(j)),
        moba_sample, lru_sample)

    def shaped(out, b, s):
        y, ssd_conv, ssd_state, k, v, lru_conv, lru_state = out
        kv_shape = (-1, b, s, MOBA_HEADS, MOBA_HEAD_DIM)
        return (y.reshape(b, s, d), ssd_conv, ssd_state, k.reshape(kv_shape), v.reshape(kv_shape),
                lru_conv, lru_state)

    yp, *rest_p = shaped(out_p, batch, seq)
    ys, *rest_s = shaped(out_s, dec_batch, dec_seq)
    return (yp, ys, *rest_p, *rest_s)
```

```python
import functools
import math

import jax
import jax.numpy as jnp
from jax import lax
from jax.experimental import pallas as pl
from jax.experimental.pallas import tpu as pltpu

D_MODEL = 1024
DEPTH = 4
N_MIXERS = 3
PLE_DIM = 256
D_FF = 4 * D_MODEL
NORM_EPS = 1e-6
CONV_W = 4
PAGE_SIZE = 128

SSD_D_INNER = 2 * D_MODEL
SSD_HEAD_DIM = 64
SSD_HEADS = SSD_D_INNER // SSD_HEAD_DIM
SSD_GROUPS = 8
SSD_HEADS_PER_GROUP = SSD_HEADS // SSD_GROUPS
SSD_STATE = 128
SSD_BC_DIM = SSD_GROUPS * SSD_STATE
SSD_CONV_DIM = SSD_D_INNER + 2 * SSD_BC_DIM
SSD_GROUP_W = SSD_D_INNER // SSD_GROUPS
SSD_CHUNK = 128

MOBA_HEADS = 8
MOBA_HEAD_DIM = D_MODEL // MOBA_HEADS
MOBA_BLOCK = 256
MOBA_TOPK = 3
PAGES_PER_BLOCK = MOBA_BLOCK // PAGE_SIZE

LRU_WIDTH = D_MODEL
LRU_BLOCKS = 8
LRU_BLOCK_W = LRU_WIDTH // LRU_BLOCKS
LRU_C = 8.0
LRU_CHUNK = 256

LANES = 128
SUBLANES = 8
VMEM_LIMIT = 56 * 1024 * 1024
NEG_BIG = -1e30
LOG2E = 1.0 / math.log(2.0)
EXP2_CAP = 126.0

F32 = jnp.float32
BF16 = jnp.bfloat16


def _cparams(*sem):
    return pltpu.CompilerParams(dimension_semantics=sem, vmem_limit_bytes=VMEM_LIMIT)


def _rms(x, g):
    var = jnp.mean(x * x, axis=-1, keepdims=True)
    return (x * lax.rsqrt(var + NORM_EPS)) * g


def _sigmoid(x):
    return 0.5 + 0.5 * jnp.tanh(0.5 * x)


def _silu(x):
    hx = 0.5 * x
    return hx + hx * jnp.tanh(hx)


def _softplus(x):
    return jnp.maximum(x, 0.0) + jnp.log1p(jnp.exp(-jnp.abs(x)))


def _gelu_tanh(x):
    return 0.5 * x * (1.0 + jnp.tanh(math.sqrt(2.0 / math.pi) * (x + 0.044715 * (x * x * x))))


def _split3(x):
    hi = x.astype(BF16)
    r1 = x - hi.astype(F32)
    mid = r1.astype(BF16)
    lo = (r1 - mid.astype(F32)).astype(BF16)
    return hi, mid, lo


def _dot(a, b):
    return jnp.dot(a, b, preferred_element_type=F32)


def _dot_nt(a, b):
    return lax.dot_general(a, b, (((1,), (1,)), ((), ())), preferred_element_type=F32)


def _dot3_nt(a, b):
    ah, am, _ = _split3(a)
    bh, bm, _ = _split3(b)
    return _dot_nt(ah, bh) + (_dot_nt(ah, bm) + _dot_nt(am, bh))


def _rmsnorm_kernel(x_ref, g_ref, o_ref):
    o_ref[...] = _rms(x_ref[...], g_ref[...]).astype(o_ref.dtype)


def rmsnorm_bf16(x, g, *, tm):
    m, d = x.shape
    return pl.pallas_call(
        _rmsnorm_kernel,
        grid=(m // tm,),
        in_specs=[pl.BlockSpec((tm, d), lambda i: (i, 0)), pl.BlockSpec((1, d), lambda i: (0, 0))],
        out_specs=pl.BlockSpec((tm, d), lambda i: (i, 0)),
        out_shape=jax.ShapeDtypeStruct((m, d), BF16),
        compiler_params=_cparams("parallel"),
        name="rmsnorm",
    )(x, g.reshape(1, d))


def _multi_linear_kernel(x_ref, *refs):
    n = len(refs) // 2
    x = x_ref[...]
    for w_ref, o_ref in zip(refs[:n], refs[n:]):
        o_ref[...] = _dot(x, w_ref[...])


def multi_linear(x, weights, layer, *, tm):
    m, d = x.shape
    return pl.pallas_call(
        _multi_linear_kernel,
        grid=(m // tm,),
        in_specs=[pl.BlockSpec((tm, d), lambda i: (i, 0))]
                 + [pl.BlockSpec((None,) + w.shape[1:], lambda i: (layer, 0, 0), pipeline_mode=pl.Buffered(1))
                    for w in weights],
        out_specs=[pl.BlockSpec((tm, w.shape[2]), lambda i: (i, 0)) for w in weights],
        out_shape=[jax.ShapeDtypeStruct((m, w.shape[2]), F32) for w in weights],
        compiler_params=_cparams("parallel"),
        name="multi_linear",
    )(x, *weights)


LAYER_TAIL_CHAINS = 2


def _layer_tail_kernel(h_ref, y_ref, wo_ref, gm_ref, wu_ref, wd_ref, p_ref, gp_ref, wg_ref, wp_ref, gn_ref,
                       *outs, final, chains, tf):
    rows_per = h_ref.shape[0] // chains
    ff = wu_ref.shape[1]
    for c in range(chains):
        rows = slice(c * rows_per, (c + 1) * rows_per)
        h1 = h_ref[rows, :] + _dot(y_ref[rows, :].astype(BF16), wo_ref[...])
        xn = _rms(h1, gm_ref[...]).astype(BF16)
        acc = None
        for f in range(ff // tf):
            hid = jnp.square(jnp.maximum(_dot(xn, wu_ref[:, f * tf:(f + 1) * tf]), 0.0))
            part = _dot(hid.astype(BF16), wd_ref[f * tf:(f + 1) * tf, :])
            acc = part if acc is None else acc + part
        h2 = h1 + acc
        gate = _sigmoid(_dot(_rms(h2, gp_ref[...]).astype(BF16), wg_ref[...]))
        h3 = h2 + _dot(p_ref[rows, :].astype(BF16), wp_ref[...]) * gate
        if final:
            outs[0][rows, :] = _rms(h3, gn_ref[...])
        else:
            outs[0][rows, :] = h3
            outs[1][rows, :] = _rms(h3, gn_ref[...]).astype(BF16)


def layer_tail(h, y, w_out, mixer_idx, g_mlp, w_up, w_down, p, g_ple, w_gate, w_proj, layer, g_next, *, tm, tf, final):
    m, d = h.shape
    k = y.shape[1]
    ff = w_up.shape[2]
    pd = p.shape[2]
    row = lambda i: (i, 0)
    fixed = lambda i: (0, 0)
    once = pl.Buffered(1)
    chains = LAYER_TAIL_CHAINS if tm % (LAYER_TAIL_CHAINS * LANES) == 0 else 1
    out_specs = [pl.BlockSpec((tm, d), row)]
    out_shape = [jax.ShapeDtypeStruct((m, d), F32)]
    if not final:
        out_specs.append(pl.BlockSpec((tm, d), row))
        out_shape.append(jax.ShapeDtypeStruct((m, d), BF16))
    outs = pl.pallas_call(
        functools.partial(_layer_tail_kernel, final=final, chains=chains, tf=tf),
        grid=(m // tm,),
        in_specs=[pl.BlockSpec((tm, d), row),
                  pl.BlockSpec((tm, k), row),
                  pl.BlockSpec((None, k, d), lambda i: (mixer_idx, 0, 0), pipeline_mode=once),
                  pl.BlockSpec((1, d), fixed),
                  pl.BlockSpec((None, d, ff), lambda i: (layer, 0, 0), pipeline_mode=once),
                  pl.BlockSpec((None, ff, d), lambda i: (layer, 0, 0), pipeline_mode=once),
                  pl.BlockSpec((None, tm, pd), lambda i: (layer, i, 0)),
                  pl.BlockSpec((1, d), fixed),
                  pl.BlockSpec((None, d, d), lambda i: (layer, 0, 0), pipeline_mode=once),
                  pl.BlockSpec((None, pd, d), lambda i: (layer, 0, 0), pipeline_mode=once),
                  pl.BlockSpec((1, d), fixed)],
        out_specs=out_specs,
        out_shape=out_shape,
        compiler_params=_cparams("parallel"),
        name="layer_tail",
    )(h, y, w_out, g_mlp.reshape(1, d), w_up, w_down, p, g_ple.reshape(1, d), w_gate, w_proj, g_next.reshape(1, d))
    return outs[0] if final else tuple(outs)


def _ssd_head_select():
    rows = jnp.arange(LANES)[:, None]
    cols = jnp.arange(SSD_HEADS * LANES)[None, :] // LANES
    return ((rows % SSD_HEADS == cols) & (rows < 3 * SSD_HEADS)).astype(BF16)


def _ssd_chunk_kernel(tbl_ref, z_ref, xbc_ref, dt_ref, cw_ref, cb_ref, dtb_ref, alog_ref, dskip_ref, ng_ref, sel_ref,
                      *refs):
    del tbl_ref
    n_pages = len(refs) - 10
    pages = refs[:n_pages]
    y_ref, conv_ref, state_ref, ksum_ref, xp_ref, act_ref, acol_ref, st_ref, xbd_ref, sbd_ref = refs[n_pages:]
    c = pl.program_id(1)
    t = SSD_CHUNK
    gw = SSD_GROUP_W
    hpg = SSD_HEADS_PER_GROUP
    hd = SSD_HEAD_DIM

    @pl.when(c == 0)
    def _():
        xp_ref[:, 0:SUBLANES, :] = jnp.zeros((xp_ref.shape[0], SUBLANES, LANES), F32)
        st_ref[...] = jnp.zeros_like(st_ref)
        xbd_ref[...] = jnp.zeros_like(xbd_ref)
        sbd_ref[...] = jnp.zeros_like(sbd_ref)

    for j in range(SSD_CONV_DIM // LANES):
        cols = slice(j * LANES, (j + 1) * LANES)
        xp_ref[j, SUBLANES:SUBLANES + t, :] = xbc_ref[:, cols]
        acc = xp_ref[j, SUBLANES - 3:SUBLANES - 3 + t, :] * cw_ref[0:1, cols]
        for k in range(1, CONV_W):
            acc = acc + xp_ref[j, SUBLANES - 3 + k:SUBLANES - 3 + k + t, :] * cw_ref[k:k + 1, cols]
        act_ref[:, cols] = _silu(acc + cb_ref[:, cols])
        xp_ref[j, 0:SUBLANES, :] = xbc_ref[t - SUBLANES:t, cols]
    conv_ref[0] = xbc_ref[t - SUBLANES:t, :]

    def row_sum(x):
        n = x.shape[0]
        while n > 1:
            n //= 2
            x = x[:n] + x[n:]
        return x[0]

    for k in range(n_pages // PAGES_PER_BLOCK):
        s = row_sum(pages[PAGES_PER_BLOCK * k][...])
        for j in range(1, PAGES_PER_BLOCK):
            s = s + row_sum(pages[PAGES_PER_BLOCK * k + j][...])
        ksum_ref[0, k] = s

    dtv = _softplus(dt_ref[...] + dtb_ref[...])
    da = dtv * (-jnp.exp(alog_ref[...]))
    row = lax.broadcasted_iota(jnp.int32, (t, t), 0)
    col = lax.broadcasted_iota(jnp.int32, (t, t), 1)
    causal = row >= col
    tril = causal.astype(BF16)
    d_hi, d_mid, d_lo = _split3(da)
    a2 = (_dot(tril, d_hi) + (_dot(tril, d_mid) + _dot(tril, d_lo))) * LOG2E
    arow_t = (a2 - jnp.log(dtv) * LOG2E).T

    lane = lax.broadcasted_iota(jnp.int32, (t, LANES), 1)
    a_hi, a_mid, a_lo = _split3(jnp.where(lane < SSD_HEADS, a2, 0.0))
    packed = (a_hi.astype(F32) + pltpu.roll(a_mid.astype(F32), SSD_HEADS, 1)
              + pltpu.roll(a_lo.astype(F32), 2 * SSD_HEADS, 1)).astype(BF16)
    acol_ref[...] = _dot(packed, sel_ref[...])

    lane_g = lax.broadcasted_iota(jnp.int32, (1, gw), 1) // hd
    for g in range(SSD_GROUPS):
        xs_g = act_ref[:, g * gw:(g + 1) * gw]
        b_g = act_ref[:, SSD_D_INNER + g * SSD_STATE:SSD_D_INNER + (g + 1) * SSD_STATE]
        c_g = act_ref[:, SSD_D_INNER + SSD_BC_DIM + g * SSD_STATE:
                      SSD_D_INNER + SSD_BC_DIM + (g + 1) * SSD_STATE]
        cb = jnp.where(causal, _dot_nt(c_g.astype(BF16), b_g.astype(BF16)), 0.0)
        b_gt = b_g.T
        st_g = st_ref[g]
        xs_b = xs_g.astype(BF16)
        for r in range(hpg):
            xbd_ref[g, r * t:(r + 1) * t, r * hd:(r + 1) * hd] = xs_b[:, r * hd:(r + 1) * hd]
        m_parts, d_parts, w_parts = [], [], []
        dec_g = jnp.zeros((1, gw), F32)
        for r in range(hpg):
            h = g * hpg + r
            acol = acol_ref[:, h * LANES:(h + 1) * LANES]
            arow = arow_t[h:h + 1, :]
            aend = acol[t - 1:t, :]
            m_parts.append((cb * jnp.exp2(jnp.minimum(acol - arow, EXP2_CAP))).astype(BF16))
            d_parts.append((c_g * jnp.exp2(acol)).astype(BF16))
            w_parts.append((b_gt * jnp.exp2(aend - arow)).astype(BF16))
            dec_g = jnp.where(lane_g == r, jnp.exp2(jnp.concatenate([aend, aend], axis=1)), dec_g)
        x_bd = xbd_ref[g]
        y_g = (_dot(jnp.concatenate(m_parts, axis=1), x_bd) + _dot(jnp.concatenate(d_parts, axis=1), sbd_ref[g])
               + xs_g * dskip_ref[:, g * gw:(g + 1) * gw])
        st_new = st_g * dec_g + _dot(jnp.concatenate(w_parts, axis=1), x_bd)
        st_ref[g] = st_new
        st_b = st_new.astype(BF16)
        for r in range(hpg):
            sbd_ref[g, r * SSD_STATE:(r + 1) * SSD_STATE, r * hd:(r + 1) * hd] = st_b[:, r * hd:(r + 1) * hd]
        yz = y_g * _silu(z_ref[:, g * gw:(g + 1) * gw])
        y_ref[:, g * gw:(g + 1) * gw] = _rms(yz, ng_ref[:, g * gw:(g + 1) * gw]).astype(y_ref.dtype)

    @pl.when(c == pl.num_programs(1) - 1)
    def _():
        for g in range(SSD_GROUPS):
            state_ref[0, g * hpg:(g + 1) * hpg] = st_ref[g].T.reshape(hpg, SSD_HEAD_DIM, SSD_STATE)


def ssd_prompt(z, xbc, dt, conv_w, conv_b, dt_bias, a_log, d_skip, norm_g, batch, seq, pool, pool_layer, table):
    t = SSD_CHUNK
    nc = seq // t
    pad = LANES - SSD_HEADS
    n_seqs, n_pages = table.shape
    pps = (n_seqs * n_pages) // (batch * nc)
    assert pps * batch * nc == n_seqs * n_pages and pps % PAGES_PER_BLOCK == 0 and n_pages % pps == 0
    gps = n_pages // pps
    page_shape = pool.shape[2:]
    rowblk = lambda b, c, tbl: (b * nc + c, 0)
    fixed = lambda b, c, tbl: (0, 0)

    def page_spec(k):
        def index(b, c, tbl):
            step = b * nc + c
            return (pool_layer, tbl[(step // gps) * n_pages + (step % gps) * pps + k], 0, 0, 0)
        return pl.BlockSpec((None, None) + page_shape, index)

    y, conv_tail, state, key_sums = pl.pallas_call(
        _ssd_chunk_kernel,
        grid_spec=pltpu.PrefetchScalarGridSpec(
            num_scalar_prefetch=1,
            grid=(batch, nc),
            in_specs=[pl.BlockSpec((t, SSD_D_INNER), rowblk),
                      pl.BlockSpec((t, SSD_CONV_DIM), rowblk),
                      pl.BlockSpec((t, LANES), rowblk),
                      pl.BlockSpec((CONV_W, SSD_CONV_DIM), fixed),
                      pl.BlockSpec((1, SSD_CONV_DIM), fixed),
                      pl.BlockSpec((1, LANES), fixed),
                      pl.BlockSpec((1, LANES), fixed),
                      pl.BlockSpec((1, SSD_D_INNER), fixed),
                      pl.BlockSpec((1, SSD_D_INNER), fixed),
                      pl.BlockSpec((LANES, SSD_HEADS * LANES), fixed)]
                     + [page_spec(k) for k in range(pps)],
            out_specs=[pl.BlockSpec((t, SSD_D_INNER), rowblk),
                       pl.BlockSpec((1, SUBLANES, SSD_CONV_DIM), lambda b, c, tbl: (b, 0, 0)),
                       pl.BlockSpec((1, SSD_HEADS, SSD_HEAD_DIM, SSD_STATE), lambda b, c, tbl: (b, 0, 0, 0)),
                       pl.BlockSpec((1, pps // PAGES_PER_BLOCK) + page_shape[1:],
                                    lambda b, c, tbl: ((b * nc + c) // gps, (b * nc + c) % gps, 0, 0))],
            scratch_shapes=[pltpu.VMEM((SSD_CONV_DIM // LANES, SUBLANES + t, LANES), F32),
                            pltpu.VMEM((t, SSD_CONV_DIM), F32),
                            pltpu.VMEM((t, SSD_HEADS * LANES), F32),
                            pltpu.VMEM((SSD_GROUPS, SSD_STATE, SSD_GROUP_W), F32),
                            pltpu.VMEM((SSD_GROUPS, SSD_HEADS_PER_GROUP * t, SSD_GROUP_W), BF16),
                            pltpu.VMEM((SSD_GROUPS, SSD_HEADS_PER_GROUP * SSD_STATE, SSD_GROUP_W), BF16)]),
        out_shape=[jax.ShapeDtypeStruct((batch * seq, SSD_D_INNER), BF16),
                   jax.ShapeDtypeStruct((batch, SUBLANES, SSD_CONV_DIM), F32),
                   jax.ShapeDtypeStruct((batch, SSD_HEADS, SSD_HEAD_DIM, SSD_STATE), F32),
                   jax.ShapeDtypeStruct((n_seqs, n_pages // PAGES_PER_BLOCK) + page_shape[1:], F32)],
        compiler_params=_cparams("parallel", "arbitrary"),
        name="ssd_prompt",
    )(table.reshape(-1), z, xbc, dt, conv_w, conv_b.reshape(1, -1), jnp.pad(dt_bias, (0, pad)).reshape(1, LANES),
      jnp.pad(a_log, (0, pad)).reshape(1, LANES), jnp.repeat(d_skip, SSD_HEAD_DIM).reshape(1, -1),
      norm_g.reshape(1, -1), _ssd_head_select(), *([pool] * pps))
    return y, conv_tail[:, SUBLANES - (CONV_W - 1):, :], state, key_sums


MOBA_HEADS_PER_STEP = 4


def _col_reduce(x, pair_op, reduce_fn):
    n = x.shape[0]
    while n > 8 * SUBLANES and n % 2 == 0:
        n //= 2
        x = pair_op(x[:n], x[n:])
    return reduce_fn(x, axis=0, keepdims=True)


def _moba_prompt_kernel(q_ref, k_ref, v_ref, o_ref, kb_ref, vt_ref, sel_ref, s_ref, p_ref):
    qi = pl.program_id(2)
    blk = MOBA_BLOCK
    seq = k_ref.shape[0]
    nb = seq // blk
    nbp = sel_ref.shape[2]
    hd = MOBA_HEAD_DIM
    heads = q_ref.shape[1] // hd

    def prepare(hh):
        cols = slice(hh * hd, (hh + 1) * hd)
        means = []
        for j in range(nb):
            kj = k_ref[j * blk:(j + 1) * blk, cols]
            means.append(jnp.sum(kj, axis=0, keepdims=True) * (1.0 / blk))
            kb_ref[hh, j * blk:(j + 1) * blk, :] = kj.astype(BF16)
        for j in range(seq // LANES):
            vt_ref[hh, :, j * LANES:(j + 1) * LANES] = v_ref[j * LANES:(j + 1) * LANES, cols].T.astype(BF16)
        if nbp > nb:
            means.append(jnp.zeros((nbp - nb, hd), F32))
        kmean = jnp.concatenate(means, axis=0)
        sub = lax.broadcasted_iota(jnp.int32, (nbp, blk), 0)
        for i in range(nb):
            gm = jnp.where(sub < i, _dot3_nt(kmean, q_ref[i * blk:(i + 1) * blk, cols]), -jnp.inf)
            rank = jnp.zeros((nbp, blk), F32)
            for m in range(i):
                gm_m = gm[m:m + 1, :]
                beats = (gm_m > gm) | ((gm_m == gm) & (sub > m))
                rank = rank + jnp.where(beats, 1.0, 0.0)
            chosen = ((sub < i) & (rank < MOBA_TOPK)) | (sub == i)
            sel_ref[hh, i] = jnp.where(chosen, 1.0, 0.0)

    @pl.when(qi == 0)
    def _():
        for hh in range(heads):
            prepare(hh)

    causal = (lax.broadcasted_iota(jnp.int32, (blk, blk), 0)
              <= lax.broadcasted_iota(jnp.int32, (blk, blk), 1))

    def attend_head(hh, i):
        cols = slice(hh * hd, (hh + 1) * hd)
        qb = (q_ref[i * blk:(i + 1) * blk, cols] * (hd ** -0.5)).astype(BF16)
        sel = sel_ref[hh, i]
        n_keys = (i + 1) * blk
        m = None
        for j in range(i + 1):
            rows = slice(j * blk, (j + 1) * blk)
            s = _dot_nt(kb_ref[hh, rows, :], qb)
            if j < i:
                s = s + jnp.where(sel[j:j + 1, :] > 0.0, 0.0, NEG_BIG)
            else:
                s = jnp.where(causal, s, NEG_BIG)
            s_ref[hh, rows, :] = s
            bm = _col_reduce(s, jnp.maximum, jnp.max)
            m = bm if m is None else jnp.maximum(m, bm)
        l = jnp.zeros((1, blk), F32)
        for j in range(i + 1):
            rows = slice(j * blk, (j + 1) * blk)
            p = jnp.exp(s_ref[hh, rows, :] - m)
            l = l + _col_reduce(p, jnp.add, jnp.sum)
            p_ref[hh, rows, :] = p.astype(BF16)
        acc = _dot(vt_ref[hh, :, 0:n_keys], p_ref[hh, 0:n_keys, :])
        o_ref[:, cols] = (acc / l).T.astype(o_ref.dtype)

    def attend(i):
        for hh in range(heads):
            attend_head(hh, i)

    for i in range(nb):
        pl.when(qi == i)(functools.partial(attend, i))


def moba_prompt(q, k, v, batch, seq):
    nb = seq // MOBA_BLOCK
    hd = MOBA_HEAD_DIM
    hps = MOBA_HEADS_PER_STEP
    wide = lambda b, h, i: (b, h)
    return pl.pallas_call(
        _moba_prompt_kernel,
        grid=(batch, MOBA_HEADS // hps, nb),
        in_specs=[pl.BlockSpec((seq, hps * hd), wide),
                  pl.BlockSpec((seq, hps * hd), wide),
                  pl.BlockSpec((seq, hps * hd), wide)],
        out_specs=pl.BlockSpec((MOBA_BLOCK, hps * hd), lambda b, h, i: (b * nb + i, h)),
        out_shape=jax.ShapeDtypeStruct((batch * seq, D_MODEL), BF16),
        scratch_shapes=[pltpu.VMEM((hps, seq, hd), BF16),
                        pltpu.VMEM((hps, hd, seq), BF16),
                        pltpu.VMEM((hps, nb, -(-nb // SUBLANES) * SUBLANES, MOBA_BLOCK), F32),
                        pltpu.VMEM((hps, seq, MOBA_BLOCK), F32),
                        pltpu.VMEM((hps, seq, MOBA_BLOCK), BF16)],
        compiler_params=_cparams("parallel", "parallel", "arbitrary"),
        name="moba_prompt",
    )(q, k, v)


def _lru_gates(xc, wa, ba, wx, bx, lam):
    xb = xc.astype(BF16)
    r = _sigmoid(_dot(xb, wa) + ba)
    i = _sigmoid(_dot(xb, wx) + bx)
    log_a = (-LRU_C) * r * _softplus(-lam)
    a = jnp.exp(log_a)
    u = jnp.sqrt(-jnp.tanh(log_a) * (a * a + 1.0)) * (i * xc)
    return a, u


def _lru_chunk_kernel(gate_ref, x_ref, cw_ref, cb_ref, wa_ref, ba_ref, wx_ref, bx_ref, lam_ref,
                      y_ref, conv_ref, hlast_ref, xp_ref, a_ref, u_ref, h_ref):
    c = pl.program_id(1)
    t = LRU_CHUNK
    w = LRU_BLOCK_W

    @pl.when(c == 0)
    def _():
        xp_ref[:, 0:SUBLANES, :] = jnp.zeros((LRU_BLOCKS, SUBLANES, w), F32)
        h_ref[...] = jnp.zeros_like(h_ref)

    for n in range(LRU_BLOCKS):
        cols = slice(n * w, (n + 1) * w)
        xp_ref[n, SUBLANES:SUBLANES + t, :] = x_ref[:, cols]
        acc = xp_ref[n, SUBLANES - 3:SUBLANES - 3 + t, :] * cw_ref[0:1, cols]
        for k in range(1, CONV_W):
            acc = acc + xp_ref[n, SUBLANES - 3 + k:SUBLANES - 3 + k + t, :] * cw_ref[k:k + 1, cols]
        xc = acc + cb_ref[:, cols]
        a, u = _lru_gates(xc, wa_ref[n], ba_ref[:, cols], wx_ref[n], bx_ref[:, cols], lam_ref[:, cols])
        a_ref[:, cols] = a
        u_ref[:, cols] = u
        xp_ref[n, 0:SUBLANES, :] = x_ref[t - SUBLANES:t, cols]
    conv_ref[0] = x_ref[t - SUBLANES:t, :]

    srow = lax.broadcasted_iota(jnp.int32, (SUBLANES, LRU_WIDTH), 0)

    def tile(i, h):
        rows = pl.ds(pl.multiple_of(i * SUBLANES, SUBLANES), SUBLANES)
        a = a_ref[rows, :]
        u = u_ref[rows, :]
        for d in (1, 2, 4):
            keep = srow >= d
            a_sh = jnp.where(keep, pltpu.roll(a, d, 0), 1.0)
            u_sh = jnp.where(keep, pltpu.roll(u, d, 0), 0.0)
            u = a * u_sh + u
            a = a * a_sh
        hs = a * h + u
        u_ref[rows, :] = hs
        return jnp.broadcast_to(hs[SUBLANES - 1:SUBLANES, :], (SUBLANES, LRU_WIDTH))

    h = lax.fori_loop(0, t // SUBLANES, tile, h_ref[...])
    h_ref[...] = h
    hlast_ref[0] = h
    y_ref[...] = (_gelu_tanh(gate_ref[...]) * u_ref[...]).astype(y_ref.dtype)


def lru_prompt(gate_br, x_br, conv_w, conv_b, w_a, b_a, w_x, b_x, lam, batch, seq):
    t = LRU_CHUNK
    nc = seq // t
    wd = LRU_WIDTH
    rowblk = lambda b, c: (b * nc + c, 0)
    fixed = lambda b, c: (0, 0)
    fixed3 = lambda b, c: (0, 0, 0)
    perb = lambda b, c: (b, 0, 0)
    y, conv_tail, h_last = pl.pallas_call(
        _lru_chunk_kernel,
        grid=(batch, nc),
        in_specs=[pl.BlockSpec((t, wd), rowblk),
                  pl.BlockSpec((t, wd), rowblk),
                  pl.BlockSpec((CONV_W, wd), fixed),
                  pl.BlockSpec((1, wd), fixed),
                  pl.BlockSpec((LRU_BLOCKS, LRU_BLOCK_W, LRU_BLOCK_W), fixed3),
                  pl.BlockSpec((1, wd), fixed),
                  pl.BlockSpec((LRU_BLOCKS, LRU_BLOCK_W, LRU_BLOCK_W), fixed3),
                  pl.BlockSpec((1, wd), fixed),
                  pl.BlockSpec((1, wd), fixed)],
        out_specs=[pl.BlockSpec((t, wd), rowblk),
                   pl.BlockSpec((1, SUBLANES, wd), perb),
                   pl.BlockSpec((1, SUBLANES, wd), perb)],
        out_shape=[jax.ShapeDtypeStruct((batch * seq, wd), BF16),
                   jax.ShapeDtypeStruct((batch, SUBLANES, wd), F32),
                   jax.ShapeDtypeStruct((batch, SUBLANES, wd), F32)],
        scratch_shapes=[pltpu.VMEM((LRU_BLOCKS, SUBLANES + t, LRU_BLOCK_W), F32),
                        pltpu.VMEM((t, wd), F32),
                        pltpu.VMEM((t, wd), F32),
                        pltpu.VMEM((SUBLANES, wd), F32)],
        compiler_params=_cparams("parallel", "arbitrary"),
        name="lru_prompt",
    )(gate_br, x_br, conv_w, conv_b.reshape(1, wd), w_a.astype(BF16), b_a.reshape(1, wd),
      w_x.astype(BF16), b_x.reshape(1, wd), lam.reshape(1, wd))
    return y, conv_tail[:, SUBLANES - (CONV_W - 1):, :], h_last[:, 0, :]


def _ssd_step_kernel(z_ref, xbc_ref, dt_ref, cs_ref, st_ref, cw_ref, cb_ref, dtb_ref, alog_ref, dskip_ref, ng_ref,
                     y_ref, cso_ref, sto_ref):
    gw = SSD_GROUP_W
    hpg = SSD_HEADS_PER_GROUP
    x = xbc_ref[0]
    buf = cs_ref[0]
    conv = x * cw_ref[CONV_W - 1:CONV_W, :]
    for k in range(CONV_W - 1):
        conv = conv + buf[k:k + 1, :] * cw_ref[k:k + 1, :]
    act = _silu(conv + cb_ref[...])
    cso_ref[0, 0:CONV_W - 2, :] = buf[1:CONV_W - 1, :]
    cso_ref[0, CONV_W - 2:CONV_W - 1, :] = x

    dtv = _softplus(dt_ref[0] + dtb_ref[...])
    dec = jnp.exp(dtv * (-jnp.exp(alog_ref[...])))
    lane_g = lax.broadcasted_iota(jnp.int32, (1, gw), 1) // SSD_HEAD_DIM
    first_row = lax.broadcasted_iota(jnp.int32, (SUBLANES, gw), 0) == 0
    ys = []
    for g in range(SSD_GROUPS):
        xs_g = act[:, g * gw:(g + 1) * gw]
        b_g = act[:, SSD_D_INNER + g * SSD_STATE:SSD_D_INNER + (g + 1) * SSD_STATE]
        c_g = act[:, SSD_D_INNER + SSD_BC_DIM + g * SSD_STATE:SSD_D_INNER + SSD_BC_DIM + (g + 1) * SSD_STATE]
        dt_e = jnp.zeros((1, gw), F32)
        dec_rows = []
        for r in range(hpg):
            h = g * hpg + r
            dt_e = jnp.where(lane_g == r, dtv[:, h:h + 1], dt_e)
            dec_rows.append(jnp.broadcast_to(dec[:, h:h + 1], (SSD_HEAD_DIM, SSD_STATE)))
        xdt = xs_g * dt_e
        x8 = jnp.where(first_row, jnp.broadcast_to(xdt, (SUBLANES, gw)), 0.0).astype(BF16)
        b8 = jnp.broadcast_to(b_g, (SUBLANES, SSD_STATE)).astype(BF16)
        outer = lax.dot_general(x8, b8, (((0,), (0,)), ((), ())), preferred_element_type=F32)
        old = st_ref[0, g * hpg:(g + 1) * hpg].reshape(gw, SSD_STATE)
        new = old * jnp.concatenate(dec_rows, axis=0) + outer
        sto_ref[0, g * hpg:(g + 1) * hpg] = new.reshape(hpg, SSD_HEAD_DIM, SSD_STATE)
        c8 = jnp.broadcast_to(c_g, (SUBLANES, SSD_STATE)).astype(BF16)
        y_g = _dot_nt(c8, new.astype(BF16))[0:1, :] + xs_g * dskip_ref[:, g * gw:(g + 1) * gw]
        yz = y_g * _silu(z_ref[0][:, g * gw:(g + 1) * gw])
        ys.append(_rms(yz, ng_ref[:, g * gw:(g + 1) * gw]))
    y_ref[0] = jnp.concatenate(ys, axis=1)


def ssd_step(z, xbc, dt, conv_state, ssm_state, layer, conv_w, conv_b, dt_bias, a_log, d_skip, norm_g):
    nb = z.shape[0]
    pad = LANES - SSD_HEADS
    per3 = lambda b: (b, 0, 0)
    fixed = lambda b: (0, 0)
    y, conv_new, state_new = pl.pallas_call(
        _ssd_step_kernel,
        grid=(nb,),
        in_specs=[pl.BlockSpec((1, 1, SSD_D_INNER), per3),
                  pl.BlockSpec((1, 1, SSD_CONV_DIM), per3),
                  pl.BlockSpec((1, 1, LANES), per3),
                  pl.BlockSpec((None, 1, CONV_W - 1, SSD_CONV_DIM), lambda b: (layer, b, 0, 0)),
                  pl.BlockSpec((None, 1, SSD_HEADS, SSD_HEAD_DIM, SSD_STATE), lambda b: (layer, b, 0, 0, 0)),
                  pl.BlockSpec((CONV_W, SSD_CONV_DIM), fixed),
                  pl.BlockSpec((1, SSD_CONV_DIM), fixed),
                  pl.BlockSpec((1, LANES), fixed),
                  pl.BlockSpec((1, LANES), fixed),
                  pl.BlockSpec((1, SSD_D_INNER), fixed),
                  pl.BlockSpec((1, SSD_D_INNER), fixed)],
        out_specs=[pl.BlockSpec((1, 1, SSD_D_INNER), per3),
                   pl.BlockSpec((1, CONV_W - 1, SSD_CONV_DIM), per3),
                   pl.BlockSpec((1, SSD_HEADS, SSD_HEAD_DIM, SSD_STATE), lambda b: (b, 0, 0, 0))],
        out_shape=[jax.ShapeDtypeStruct((nb, 1, SSD_D_INNER), F32),
                   jax.ShapeDtypeStruct((nb, CONV_W - 1, SSD_CONV_DIM), F32),
                   jax.ShapeDtypeStruct((nb, SSD_HEADS, SSD_HEAD_DIM, SSD_STATE), F32)],
        compiler_params=_cparams("parallel"),
        name="ssd_step",
    )(z.reshape(nb, 1, -1), xbc.reshape(nb, 1, -1), dt.reshape(nb, 1, -1), conv_state, ssm_state,
      conv_w, conv_b.reshape(1, -1), jnp.pad(dt_bias, (0, pad)).reshape(1, LANES),
      jnp.pad(a_log, (0, pad)).reshape(1, LANES), jnp.repeat(d_skip, SSD_HEAD_DIM).reshape(1, -1),
      norm_g.reshape(1, -1))
    return y.reshape(nb, SSD_D_INNER), conv_new, state_new


def _lru_step_kernel(gate_ref, x_ref, cs_ref, h0_ref, cw_ref, cb_ref, wa_ref, ba_ref, wx_ref, bx_ref, lam_ref,
                     y_ref, cso_ref, ho_ref):
    w = LRU_BLOCK_W
    x = x_ref[...]
    xc = x * cw_ref[CONV_W - 1:CONV_W, :] + cb_ref[...]
    for k in range(CONV_W - 1):
        xc = xc + cs_ref[k] * cw_ref[k:k + 1, :]
    for k in range(CONV_W - 2):
        cso_ref[k] = cs_ref[k + 1]
    cso_ref[CONV_W - 2] = x
    for n in range(LRU_BLOCKS):
        cols = slice(n * w, (n + 1) * w)
        a, u = _lru_gates(xc[:, cols], wa_ref[n], ba_ref[:, cols], wx_ref[n], bx_ref[:, cols], lam_ref[:, cols])
        h = a * h0_ref[:, cols] + u
        ho_ref[:, cols] = h
        y_ref[:, cols] = _gelu_tanh(gate_ref[:, cols]) * h


def lru_step(gate_br, x_br, conv_state, h0, conv_w, conv_b, w_a, b_a, w_x, b_x, lam):
    nb, wd = x_br.shape
    return pl.pallas_call(
        _lru_step_kernel,
        out_shape=[jax.ShapeDtypeStruct((nb, wd), F32),
                   jax.ShapeDtypeStruct((CONV_W - 1, nb, wd), F32),
                   jax.ShapeDtypeStruct((nb, wd), F32)],
        compiler_params=pltpu.CompilerParams(vmem_limit_bytes=VMEM_LIMIT),
        name="lru_step",
    )(gate_br, x_br, conv_state, h0, conv_w, conv_b.reshape(1, wd), w_a.astype(BF16), b_a.reshape(1, wd),
      w_x.astype(BF16), b_x.reshape(1, wd), lam.reshape(1, wd))


def _moba_gate_kernel(q_ref, bs_ref, sel_ref):
    n_blocks = bs_ref.shape[1]
    q = q_ref[0]
    gates = [jnp.sum((bs_ref[0, n] * (1.0 / MOBA_BLOCK)) * q, axis=-1, keepdims=True) for n in range(n_blocks)]
    lane = lax.broadcasted_iota(jnp.int32, (MOBA_HEADS, LANES), 1)
    out = jnp.zeros((MOBA_HEADS, LANES), F32)
    for k in range(MOBA_TOPK):
        best = functools.reduce(jnp.maximum, gates)
        idx = jnp.full((MOBA_HEADS, 1), float(n_blocks), F32)
        for n in reversed(range(n_blocks)):
            idx = jnp.where(gates[n] == best, float(n), idx)
        out = jnp.where(lane == k, idx, out)
        gates = [jnp.where(idx == float(n), -jnp.inf, gates[n]) for n in range(n_blocks)]
    sel_ref[0] = out.astype(jnp.int32)


def moba_gate(q, block_sums):
    nb = q.shape[0]
    n_blocks = block_sums.shape[1]
    return pl.pallas_call(
        _moba_gate_kernel,
        grid=(nb,),
        in_specs=[pl.BlockSpec((1, MOBA_HEADS, MOBA_HEAD_DIM), lambda b: (b, 0, 0)),
                  pl.BlockSpec((1, n_blocks, MOBA_HEADS, MOBA_HEAD_DIM), lambda b: (b, 0, 0, 0))],
        out_specs=pl.BlockSpec((1, MOBA_HEADS, LANES), lambda b: (b, 0, 0)),
        out_shape=jax.ShapeDtypeStruct((nb, MOBA_HEADS, LANES), jnp.int32),
        compiler_params=_cparams("parallel"),
        name="moba_gate",
    )(q.reshape(nb, MOBA_HEADS, MOBA_HEAD_DIM), block_sums)


def _moba_decode_kernel(tbl_ref, sel_ref, q_ref, kn_ref, vn_ref, kpool_ref, vpool_ref, o_ref, kbuf, vbuf, sem,
                        *, layer, n_table_pages):
    b = pl.program_id(0)
    slot = lax.rem(b, 2)
    n_sel = MOBA_TOPK * PAGES_PER_BLOCK
    hd = MOBA_HEAD_DIM

    def page_copies(bb, sl):
        out = []
        for hh in range(MOBA_HEADS):
            for k in range(MOBA_TOPK):
                blk = sel_ref[(bb * MOBA_HEADS + hh) * MOBA_TOPK + k]
                for s in range(PAGES_PER_BLOCK):
                    page = tbl_ref[bb * n_table_pages + blk * PAGES_PER_BLOCK + s]
                    i = k * PAGES_PER_BLOCK + s
                    out.append(pltpu.make_async_copy(kpool_ref.at[layer, page, :, hh, :], kbuf.at[sl, hh, i],
                                                     sem.at[sl, 0, hh, i]))
                    out.append(pltpu.make_async_copy(vpool_ref.at[layer, page, :, hh, :], vbuf.at[sl, hh, i],
                                                     sem.at[sl, 1, hh, i]))
        return out

    def start_all(copies):
        for i, c in enumerate(copies):
            c.start(priority=i % 2)

    @pl.when(b == 0)
    def _():
        start_all(page_copies(b, 0))

    @pl.when(b + 1 < pl.num_programs(0))
    def _():
        start_all(page_copies(b + 1, 1 - slot))

    for c in page_copies(b, slot):
        c.wait()

    scale = hd ** -0.5
    outs = []
    for hh in range(MOBA_HEADS):
        cols = slice(hh * hd, (hh + 1) * hd)
        q = q_ref[0][:, cols]
        q8 = jnp.broadcast_to(q, (SUBLANES, hd)).astype(BF16)
        k_cat = kbuf[slot, hh].reshape(n_sel * PAGE_SIZE, hd).astype(BF16)
        v_cat = vbuf[slot, hh].reshape(n_sel * PAGE_SIZE, hd).astype(BF16)
        s = _dot_nt(q8, k_cat) * scale
        s_new = jnp.sum(q * kn_ref[0][:, cols], axis=1, keepdims=True) * scale
        m = jnp.maximum(jnp.max(s, axis=1, keepdims=True), s_new)
        p = jnp.exp(s - m)
        p_new = jnp.exp(s_new - m)
        denom = jnp.sum(p, axis=1, keepdims=True) + p_new
        o = (_dot(p.astype(BF16), v_cat) + p_new * vn_ref[0][:, cols]) / denom
        outs.append(o[0:1, :])
    o_ref[0] = jnp.concatenate(outs, axis=1)


def moba_decode(q, k_new, v_new, k_pool, v_pool, layer, table, sel):
    nb = q.shape[0]
    n_pages = table.shape[1]
    hd = MOBA_HEAD_DIM
    n_sel = MOBA_TOPK * PAGES_PER_BLOCK
    tok = pl.BlockSpec((1, 1, D_MODEL), lambda b, tbl, sl: (b, 0, 0))
    hbm = pl.BlockSpec(memory_space=pl.ANY)
    out = pl.pallas_call(
        functools.partial(_moba_decode_kernel, layer=layer, n_table_pages=n_pages),
        grid_spec=pltpu.PrefetchScalarGridSpec(
            num_scalar_prefetch=2,
            grid=(nb,),
            in_specs=[tok, tok, tok, hbm, hbm],
            out_specs=tok,
            scratch_shapes=[pltpu.VMEM((2, MOBA_HEADS, n_sel, PAGE_SIZE, hd), F32),
                            pltpu.VMEM((2, MOBA_HEADS, n_sel, PAGE_SIZE, hd), F32),
                            pltpu.SemaphoreType.DMA((2, 2, MOBA_HEADS, n_sel))]),
        out_shape=jax.ShapeDtypeStruct((nb, 1, D_MODEL), F32),
        compiler_params=_cparams("arbitrary"),
        name="moba_decode",
    )(table.reshape(-1), sel[:, :, :MOBA_TOPK].reshape(-1),
      q.reshape(nb, 1, D_MODEL), k_new.reshape(nb, 1, D_MODEL), v_new.reshape(nb, 1, D_MODEL), k_pool, v_pool)
    return out.reshape(nb, D_MODEL)


def _prep_weights(W):
    bf = lambda a: a.astype(BF16)
    w_in, w_qkv, w_lru = lax.optimization_barrier((bf(W["ssd_w_in"]), bf(W["moba_w_qkv"]), bf(W["lru_w_in"])))
    zx = SSD_D_INNER + SSD_CONV_DIM
    return dict(
        ssd_w_z=w_in[:, :, :SSD_D_INNER],
        ssd_w_xbc=w_in[:, :, SSD_D_INNER:zx],
        ssd_w_dt=jnp.pad(w_in[:, :, zx:], ((0, 0), (0, 0), (0, LANES - SSD_HEADS))),
        ssd_w_out=bf(W["ssd_w_out"]),
        moba_w_q=w_qkv[:, :, :D_MODEL],
        moba_w_k=w_qkv[:, :, D_MODEL:2 * D_MODEL],
        moba_w_v=w_qkv[:, :, 2 * D_MODEL:],
        moba_w_o=bf(W["moba_w_o"]),
        lru_w_gate=w_lru[:, :, :LRU_WIDTH],
        lru_w_x=w_lru[:, :, LRU_WIDTH:],
        lru_w_out=bf(W["lru_w_out"]),
        mlp_w_up=bf(W["mlp_w_up"]), mlp_w_down=bf(W["mlp_w_down"]),
        ple_w_proj=bf(W["ple_w_proj"]), ple_w_gate=bf(W["ple_w_gate"]),
    )


def _trunk(h, p, W, Wb, tiles, ssd_fn, moba_fn, lru_fn):
    tm, tf = tiles
    ssd_out, moba_out, lru_out = [], [], []
    xn = rmsnorm_bf16(h, W["norm_mix"][0], tm=tm)
    for i in range(DEPTH):
        j, kind = i // N_MIXERS, i % N_MIXERS
        proj = lambda *names: multi_linear(xn, [Wb[n] for n in names], j, tm=tm)
        if kind == 0:
            y, conv, state = ssd_fn(j, *proj("ssd_w_z", "ssd_w_xbc", "ssd_w_dt"))
            ssd_out.append((conv, state))
            w_out = Wb["ssd_w_out"]
        elif kind == 1:
            q, k, v = proj("moba_w_q", "moba_w_k", "moba_w_v")
            moba_out.append((k, v))
            y, w_out = moba_fn(j, q, k, v), Wb["moba_w_o"]
        else:
            y, conv, state = lru_fn(j, *proj("lru_w_gate", "lru_w_x"))
            lru_out.append((conv, state))
            w_out = Wb["lru_w_out"]
        final = i == DEPTH - 1
        g_next = W["norm_final"] if final else W["norm_mix"][i + 1]
        out = layer_tail(h, y, w_out, j, W["norm_mlp"][i], Wb["mlp_w_up"], Wb["mlp_w_down"], p,
                         W["norm_ple"][i], Wb["ple_w_gate"], Wb["ple_w_proj"], i, g_next,
                         tm=tm, tf=tf, final=final)
        h, xn = (out, None) if final else out
    stack = lambda pairs, n: jnp.stack([pr[n] for pr in pairs])
    return (h, stack(ssd_out, 0), stack(ssd_out, 1), stack(moba_out, 0), stack(moba_out, 1),
            stack(lru_out, 0), stack(lru_out, 1))


PROMPT_TILES = (512, 1024)
SAMPLE_TILES = (32, 1024)


def kernel(x_prompt, x_sample, state_ssd_conv, state_ssd, cache_k, cache_v, page_table, state_lru_conv,
           state_lru, p_prompt, p_sample, norm_mix, norm_mlp, norm_ple, norm_final, ssd_w_in, ssd_conv_w,
           ssd_conv_b, ssd_dt_bias, ssd_a_log, ssd_d, ssd_norm, ssd_w_out, moba_w_qkv, moba_w_o, lru_w_in,
           lru_conv_w, lru_conv_b, lru_w_a, lru_b_a, lru_w_x, lru_b_x, lru_lambda, lru_w_out, mlp_w_up,
           mlp_w_down, ple_w_proj, ple_w_gate):
    W = dict(norm_mix=norm_mix, norm_mlp=norm_mlp, norm_ple=norm_ple, norm_final=norm_final,
             ssd_w_in=ssd_w_in, ssd_w_out=ssd_w_out, moba_w_qkv=moba_w_qkv, moba_w_o=moba_w_o,
             lru_w_in=lru_w_in, lru_w_out=lru_w_out, mlp_w_up=mlp_w_up, mlp_w_down=mlp_w_down,
             ple_w_proj=ple_w_proj, ple_w_gate=ple_w_gate)
    Wb = _prep_weights(W)
    batch, seq, d = x_prompt.shape
    dec_batch, dec_seq, _ = x_sample.shape
    assert dec_seq == 1 and seq % MOBA_BLOCK == 0 and seq % LRU_CHUNK == 0

    ssd_args = lambda j: (ssd_conv_w[j], ssd_conv_b[j], ssd_dt_bias[j], ssd_a_log[j], ssd_d[j], ssd_norm[j])
    lru_args = lambda j: (lru_conv_w[j], lru_conv_b[j], lru_w_a[j], lru_b_a[j].reshape(-1), lru_w_x[j],
                          lru_b_x[j].reshape(-1), lru_lambda[j])

    n_ssd_calls = ssd_w_in.shape[0]
    assert cache_k.shape[0] == 1 and dec_batch % n_ssd_calls == 0
    share = dec_batch // n_ssd_calls
    key_sums = []

    def ssd_prompt_fn(j, z, xbc, dt):
        y, conv, state, sums = ssd_prompt(z, xbc, dt, *ssd_args(j), batch, seq,
                                          cache_k, 0, page_table[j * share:(j + 1) * share])
        key_sums.append(sums)
        return y, conv, state

    out_p = _trunk(
        x_prompt.reshape(batch * seq, d), p_prompt.reshape(DEPTH, batch * seq, PLE_DIM), W, Wb, PROMPT_TILES,
        ssd_prompt_fn,
        lambda j, q, k, v: moba_prompt(q, k, v, batch, seq),
        lambda j, gate, x: lru_prompt(gate, x, *lru_args(j), batch, seq))

    def moba_sample(j, q, k, v):
        sel = moba_gate(q, jnp.concatenate(key_sums, axis=0))
        return moba_decode(q, k, v, cache_k, cache_v, j, page_table, sel)

    def lru_sample(j, gate, x):
        y, conv, h = lru_step(gate, x, jnp.swapaxes(state_lru_conv[j], 0, 1), state_lru[j], *lru_args(j))
        return y, jnp.swapaxes(conv, 0, 1), h

    out_s = _trunk(
        x_sample.reshape(dec_batch, d), p_sample.reshape(DEPTH, dec_batch, PLE_DIM), W, Wb, SAMPLE_TILES,
        lambda j, z, xbc, dt: ssd_step(z, xbc, dt, state_ssd_conv, state_ssd, j, *ssd_args(j)),
        moba_sample, lru_sample)

    def shaped(out, b, s):
        y, ssd_conv, ssd_state, k, v, lru_conv, lru_state = out
        kv_shape = (-1, b, s, MOBA_HEADS, MOBA_HEAD_DIM)
        return (y.reshape(b, s, d), ssd_conv, ssd_state, k.reshape(kv_shape), v.reshape(kv_shape),
                lru_conv, lru_state)

    yp, *rest_p = shaped(out_p, batch, seq)
    ys, *rest_s = shaped(out_s, dec_batch, dec_seq)
    return (yp, ys, *rest_p, *rest_s)
```

```python
import functools
import math

import jax
import jax.numpy as jnp
from jax import lax
from jax.experimental import pallas as pl
from jax.experimental.pallas import tpu as pltpu

D_MODEL = 1024
DEPTH = 4
N_MIXERS = 3
PLE_DIM = 256
D_FF = 4 * D_MODEL
NORM_EPS = 1e-6
CONV_W = 4
PAGE_SIZE = 128

SSD_D_INNER = 2 * D_MODEL
SSD_HEAD_DIM = 64
SSD_HEADS = SSD_D_INNER // SSD_HEAD_DIM
SSD_GROUPS = 8
SSD_HEADS_PER_GROUP = SSD_HEADS // SSD_GROUPS
SSD_STATE = 128
SSD_BC_DIM = SSD_GROUPS * SSD_STATE
SSD_CONV_DIM = SSD_D_INNER + 2 * SSD_BC_DIM
SSD_GROUP_W = SSD_D_INNER // SSD_GROUPS
SSD_CHUNK = 128

MOBA_HEADS = 8
MOBA_HEAD_DIM = D_MODEL // MOBA_HEADS
MOBA_BLOCK = 256
MOBA_TOPK = 3
PAGES_PER_BLOCK = MOBA_BLOCK // PAGE_SIZE

LRU_WIDTH = D_MODEL
LRU_BLOCKS = 8
LRU_BLOCK_W = LRU_WIDTH // LRU_BLOCKS
LRU_C = 8.0
LRU_CHUNK = 256

LANES = 128
SUBLANES = 8
VMEM_LIMIT = 56 * 1024 * 1024
NEG_BIG = -1e30
LOG2E = 1.0 / math.log(2.0)
EXP2_CAP = 126.0

F32 = jnp.float32
BF16 = jnp.bfloat16


def _cparams(*sem):
    return pltpu.CompilerParams(dimension_semantics=sem, vmem_limit_bytes=VMEM_LIMIT)


def _rms(x, g):
    var = jnp.mean(x * x, axis=-1, keepdims=True)
    return (x * lax.rsqrt(var + NORM_EPS)) * g


def _sigmoid(x):
    return 0.5 + 0.5 * jnp.tanh(0.5 * x)


def _silu(x):
    hx = 0.5 * x
    return hx + hx * jnp.tanh(hx)


def _softplus(x):
    return jnp.maximum(x, 0.0) + jnp.log1p(jnp.exp(-jnp.abs(x)))


def _gelu_tanh(x):
    return 0.5 * x * (1.0 + jnp.tanh(math.sqrt(2.0 / math.pi) * (x + 0.044715 * (x * x * x))))


def _split3(x):
    hi = x.astype(BF16)
    r1 = x - hi.astype(F32)
    mid = r1.astype(BF16)
    lo = (r1 - mid.astype(F32)).astype(BF16)
    return hi, mid, lo


def _dot(a, b):
    return jnp.dot(a, b, preferred_element_type=F32)


def _dot_nt(a, b):
    return lax.dot_general(a, b, (((1,), (1,)), ((), ())), preferred_element_type=F32)


def _dot3_nt(a, b):
    ah, am, _ = _split3(a)
    bh, bm, _ = _split3(b)
    return _dot_nt(ah, bh) + (_dot_nt(ah, bm) + _dot_nt(am, bh))


def _multi_linear_kernel(x_ref, *refs, normed):
    x = x_ref[...]
    if normed:
        x = _rms(x, refs[0][...]).astype(BF16)
        refs = refs[1:]
    n = len(refs) // 2
    for w_ref, o_ref in zip(refs[:n], refs[n:]):
        o_ref[...] = _dot(x, w_ref[...])


def multi_linear(x, weights, layer, *, tm, norm_g=None):
    m, d = x.shape
    normed = norm_g is not None
    gain = [norm_g.reshape(1, d)] if normed else []
    return pl.pallas_call(
        functools.partial(_multi_linear_kernel, normed=normed),
        grid=(m // tm,),
        in_specs=[pl.BlockSpec((tm, d), lambda i: (i, 0))]
                 + [pl.BlockSpec((1, d), lambda i: (0, 0)) for _ in gain]
                 + [pl.BlockSpec((None,) + w.shape[1:], lambda i: (layer, 0, 0), pipeline_mode=pl.Buffered(1))
                    for w in weights],
        out_specs=[pl.BlockSpec((tm, w.shape[2]), lambda i: (i, 0)) for w in weights],
        out_shape=[jax.ShapeDtypeStruct((m, w.shape[2]), F32) for w in weights],
        compiler_params=_cparams("parallel"),
        name="multi_linear",
    )(x, *gain, *weights)


LAYER_TAIL_CHAINS = 2


def _layer_tail_kernel(h_ref, y_ref, wo_ref, gm_ref, wu_ref, wd_ref, p_ref, gp_ref, wg_ref, wp_ref, gn_ref,
                       *outs, final, chains, tf):
    rows_per = h_ref.shape[0] // chains
    ff = wu_ref.shape[1]
    for c in range(chains):
        rows = slice(c * rows_per, (c + 1) * rows_per)
        h1 = h_ref[rows, :] + _dot(y_ref[rows, :].astype(BF16), wo_ref[...])
        xn = _rms(h1, gm_ref[...]).astype(BF16)
        acc = None
        for f in range(ff // tf):
            hid = jnp.square(jnp.maximum(_dot(xn, wu_ref[:, f * tf:(f + 1) * tf]), 0.0))
            part = _dot(hid.astype(BF16), wd_ref[f * tf:(f + 1) * tf, :])
            acc = part if acc is None else acc + part
        h2 = h1 + acc
        gate = _sigmoid(_dot(_rms(h2, gp_ref[...]).astype(BF16), wg_ref[...]))
        h3 = h2 + _dot(p_ref[rows, :].astype(BF16), wp_ref[...]) * gate
        if final:
            outs[0][rows, :] = _rms(h3, gn_ref[...])
        else:
            outs[0][rows, :] = h3
            outs[1][rows, :] = _rms(h3, gn_ref[...]).astype(BF16)


def layer_tail(h, y, w_out, mixer_idx, g_mlp, w_up, w_down, p, g_ple, w_gate, w_proj, layer, g_next, *, tm, tf, final):
    m, d = h.shape
    k = y.shape[1]
    ff = w_up.shape[2]
    pd = p.shape[2]
    row = lambda i: (i, 0)
    fixed = lambda i: (0, 0)
    once = pl.Buffered(1)
    chains = LAYER_TAIL_CHAINS if tm % (LAYER_TAIL_CHAINS * LANES) == 0 else 1
    out_specs = [pl.BlockSpec((tm, d), row)]
    out_shape = [jax.ShapeDtypeStruct((m, d), F32)]
    if not final:
        out_specs.append(pl.BlockSpec((tm, d), row))
        out_shape.append(jax.ShapeDtypeStruct((m, d), BF16))
    outs = pl.pallas_call(
        functools.partial(_layer_tail_kernel, final=final, chains=chains, tf=tf),
        grid=(m // tm,),
        in_specs=[pl.BlockSpec((tm, d), row),
                  pl.BlockSpec((tm, k), row),
                  pl.BlockSpec((None, k, d), lambda i: (mixer_idx, 0, 0), pipeline_mode=once),
                  pl.BlockSpec((1, d), fixed),
                  pl.BlockSpec((None, d, ff), lambda i: (layer, 0, 0), pipeline_mode=once),
                  pl.BlockSpec((None, ff, d), lambda i: (layer, 0, 0), pipeline_mode=once),
                  pl.BlockSpec((None, tm, pd), lambda i: (layer, i, 0)),
                  pl.BlockSpec((1, d), fixed),
                  pl.BlockSpec((None, d, d), lambda i: (layer, 0, 0), pipeline_mode=once),
                  pl.BlockSpec((None, pd, d), lambda i: (layer, 0, 0), pipeline_mode=once),
                  pl.BlockSpec((1, d), fixed)],
        out_specs=out_specs,
        out_shape=out_shape,
        compiler_params=_cparams("parallel"),
        name="layer_tail",
    )(h, y, w_out, g_mlp.reshape(1, d), w_up, w_down, p, g_ple.reshape(1, d), w_gate, w_proj, g_next.reshape(1, d))
    return outs[0] if final else tuple(outs)


def _ssd_head_select():
    rows = jnp.arange(LANES)[:, None]
    cols = jnp.arange(SSD_HEADS * LANES)[None, :] // LANES
    return ((rows % SSD_HEADS == cols) & (rows < 3 * SSD_HEADS)).astype(BF16)


def _ssd_chunk_kernel(tbl_ref, z_ref, xbc_ref, dt_ref, cw_ref, cb_ref, dtb_ref, alog_ref, dskip_ref, ng_ref, sel_ref,
                      *refs):
    del tbl_ref
    n_pages = len(refs) - 10
    pages = refs[:n_pages]
    y_ref, conv_ref, state_ref, ksum_ref, xp_ref, act_ref, acol_ref, st_ref, xbd_ref, sbd_ref = refs[n_pages:]
    c = pl.program_id(1)
    t = SSD_CHUNK
    gw = SSD_GROUP_W
    hpg = SSD_HEADS_PER_GROUP
    hd = SSD_HEAD_DIM

    @pl.when(c == 0)
    def _():
        xp_ref[:, 0:SUBLANES, :] = jnp.zeros((xp_ref.shape[0], SUBLANES, LANES), F32)
        st_ref[...] = jnp.zeros_like(st_ref)
        xbd_ref[...] = jnp.zeros_like(xbd_ref)
        sbd_ref[...] = jnp.zeros_like(sbd_ref)

    for j in range(SSD_CONV_DIM // LANES):
        cols = slice(j * LANES, (j + 1) * LANES)
        xp_ref[j, SUBLANES:SUBLANES + t, :] = xbc_ref[:, cols]
        acc = xp_ref[j, SUBLANES - 3:SUBLANES - 3 + t, :] * cw_ref[0:1, cols]
        for k in range(1, CONV_W):
            acc = acc + xp_ref[j, SUBLANES - 3 + k:SUBLANES - 3 + k + t, :] * cw_ref[k:k + 1, cols]
        act_ref[:, cols] = _silu(acc + cb_ref[:, cols])
        xp_ref[j, 0:SUBLANES, :] = xbc_ref[t - SUBLANES:t, cols]
    conv_ref[0] = xbc_ref[t - SUBLANES:t, :]

    def row_sum(x):
        n = x.shape[0]
        while n > 1:
            n //= 2
            x = x[:n] + x[n:]
        return x[0]

    for k in range(n_pages // PAGES_PER_BLOCK):
        s = row_sum(pages[PAGES_PER_BLOCK * k][...])
        for j in range(1, PAGES_PER_BLOCK):
            s = s + row_sum(pages[PAGES_PER_BLOCK * k + j][...])
        ksum_ref[0, k] = s

    dtv = _softplus(dt_ref[...] + dtb_ref[...])
    da = dtv * (-jnp.exp(alog_ref[...]))
    row = lax.broadcasted_iota(jnp.int32, (t, t), 0)
    col = lax.broadcasted_iota(jnp.int32, (t, t), 1)
    causal = row >= col
    tril = causal.astype(BF16)
    d_hi, d_mid, d_lo = _split3(da)
    a2 = (_dot(tril, d_hi) + (_dot(tril, d_mid) + _dot(tril, d_lo))) * LOG2E
    arow_t = (a2 - jnp.log(dtv) * LOG2E).T

    lane = lax.broadcasted_iota(jnp.int32, (t, LANES), 1)
    a_hi, a_mid, a_lo = _split3(jnp.where(lane < SSD_HEADS, a2, 0.0))
    packed = (a_hi.astype(F32) + pltpu.roll(a_mid.astype(F32), SSD_HEADS, 1)
              + pltpu.roll(a_lo.astype(F32), 2 * SSD_HEADS, 1)).astype(BF16)
    acol_ref[...] = _dot(packed, sel_ref[...])

    lane_g = lax.broadcasted_iota(jnp.int32, (1, gw), 1) // hd
    for g in range(SSD_GROUPS):
        xs_g = act_ref[:, g * gw:(g + 1) * gw]
        b_g = act_ref[:, SSD_D_INNER + g * SSD_STATE:SSD_D_INNER + (g + 1) * SSD_STATE]
        c_g = act_ref[:, SSD_D_INNER + SSD_BC_DIM + g * SSD_STATE:
                      SSD_D_INNER + SSD_BC_DIM + (g + 1) * SSD_STATE]
        cb = jnp.where(causal, _dot_nt(c_g.astype(BF16), b_g.astype(BF16)), 0.0)
        b_gt = b_g.T
        st_g = st_ref[g]
        xs_b = xs_g.astype(BF16)
        for r in range(hpg):
            xbd_ref[g, r * t:(r + 1) * t, r * hd:(r + 1) * hd] = xs_b[:, r * hd:(r + 1) * hd]
        m_parts, d_parts, w_parts = [], [], []
        dec_g = jnp.zeros((1, gw), F32)
        for r in range(hpg):
            h = g * hpg + r
            acol = acol_ref[:, h * LANES:(h + 1) * LANES]
            arow = arow_t[h:h + 1, :]
            aend = acol[t - 1:t, :]
            m_parts.append((cb * jnp.exp2(jnp.minimum(acol - arow, EXP2_CAP))).astype(BF16))
            d_parts.append((c_g * jnp.exp2(acol)).astype(BF16))
            w_parts.append((b_gt * jnp.exp2(aend - arow)).astype(BF16))
            dec_g = jnp.where(lane_g == r, jnp.exp2(jnp.concatenate([aend, aend], axis=1)), dec_g)
        x_bd = xbd_ref[g]
        y_g = (_dot(jnp.concatenate(m_parts, axis=1), x_bd) + _dot(jnp.concatenate(d_parts, axis=1), sbd_ref[g])
               + xs_g * dskip_ref[:, g * gw:(g + 1) * gw])
        st_new = st_g * dec_g + _dot(jnp.concatenate(w_parts, axis=1), x_bd)
        st_ref[g] = st_new
        st_b = st_new.astype(BF16)
        for r in range(hpg):
            sbd_ref[g, r * SSD_STATE:(r + 1) * SSD_STATE, r * hd:(r + 1) * hd] = st_b[:, r * hd:(r + 1) * hd]
        yz = y_g * _silu(z_ref[:, g * gw:(g + 1) * gw])
        y_ref[:, g * gw:(g + 1) * gw] = _rms(yz, ng_ref[:, g * gw:(g + 1) * gw]).astype(y_ref.dtype)

    @pl.when(c == pl.num_programs(1) - 1)
    def _():
        for g in range(SSD_GROUPS):
            state_ref[0, g * hpg:(g + 1) * hpg] = st_ref[g].T.reshape(hpg, SSD_HEAD_DIM, SSD_STATE)


def ssd_prompt(z, xbc, dt, conv_w, conv_b, dt_bias, a_log, d_skip, norm_g, batch, seq, pool, pool_layer, table):
    t = SSD_CHUNK
    nc = seq // t
    pad = LANES - SSD_HEADS
    n_seqs, n_pages = table.shape
    pps = (n_seqs * n_pages) // (batch * nc)
    assert pps * batch * nc == n_seqs * n_pages and pps % PAGES_PER_BLOCK == 0 and n_pages % pps == 0
    gps = n_pages // pps
    page_shape = pool.shape[2:]
    rowblk = lambda b, c, tbl: (b * nc + c, 0)
    fixed = lambda b, c, tbl: (0, 0)

    def page_spec(k):
        def index(b, c, tbl):
            step = b * nc + c
            return (pool_layer, tbl[(step // gps) * n_pages + (step % gps) * pps + k], 0, 0, 0)
        return pl.BlockSpec((None, None) + page_shape, index)

    y, conv_tail, state, key_sums = pl.pallas_call(
        _ssd_chunk_kernel,
        grid_spec=pltpu.PrefetchScalarGridSpec(
            num_scalar_prefetch=1,
            grid=(batch, nc),
            in_specs=[pl.BlockSpec((t, SSD_D_INNER), rowblk),
                      pl.BlockSpec((t, SSD_CONV_DIM), rowblk),
                      pl.BlockSpec((t, LANES), rowblk),
                      pl.BlockSpec((CONV_W, SSD_CONV_DIM), fixed),
                      pl.BlockSpec((1, SSD_CONV_DIM), fixed),
                      pl.BlockSpec((1, LANES), fixed),
                      pl.BlockSpec((1, LANES), fixed),
                      pl.BlockSpec((1, SSD_D_INNER), fixed),
                      pl.BlockSpec((1, SSD_D_INNER), fixed),
                      pl.BlockSpec((LANES, SSD_HEADS * LANES), fixed)]
                     + [page_spec(k) for k in range(pps)],
            out_specs=[pl.BlockSpec((t, SSD_D_INNER), rowblk),
                       pl.BlockSpec((1, SUBLANES, SSD_CONV_DIM), lambda b, c, tbl: (b, 0, 0)),
                       pl.BlockSpec((1, SSD_HEADS, SSD_HEAD_DIM, SSD_STATE), lambda b, c, tbl: (b, 0, 0, 0)),
                       pl.BlockSpec((1, pps // PAGES_PER_BLOCK) + page_shape[1:],
                                    lambda b, c, tbl: ((b * nc + c) // gps, (b * nc + c) % gps, 0, 0))],
            scratch_shapes=[pltpu.VMEM((SSD_CONV_DIM // LANES, SUBLANES + t, LANES), F32),
                            pltpu.VMEM((t, SSD_CONV_DIM), F32),
                            pltpu.VMEM((t, SSD_HEADS * LANES), F32),
                            pltpu.VMEM((SSD_GROUPS, SSD_STATE, SSD_GROUP_W), F32),
                            pltpu.VMEM((SSD_GROUPS, SSD_HEADS_PER_GROUP * t, SSD_GROUP_W), BF16),
                            pltpu.VMEM((SSD_GROUPS, SSD_HEADS_PER_GROUP * SSD_STATE, SSD_GROUP_W), BF16)]),
        out_shape=[jax.ShapeDtypeStruct((batch * seq, SSD_D_INNER), BF16),
                   jax.ShapeDtypeStruct((batch, SUBLANES, SSD_CONV_DIM), F32),
                   jax.ShapeDtypeStruct((batch, SSD_HEADS, SSD_HEAD_DIM, SSD_STATE), F32),
                   jax.ShapeDtypeStruct((n_seqs, n_pages // PAGES_PER_BLOCK) + page_shape[1:], F32)],
        compiler_params=_cparams("parallel", "arbitrary"),
        name="ssd_prompt",
    )(table.reshape(-1), z, xbc, dt, conv_w, conv_b.reshape(1, -1), jnp.pad(dt_bias, (0, pad)).reshape(1, LANES),
      jnp.pad(a_log, (0, pad)).reshape(1, LANES), jnp.repeat(d_skip, SSD_HEAD_DIM).reshape(1, -1),
      norm_g.reshape(1, -1), _ssd_head_select(), *([pool] * pps))
    return y, conv_tail[:, SUBLANES - (CONV_W - 1):, :], state, key_sums


MOBA_HEADS_PER_STEP = 4


def _col_reduce(x, pair_op, reduce_fn):
    n = x.shape[0]
    while n > 8 * SUBLANES and n % 2 == 0:
        n //= 2
        x = pair_op(x[:n], x[n:])
    return reduce_fn(x, axis=0, keepdims=True)


def _moba_prompt_kernel(q_ref, k_ref, v_ref, o_ref, kb_ref, vt_ref, sel_ref, s_ref, p_ref):
    qi = pl.program_id(2)
    blk = MOBA_BLOCK
    seq = k_ref.shape[0]
    nb = seq // blk
    nbp = sel_ref.shape[2]
    hd = MOBA_HEAD_DIM
    heads = q_ref.shape[1] // hd

    def prepare(hh):
        cols = slice(hh * hd, (hh + 1) * hd)
        means = []
        for j in range(nb):
            kj = k_ref[j * blk:(j + 1) * blk, cols]
            means.append(jnp.sum(kj, axis=0, keepdims=True) * (1.0 / blk))
            kb_ref[hh, j * blk:(j + 1) * blk, :] = kj.astype(BF16)
        for j in range(seq // LANES):
            vt_ref[hh, :, j * LANES:(j + 1) * LANES] = v_ref[j * LANES:(j + 1) * LANES, cols].T.astype(BF16)
        if nbp > nb:
            means.append(jnp.zeros((nbp - nb, hd), F32))
        kmean = jnp.concatenate(means, axis=0)
        sub = lax.broadcasted_iota(jnp.int32, (nbp, blk), 0)
        for i in range(nb):
            gm = jnp.where(sub < i, _dot3_nt(kmean, q_ref[i * blk:(i + 1) * blk, cols]), -jnp.inf)
            rank = jnp.zeros((nbp, blk), F32)
            for m in range(i):
                gm_m = gm[m:m + 1, :]
                beats = (gm_m > gm) | ((gm_m == gm) & (sub > m))
                rank = rank + jnp.where(beats, 1.0, 0.0)
            chosen = ((sub < i) & (rank < MOBA_TOPK)) | (sub == i)
            sel_ref[hh, i] = jnp.where(chosen, 1.0, 0.0)

    @pl.when(qi == 0)
    def _():
        for hh in range(heads):
            prepare(hh)

    causal = (lax.broadcasted_iota(jnp.int32, (blk, blk), 0)
              <= lax.broadcasted_iota(jnp.int32, (blk, blk), 1))

    def attend_head(hh, i):
        cols = slice(hh * hd, (hh + 1) * hd)
        qb = (q_ref[i * blk:(i + 1) * blk, cols] * (hd ** -0.5)).astype(BF16)
        sel = sel_ref[hh, i]
        n_keys = (i + 1) * blk
        m = None
        for j in range(i + 1):
            rows = slice(j * blk, (j + 1) * blk)
            s = _dot_nt(kb_ref[hh, rows, :], qb)
            if j < i:
                s = s + jnp.where(sel[j:j + 1, :] > 0.0, 0.0, NEG_BIG)
            else:
                s = jnp.where(causal, s, NEG_BIG)
            s_ref[hh, rows, :] = s
            bm = _col_reduce(s, jnp.maximum, jnp.max)
            m = bm if m is None else jnp.maximum(m, bm)
        l = jnp.zeros((1, blk), F32)
        for j in range(i + 1):
            rows = slice(j * blk, (j + 1) * blk)
            p = jnp.exp(s_ref[hh, rows, :] - m)
            l = l + _col_reduce(p, jnp.add, jnp.sum)
            p_ref[hh, rows, :] = p.astype(BF16)
        acc = _dot(vt_ref[hh, :, 0:n_keys], p_ref[hh, 0:n_keys, :])
        o_ref[:, cols] = (acc / l).T.astype(o_ref.dtype)

    def attend(i):
        for hh in range(heads):
            attend_head(hh, i)

    for i in range(nb):
        pl.when(qi == i)(functools.partial(attend, i))


def moba_prompt(q, k, v, batch, seq):
    nb = seq // MOBA_BLOCK
    hd = MOBA_HEAD_DIM
    hps = MOBA_HEADS_PER_STEP
    wide = lambda b, h, i: (b, h)
    return pl.pallas_call(
        _moba_prompt_kernel,
        grid=(batch, MOBA_HEADS // hps, nb),
        in_specs=[pl.BlockSpec((seq, hps * hd), wide),
                  pl.BlockSpec((seq, hps * hd), wide),
                  pl.BlockSpec((seq, hps * hd), wide)],
        out_specs=pl.BlockSpec((MOBA_BLOCK, hps * hd), lambda b, h, i: (b * nb + i, h)),
        out_shape=jax.ShapeDtypeStruct((batch * seq, D_MODEL), BF16),
        scratch_shapes=[pltpu.VMEM((hps, seq, hd), BF16),
                        pltpu.VMEM((hps, hd, seq), BF16),
                        pltpu.VMEM((hps, nb, -(-nb // SUBLANES) * SUBLANES, MOBA_BLOCK), F32),
                        pltpu.VMEM((hps, seq, MOBA_BLOCK), F32),
                        pltpu.VMEM((hps, seq, MOBA_BLOCK), BF16)],
        compiler_params=_cparams("parallel", "parallel", "arbitrary"),
        name="moba_prompt",
    )(q, k, v)


def _lru_gates(xc, wa, ba, wx, bx, lam):
    xb = xc.astype(BF16)
    r = _sigmoid(_dot(xb, wa) + ba)
    i = _sigmoid(_dot(xb, wx) + bx)
    log_a = (-LRU_C) * r * _softplus(-lam)
    a = jnp.exp(log_a)
    v = -jnp.tanh(log_a) * (a * a + 1.0)
    u = jnp.where(v > 0.0, v * lax.rsqrt(v), 0.0) * (i * xc)
    return a, u


def _lru_chunk_kernel(gate_ref, x_ref, cw_ref, cb_ref, wa_ref, ba_ref, wx_ref, bx_ref, lam_ref,
                      y_ref, conv_ref, hlast_ref, xp_ref, a_ref, u_ref, h_ref):
    c = pl.program_id(1)
    t = LRU_CHUNK
    w = LRU_BLOCK_W

    @pl.when(c == 0)
    def _():
        xp_ref[:, 0:SUBLANES, :] = jnp.zeros((LRU_BLOCKS, SUBLANES, w), F32)
        h_ref[...] = jnp.zeros_like(h_ref)

    for n in range(LRU_BLOCKS):
        cols = slice(n * w, (n + 1) * w)
        xp_ref[n, SUBLANES:SUBLANES + t, :] = x_ref[:, cols]
        acc = xp_ref[n, SUBLANES - 3:SUBLANES - 3 + t, :] * cw_ref[0:1, cols]
        for k in range(1, CONV_W):
            acc = acc + xp_ref[n, SUBLANES - 3 + k:SUBLANES - 3 + k + t, :] * cw_ref[k:k + 1, cols]
        xc = acc + cb_ref[:, cols]
        a, u = _lru_gates(xc, wa_ref[n], ba_ref[:, cols], wx_ref[n], bx_ref[:, cols], lam_ref[:, cols])
        a_ref[:, cols] = a
        u_ref[:, cols] = u
        xp_ref[n, 0:SUBLANES, :] = x_ref[t - SUBLANES:t, cols]
    conv_ref[0] = x_ref[t - SUBLANES:t, :]

    srow = lax.broadcasted_iota(jnp.int32, (SUBLANES, LRU_WIDTH), 0)

    def tile(i, h):
        rows = pl.ds(pl.multiple_of(i * SUBLANES, SUBLANES), SUBLANES)
        a = a_ref[rows, :]
        u = u_ref[rows, :]
        for d in (1, 2, 4):
            keep = srow >= d
            a_sh = jnp.where(keep, pltpu.roll(a, d, 0), 1.0)
            u_sh = jnp.where(keep, pltpu.roll(u, d, 0), 0.0)
            u = a * u_sh + u
            a = a * a_sh
        hs = a * h + u
        u_ref[rows, :] = hs
        return jnp.broadcast_to(hs[SUBLANES - 1:SUBLANES, :], (SUBLANES, LRU_WIDTH))

    h = lax.fori_loop(0, t // SUBLANES, tile, h_ref[...])
    h_ref[...] = h
    hlast_ref[0] = h
    y_ref[...] = (_gelu_tanh(gate_ref[...]) * u_ref[...]).astype(y_ref.dtype)


def lru_prompt(gate_br, x_br, conv_w, conv_b, w_a, b_a, w_x, b_x, lam, batch, seq):
    t = LRU_CHUNK
    nc = seq // t
    wd = LRU_WIDTH
    rowblk = lambda b, c: (b * nc + c, 0)
    fixed = lambda b, c: (0, 0)
    fixed3 = lambda b, c: (0, 0, 0)
    perb = lambda b, c: (b, 0, 0)
    y, conv_tail, h_last = pl.pallas_call(
        _lru_chunk_kernel,
        grid=(batch, nc),
        in_specs=[pl.BlockSpec((t, wd), rowblk),
                  pl.BlockSpec((t, wd), rowblk),
                  pl.BlockSpec((CONV_W, wd), fixed),
                  pl.BlockSpec((1, wd), fixed),
                  pl.BlockSpec((LRU_BLOCKS, LRU_BLOCK_W, LRU_BLOCK_W), fixed3),
                  pl.BlockSpec((1, wd), fixed),
                  pl.BlockSpec((LRU_BLOCKS, LRU_BLOCK_W, LRU_BLOCK_W), fixed3),
                  pl.BlockSpec((1, wd), fixed),
                  pl.BlockSpec((1, wd), fixed)],
        out_specs=[pl.BlockSpec((t, wd), rowblk),
                   pl.BlockSpec((1, SUBLANES, wd), perb),
                   pl.BlockSpec((1, SUBLANES, wd), perb)],
        out_shape=[jax.ShapeDtypeStruct((batch * seq, wd), BF16),
                   jax.ShapeDtypeStruct((batch, SUBLANES, wd), F32),
                   jax.ShapeDtypeStruct((batch, SUBLANES, wd), F32)],
        scratch_shapes=[pltpu.VMEM((LRU_BLOCKS, SUBLANES + t, LRU_BLOCK_W), F32),
                        pltpu.VMEM((t, wd), F32),
                        pltpu.VMEM((t, wd), F32),
                        pltpu.VMEM((SUBLANES, wd), F32)],
        compiler_params=_cparams("parallel", "arbitrary"),
        name="lru_prompt",
    )(gate_br, x_br, conv_w, conv_b.reshape(1, wd), w_a.astype(BF16), b_a.reshape(1, wd),
      w_x.astype(BF16), b_x.reshape(1, wd), lam.reshape(1, wd))
    return y, conv_tail[:, SUBLANES - (CONV_W - 1):, :], h_last[:, 0, :]


def _ssd_step_kernel(z_ref, xbc_ref, dt_ref, cs_ref, st_ref, cw_ref, cb_ref, dtb_ref, alog_ref, dskip_ref, ng_ref,
                     y_ref, cso_ref, sto_ref):
    gw = SSD_GROUP_W
    hpg = SSD_HEADS_PER_GROUP
    x = xbc_ref[0]
    buf = cs_ref[0]
    conv = x * cw_ref[CONV_W - 1:CONV_W, :]
    for k in range(CONV_W - 1):
        conv = conv + buf[k:k + 1, :] * cw_ref[k:k + 1, :]
    act = _silu(conv + cb_ref[...])
    cso_ref[0, 0:CONV_W - 2, :] = buf[1:CONV_W - 1, :]
    cso_ref[0, CONV_W - 2:CONV_W - 1, :] = x

    dtv = _softplus(dt_ref[0] + dtb_ref[...])
    dec = jnp.exp(dtv * (-jnp.exp(alog_ref[...])))
    lane_g = lax.broadcasted_iota(jnp.int32, (1, gw), 1) // SSD_HEAD_DIM
    first_row = lax.broadcasted_iota(jnp.int32, (SUBLANES, gw), 0) == 0
    ys = []
    for g in range(SSD_GROUPS):
        xs_g = act[:, g * gw:(g + 1) * gw]
        b_g = act[:, SSD_D_INNER + g * SSD_STATE:SSD_D_INNER + (g + 1) * SSD_STATE]
        c_g = act[:, SSD_D_INNER + SSD_BC_DIM + g * SSD_STATE:SSD_D_INNER + SSD_BC_DIM + (g + 1) * SSD_STATE]
        dt_e = jnp.zeros((1, gw), F32)
        dec_rows = []
        for r in range(hpg):
            h = g * hpg + r
            dt_e = jnp.where(lane_g == r, dtv[:, h:h + 1], dt_e)
            dec_rows.append(jnp.broadcast_to(dec[:, h:h + 1], (SSD_HEAD_DIM, SSD_STATE)))
        xdt = xs_g * dt_e
        x8 = jnp.where(first_row, jnp.broadcast_to(xdt, (SUBLANES, gw)), 0.0).astype(BF16)
        b8 = jnp.broadcast_to(b_g, (SUBLANES, SSD_STATE)).astype(BF16)
        outer = lax.dot_general(x8, b8, (((0,), (0,)), ((), ())), preferred_element_type=F32)
        old = st_ref[0, g * hpg:(g + 1) * hpg].reshape(gw, SSD_STATE)
        new = old * jnp.concatenate(dec_rows, axis=0) + outer
        sto_ref[0, g * hpg:(g + 1) * hpg] = new.reshape(hpg, SSD_HEAD_DIM, SSD_STATE)
        c8 = jnp.broadcast_to(c_g, (SUBLANES, SSD_STATE)).astype(BF16)
        y_g = _dot_nt(c8, new.astype(BF16))[0:1, :] + xs_g * dskip_ref[:, g * gw:(g + 1) * gw]
        yz = y_g * _silu(z_ref[0][:, g * gw:(g + 1) * gw])
        ys.append(_rms(yz, ng_ref[:, g * gw:(g + 1) * gw]))
    y_ref[0] = jnp.concatenate(ys, axis=1)


def ssd_step(z, xbc, dt, conv_state, ssm_state, layer, conv_w, conv_b, dt_bias, a_log, d_skip, norm_g):
    nb = z.shape[0]
    pad = LANES - SSD_HEADS
    per3 = lambda b: (b, 0, 0)
    fixed = lambda b: (0, 0)
    y, conv_new, state_new = pl.pallas_call(
        _ssd_step_kernel,
        grid=(nb,),
        in_specs=[pl.BlockSpec((1, 1, SSD_D_INNER), per3),
                  pl.BlockSpec((1, 1, SSD_CONV_DIM), per3),
                  pl.BlockSpec((1, 1, LANES), per3),
                  pl.BlockSpec((None, 1, CONV_W - 1, SSD_CONV_DIM), lambda b: (layer, b, 0, 0)),
                  pl.BlockSpec((None, 1, SSD_HEADS, SSD_HEAD_DIM, SSD_STATE), lambda b: (layer, b, 0, 0, 0)),
                  pl.BlockSpec((CONV_W, SSD_CONV_DIM), fixed),
                  pl.BlockSpec((1, SSD_CONV_DIM), fixed),
                  pl.BlockSpec((1, LANES), fixed),
                  pl.BlockSpec((1, LANES), fixed),
                  pl.BlockSpec((1, SSD_D_INNER), fixed),
                  pl.BlockSpec((1, SSD_D_INNER), fixed)],
        out_specs=[pl.BlockSpec((1, 1, SSD_D_INNER), per3),
                   pl.BlockSpec((1, CONV_W - 1, SSD_CONV_DIM), per3),
                   pl.BlockSpec((1, SSD_HEADS, SSD_HEAD_DIM, SSD_STATE), lambda b: (b, 0, 0, 0))],
        out_shape=[jax.ShapeDtypeStruct((nb, 1, SSD_D_INNER), F32),
                   jax.ShapeDtypeStruct((nb, CONV_W - 1, SSD_CONV_DIM), F32),
                   jax.ShapeDtypeStruct((nb, SSD_HEADS, SSD_HEAD_DIM, SSD_STATE), F32)],
        compiler_params=_cparams("parallel"),
        name="ssd_step",
    )(z.reshape(nb, 1, -1), xbc.reshape(nb, 1, -1), dt.reshape(nb, 1, -1), conv_state, ssm_state,
      conv_w, conv_b.reshape(1, -1), jnp.pad(dt_bias, (0, pad)).reshape(1, LANES),
      jnp.pad(a_log, (0, pad)).reshape(1, LANES), jnp.repeat(d_skip, SSD_HEAD_DIM).reshape(1, -1),
      norm_g.reshape(1, -1))
    return y.reshape(nb, SSD_D_INNER), conv_new, state_new


def _lru_step_kernel(gate_ref, x_ref, cs_ref, h0_ref, cw_ref, cb_ref, wa_ref, ba_ref, wx_ref, bx_ref, lam_ref,
                     y_ref, cso_ref, ho_ref):
    w = LRU_BLOCK_W
    x = x_ref[...]
    xc = x * cw_ref[CONV_W - 1:CONV_W, :] + cb_ref[...]
    for k in range(CONV_W - 1):
        xc = xc + cs_ref[k] * cw_ref[k:k + 1, :]
    for k in range(CONV_W - 2):
        cso_ref[k] = cs_ref[k + 1]
    cso_ref[CONV_W - 2] = x
    for n in range(LRU_BLOCKS):
        cols = slice(n * w, (n + 1) * w)
        a, u = _lru_gates(xc[:, cols], wa_ref[n], ba_ref[:, cols], wx_ref[n], bx_ref[:, cols], lam_ref[:, cols])
        h = a * h0_ref[:, cols] + u
        ho_ref[:, cols] = h
        y_ref[:, cols] = _gelu_tanh(gate_ref[:, cols]) * h


def lru_step(gate_br, x_br, conv_state, h0, conv_w, conv_b, w_a, b_a, w_x, b_x, lam):
    nb, wd = x_br.shape
    return pl.pallas_call(
        _lru_step_kernel,
        out_shape=[jax.ShapeDtypeStruct((nb, wd), F32),
                   jax.ShapeDtypeStruct((CONV_W - 1, nb, wd), F32),
                   jax.ShapeDtypeStruct((nb, wd), F32)],
        compiler_params=pltpu.CompilerParams(vmem_limit_bytes=VMEM_LIMIT),
        name="lru_step",
    )(gate_br, x_br, conv_state, h0, conv_w, conv_b.reshape(1, wd), w_a.astype(BF16), b_a.reshape(1, wd),
      w_x.astype(BF16), b_x.reshape(1, wd), lam.reshape(1, wd))


def _moba_gate_kernel(q_ref, bs_ref, sel_ref):
    n_blocks = bs_ref.shape[1]
    q = q_ref[0]
    gates = [jnp.sum((bs_ref[0, n] * (1.0 / MOBA_BLOCK)) * q, axis=-1, keepdims=True) for n in range(n_blocks)]
    lane = lax.broadcasted_iota(jnp.int32, (MOBA_HEADS, LANES), 1)
    out = jnp.zeros((MOBA_HEADS, LANES), F32)
    for k in range(MOBA_TOPK):
        best = functools.reduce(jnp.maximum, gates)
        idx = jnp.full((MOBA_HEADS, 1), float(n_blocks), F32)
        for n in reversed(range(n_blocks)):
            idx = jnp.where(gates[n] == best, float(n), idx)
        out = jnp.where(lane == k, idx, out)
        gates = [jnp.where(idx == float(n), -jnp.inf, gates[n]) for n in range(n_blocks)]
    sel_ref[0] = out.astype(jnp.int32)


def moba_gate(q, block_sums):
    nb = q.shape[0]
    n_blocks = block_sums.shape[1]
    return pl.pallas_call(
        _moba_gate_kernel,
        grid=(nb,),
        in_specs=[pl.BlockSpec((1, MOBA_HEADS, MOBA_HEAD_DIM), lambda b: (b, 0, 0)),
                  pl.BlockSpec((1, n_blocks, MOBA_HEADS, MOBA_HEAD_DIM), lambda b: (b, 0, 0, 0))],
        out_specs=pl.BlockSpec((1, MOBA_HEADS, LANES), lambda b: (b, 0, 0)),
        out_shape=jax.ShapeDtypeStruct((nb, MOBA_HEADS, LANES), jnp.int32),
        compiler_params=_cparams("parallel"),
        name="moba_gate",
    )(q.reshape(nb, MOBA_HEADS, MOBA_HEAD_DIM), block_sums)


MOBA_DECODE_SLOTS = 3


def _moba_decode_kernel(tbl_ref, sel_ref, q_ref, kn_ref, vn_ref, kpool_ref, vpool_ref, o_ref, kbuf, vbuf, sem,
                        *, layer, n_table_pages):
    b = pl.program_id(0)
    nb = pl.num_programs(0)
    n_slots = kbuf.shape[0]
    slot = lax.rem(b, n_slots)
    n_sel = MOBA_TOPK * PAGES_PER_BLOCK
    hd = MOBA_HEAD_DIM

    def page_copies(bb, sl):
        out = []
        for hh in range(MOBA_HEADS):
            for k in range(MOBA_TOPK):
                blk = sel_ref[(bb * MOBA_HEADS + hh) * MOBA_TOPK + k]
                for s in range(PAGES_PER_BLOCK):
                    page = tbl_ref[bb * n_table_pages + blk * PAGES_PER_BLOCK + s]
                    i = k * PAGES_PER_BLOCK + s
                    out.append(pltpu.make_async_copy(kpool_ref.at[layer, page, :, hh, :], kbuf.at[sl, hh, i],
                                                     sem.at[sl, 0, hh, i]))
                    out.append(pltpu.make_async_copy(vpool_ref.at[layer, page, :, hh, :], vbuf.at[sl, hh, i],
                                                     sem.at[sl, 1, hh, i]))
        return out

    def start_all(copies):
        for i, c in enumerate(copies):
            c.start(priority=i % 2)

    @pl.when(b == 0)
    def _():
        for ahead in range(n_slots - 1):
            @pl.when(ahead < nb)
            def _():
                start_all(page_copies(ahead, ahead))

    @pl.when(b + n_slots - 1 < nb)
    def _():
        start_all(page_copies(b + n_slots - 1, lax.rem(b + n_slots - 1, n_slots)))

    for c in page_copies(b, slot):
        c.wait()

    scale = hd ** -0.5
    outs = []
    for hh in range(MOBA_HEADS):
        cols = slice(hh * hd, (hh + 1) * hd)
        q = q_ref[0][:, cols]
        q8 = jnp.broadcast_to(q, (SUBLANES, hd)).astype(BF16)
        k_cat = kbuf[slot, hh].reshape(n_sel * PAGE_SIZE, hd).astype(BF16)
        v_cat = vbuf[slot, hh].reshape(n_sel * PAGE_SIZE, hd).astype(BF16)
        s = _dot_nt(q8, k_cat) * scale
        s_new = jnp.sum(q * kn_ref[0][:, cols], axis=1, keepdims=True) * scale
        m = jnp.maximum(jnp.max(s, axis=1, keepdims=True), s_new)
        p = jnp.exp(s - m)
        p_new = jnp.exp(s_new - m)
        denom = jnp.sum(p, axis=1, keepdims=True) + p_new
        o = (_dot(p.astype(BF16), v_cat) + p_new * vn_ref[0][:, cols]) / denom
        outs.append(o[0:1, :])
    o_ref[0] = jnp.concatenate(outs, axis=1)


def moba_decode(q, k_new, v_new, k_pool, v_pool, layer, table, sel):
    nb = q.shape[0]
    n_pages = table.shape[1]
    hd = MOBA_HEAD_DIM
    n_sel = MOBA_TOPK * PAGES_PER_BLOCK
    tok = pl.BlockSpec((1, 1, D_MODEL), lambda b, tbl, sl: (b, 0, 0))
    hbm = pl.BlockSpec(memory_space=pl.ANY)
    out = pl.pallas_call(
        functools.partial(_moba_decode_kernel, layer=layer, n_table_pages=n_pages),
        grid_spec=pltpu.PrefetchScalarGridSpec(
            num_scalar_prefetch=2,
            grid=(nb,),
            in_specs=[tok, tok, tok, hbm, hbm],
            out_specs=tok,
            scratch_shapes=[pltpu.VMEM((MOBA_DECODE_SLOTS, MOBA_HEADS, n_sel, PAGE_SIZE, hd), F32),
                            pltpu.VMEM((MOBA_DECODE_SLOTS, MOBA_HEADS, n_sel, PAGE_SIZE, hd), F32),
                            pltpu.SemaphoreType.DMA((MOBA_DECODE_SLOTS, 2, MOBA_HEADS, n_sel))]),
        out_shape=jax.ShapeDtypeStruct((nb, 1, D_MODEL), F32),
        compiler_params=_cparams("arbitrary"),
        name="moba_decode",
    )(table.reshape(-1), sel[:, :, :MOBA_TOPK].reshape(-1),
      q.reshape(nb, 1, D_MODEL), k_new.reshape(nb, 1, D_MODEL), v_new.reshape(nb, 1, D_MODEL), k_pool, v_pool)
    return out.reshape(nb, D_MODEL)


def _prep_weights(W):
    bf = lambda a: a.astype(BF16)
    w_in, w_qkv, w_lru = lax.optimization_barrier((bf(W["ssd_w_in"]), bf(W["moba_w_qkv"]), bf(W["lru_w_in"])))
    zx = SSD_D_INNER + SSD_CONV_DIM
    return dict(
        ssd_w_z=w_in[:, :, :SSD_D_INNER],
        ssd_w_xbc=w_in[:, :, SSD_D_INNER:zx],
        ssd_w_dt=jnp.pad(w_in[:, :, zx:], ((0, 0), (0, 0), (0, LANES - SSD_HEADS))),
        ssd_w_out=bf(W["ssd_w_out"]),
        moba_w_q=w_qkv[:, :, :D_MODEL],
        moba_w_k=w_qkv[:, :, D_MODEL:2 * D_MODEL],
        moba_w_v=w_qkv[:, :, 2 * D_MODEL:],
        moba_w_o=bf(W["moba_w_o"]),
        lru_w_gate=w_lru[:, :, :LRU_WIDTH],
        lru_w_x=w_lru[:, :, LRU_WIDTH:],
        lru_w_out=bf(W["lru_w_out"]),
        mlp_w_up=bf(W["mlp_w_up"]), mlp_w_down=bf(W["mlp_w_down"]),
        ple_w_proj=bf(W["ple_w_proj"]), ple_w_gate=bf(W["ple_w_gate"]),
    )


def _trunk(h, p, W, Wb, tiles, ssd_fn, moba_fn, lru_fn):
    tm, tf = tiles
    ssd_out, moba_out, lru_out = [], [], []
    xn = None
    for i in range(DEPTH):
        j, kind = i // N_MIXERS, i % N_MIXERS
        if i == 0:
            proj = lambda *names: multi_linear(h, [Wb[n] for n in names], j, tm=tm, norm_g=W["norm_mix"][0])
        else:
            proj = lambda *names: multi_linear(xn, [Wb[n] for n in names], j, tm=tm)
        if kind == 0:
            y, conv, state = ssd_fn(j, *proj("ssd_w_z", "ssd_w_xbc", "ssd_w_dt"))
            ssd_out.append((conv, state))
            w_out = Wb["ssd_w_out"]
        elif kind == 1:
            q, k, v = proj("moba_w_q", "moba_w_k", "moba_w_v")
            moba_out.append((k, v))
            y, w_out = moba_fn(j, q, k, v), Wb["moba_w_o"]
        else:
            y, conv, state = lru_fn(j, *proj("lru_w_gate", "lru_w_x"))
            lru_out.append((conv, state))
            w_out = Wb["lru_w_out"]
        final = i == DEPTH - 1
        g_next = W["norm_final"] if final else W["norm_mix"][i + 1]
        out = layer_tail(h, y, w_out, j, W["norm_mlp"][i], Wb["mlp_w_up"], Wb["mlp_w_down"], p,
                         W["norm_ple"][i], Wb["ple_w_gate"], Wb["ple_w_proj"], i, g_next,
                         tm=tm, tf=tf, final=final)
        h, xn = (out, None) if final else out
    stack = lambda pairs, n: jnp.stack([pr[n] for pr in pairs])
    return (h, stack(ssd_out, 0), stack(ssd_out, 1), stack(moba_out, 0), stack(moba_out, 1),
            stack(lru_out, 0), stack(lru_out, 1))


PROMPT_TILES = (512, 1024)
SAMPLE_TILES = (32, 1024)


def kernel(x_prompt, x_sample, state_ssd_conv, state_ssd, cache_k, cache_v, page_table, state_lru_conv,
           state_lru, p_prompt, p_sample, norm_mix, norm_mlp, norm_ple, norm_final, ssd_w_in, ssd_conv_w,
           ssd_conv_b, ssd_dt_bias, ssd_a_log, ssd_d, ssd_norm, ssd_w_out, moba_w_qkv, moba_w_o, lru_w_in,
           lru_conv_w, lru_conv_b, lru_w_a, lru_b_a, lru_w_x, lru_b_x, lru_lambda, lru_w_out, mlp_w_up,
           mlp_w_down, ple_w_proj, ple_w_gate):
    W = dict(norm_mix=norm_mix, norm_mlp=norm_mlp, norm_ple=norm_ple, norm_final=norm_final,
             ssd_w_in=ssd_w_in, ssd_w_out=ssd_w_out, moba_w_qkv=moba_w_qkv, moba_w_o=moba_w_o,
             lru_w_in=lru_w_in, lru_w_out=lru_w_out, mlp_w_up=mlp_w_up, mlp_w_down=mlp_w_down,
             ple_w_proj=ple_w_proj, ple_w_gate=ple_w_gate)
    Wb = _prep_weights(W)
    batch, seq, d = x_prompt.shape
    dec_batch, dec_seq, _ = x_sample.shape
    assert dec_seq == 1 and seq % MOBA_BLOCK == 0 and seq % LRU_CHUNK == 0

    ssd_args = lambda j: (ssd_conv_w[j], ssd_conv_b[j], ssd_dt_bias[j], ssd_a_log[j], ssd_d[j], ssd_norm[j])
    lru_args = lambda j: (lru_conv_w[j], lru_conv_b[j], lru_w_a[j], lru_b_a[j].reshape(-1), lru_w_x[j],
                          lru_b_x[j].reshape(-1), lru_lambda[j])

    n_ssd_calls = ssd_w_in.shape[0]
    assert cache_k.shape[0] == 1 and dec_batch % n_ssd_calls == 0
    share = dec_batch // n_ssd_calls
    key_sums = []

    def ssd_prompt_fn(j, z, xbc, dt):
        y, conv, state, sums = ssd_prompt(z, xbc, dt, *ssd_args(j), batch, seq,
                                          cache_k, 0, page_table[j * share:(j + 1) * share])
        key_sums.append(sums)
        return y, conv, state

    out_p = _trunk(
        x_prompt.reshape(batch * seq, d), p_prompt.reshape(DEPTH, batch * seq, PLE_DIM), W, Wb, PROMPT_TILES,
        ssd_prompt_fn,
        lambda j, q, k, v: moba_prompt(q, k, v, batch, seq),
        lambda j, gate, x: lru_prompt(gate, x, *lru_args(j), batch, seq))

    def moba_sample(j, q, k, v):
        sel = moba_gate(q, jnp.concatenate(key_sums, axis=0))
        return moba_decode(q, k, v, cache_k, cache_v, j, page_table, sel)

    def lru_sample(j, gate, x):
        y, conv, h = lru_step(gate, x, jnp.swapaxes(state_lru_conv[j], 0, 1), state_lru[j], *lru_args(j))
        return y, jnp.swapaxes(conv, 0, 1), h

    out_s = _trunk(
        x_sample.reshape(dec_batch, d), p_sample.reshape(DEPTH, dec_batch, PLE_DIM), W, Wb, SAMPLE_TILES,
        lambda j, z, xbc, dt: ssd_step(z, xbc, dt, state_ssd_conv, state_ssd, j, *ssd_args(j)),
        moba_sample, lru_sample)

    def shaped(out, b, s):
        y, ssd_conv, ssd_state, k, v, lru_conv, lru_state = out
        kv_shape = (-1, b, s, MOBA_HEADS, MOBA_HEAD_DIM)
        return (y.reshape(b, s, d), ssd_conv, ssd_state, k.reshape(kv_shape), v.reshape(kv_shape),
                lru_conv, lru_state)

    yp, *rest_p = shaped(out_p, batch, seq)
    ys, *rest_s = shaped(out_s, dec_batch, dec_seq)
    return (yp, ys, *rest_p, *rest_s)
```

```python
import functools
import math

import jax
import jax.numpy as jnp
from jax import lax
from jax.experimental import pallas as pl
from jax.experimental.pallas import tpu as pltpu

D_MODEL = 1024
DEPTH = 4
N_MIXERS = 3
PLE_DIM = 256
D_FF = 4 * D_MODEL
NORM_EPS = 1e-6
CONV_W = 4
PAGE_SIZE = 128

SSD_D_INNER = 2 * D_MODEL
SSD_HEAD_DIM = 64
SSD_HEADS = SSD_D_INNER // SSD_HEAD_DIM
SSD_GROUPS = 8
SSD_HEADS_PER_GROUP = SSD_HEADS // SSD_GROUPS
SSD_STATE = 128
SSD_BC_DIM = SSD_GROUPS * SSD_STATE
SSD_CONV_DIM = SSD_D_INNER + 2 * SSD_BC_DIM
SSD_GROUP_W = SSD_D_INNER // SSD_GROUPS
SSD_CHUNK = 128

MOBA_HEADS = 8
MOBA_HEAD_DIM = D_MODEL // MOBA_HEADS
MOBA_BLOCK = 256
MOBA_TOPK = 3
PAGES_PER_BLOCK = MOBA_BLOCK // PAGE_SIZE

LRU_WIDTH = D_MODEL
LRU_BLOCKS = 8
LRU_BLOCK_W = LRU_WIDTH // LRU_BLOCKS
LRU_C = 8.0
LRU_CHUNK = 256

LANES = 128
SUBLANES = 8
VMEM_LIMIT = 56 * 1024 * 1024
NEG_BIG = -1e30
LOG2E = 1.0 / math.log(2.0)
EXP2_CAP = 126.0

F32 = jnp.float32
BF16 = jnp.bfloat16


def _cparams(*sem):
    return pltpu.CompilerParams(dimension_semantics=sem, vmem_limit_bytes=VMEM_LIMIT)


def _rms(x, g):
    var = jnp.mean(x * x, axis=-1, keepdims=True)
    return (x * lax.rsqrt(var + NORM_EPS)) * g


def _sigmoid(x):
    return 0.5 + 0.5 * jnp.tanh(0.5 * x)


def _silu(x):
    hx = 0.5 * x
    return hx + hx * jnp.tanh(hx)


def _softplus(x):
    return jnp.maximum(x, 0.0) + jnp.log1p(jnp.exp(-jnp.abs(x)))


def _gelu_tanh(x):
    return 0.5 * x * (1.0 + jnp.tanh(math.sqrt(2.0 / math.pi) * (x + 0.044715 * (x * x * x))))


def _split3(x):
    hi = x.astype(BF16)
    r1 = x - hi.astype(F32)
    mid = r1.astype(BF16)
    lo = (r1 - mid.astype(F32)).astype(BF16)
    return hi, mid, lo


def _dot(a, b):
    return jnp.dot(a, b, preferred_element_type=F32)


def _dot_nt(a, b):
    return lax.dot_general(a, b, (((1,), (1,)), ((), ())), preferred_element_type=F32)


def _dot3_nt(a, b):
    ah, am, _ = _split3(a)
    bh, bm, _ = _split3(b)
    return _dot_nt(ah, bh) + (_dot_nt(ah, bm) + _dot_nt(am, bh))


def _multi_linear_kernel(x_ref, *refs, normed):
    x = x_ref[...]
    if normed:
        x = _rms(x, refs[0][...]).astype(BF16)
        refs = refs[1:]
    n = len(refs) // 2
    for w_ref, o_ref in zip(refs[:n], refs[n:]):
        o_ref[...] = _dot(x, w_ref[...])


def multi_linear(x, weights, layer, *, tm, norm_g=None):
    m, d = x.shape
    normed = norm_g is not None
    gain = [norm_g.reshape(1, d)] if normed else []
    return pl.pallas_call(
        functools.partial(_multi_linear_kernel, normed=normed),
        grid=(m // tm,),
        in_specs=[pl.BlockSpec((tm, d), lambda i: (i, 0))]
                 + [pl.BlockSpec((1, d), lambda i: (0, 0)) for _ in gain]
                 + [pl.BlockSpec((None,) + w.shape[1:], lambda i: (layer, 0, 0), pipeline_mode=pl.Buffered(1))
                    for w in weights],
        out_specs=[pl.BlockSpec((tm, w.shape[2]), lambda i: (i, 0)) for w in weights],
        out_shape=[jax.ShapeDtypeStruct((m, w.shape[2]), F32) for w in weights],
        compiler_params=_cparams("parallel"),
        name="multi_linear",
    )(x, *gain, *weights)


LAYER_TAIL_CHAINS = 2


def _layer_tail_kernel(h_ref, y_ref, wo_ref, gm_ref, wu_ref, wd_ref, p_ref, gp_ref, wg_ref, wp_ref, gn_ref,
                       *outs, final, chains, tf):
    rows_per = h_ref.shape[0] // chains
    ff = wu_ref.shape[1]
    for c in range(chains):
        rows = slice(c * rows_per, (c + 1) * rows_per)
        h1 = h_ref[rows, :] + _dot(y_ref[rows, :].astype(BF16), wo_ref[...])
        xn = _rms(h1, gm_ref[...]).astype(BF16)
        acc = None
        for f in range(ff // tf):
            hid = jnp.square(jnp.maximum(_dot(xn, wu_ref[:, f * tf:(f + 1) * tf]), 0.0))
            part = _dot(hid.astype(BF16), wd_ref[f * tf:(f + 1) * tf, :])
            acc = part if acc is None else acc + part
        h2 = h1 + acc
        gate = _sigmoid(_dot(_rms(h2, gp_ref[...]).astype(BF16), wg_ref[...]))
        h3 = h2 + _dot(p_ref[rows, :].astype(BF16), wp_ref[...]) * gate
        if final:
            outs[0][rows, :] = _rms(h3, gn_ref[...])
        else:
            outs[0][rows, :] = h3
            outs[1][rows, :] = _rms(h3, gn_ref[...]).astype(BF16)


def layer_tail(h, y, w_out, mixer_idx, g_mlp, w_up, w_down, p, g_ple, w_gate, w_proj, layer, g_next, *, tm, tf, final):
    m, d = h.shape
    k = y.shape[1]
    ff = w_up.shape[2]
    pd = p.shape[2]
    row = lambda i: (i, 0)
    fixed = lambda i: (0, 0)
    once = pl.Buffered(1)
    chains = LAYER_TAIL_CHAINS if tm % (LAYER_TAIL_CHAINS * LANES) == 0 else 1
    out_specs = [pl.BlockSpec((tm, d), row)]
    out_shape = [jax.ShapeDtypeStruct((m, d), F32)]
    if not final:
        out_specs.append(pl.BlockSpec((tm, d), row))
        out_shape.append(jax.ShapeDtypeStruct((m, d), BF16))
    outs = pl.pallas_call(
        functools.partial(_layer_tail_kernel, final=final, chains=chains, tf=tf),
        grid=(m // tm,),
        in_specs=[pl.BlockSpec((tm, d), row),
                  pl.BlockSpec((tm, k), row),
                  pl.BlockSpec((None, k, d), lambda i: (mixer_idx, 0, 0), pipeline_mode=once),
                  pl.BlockSpec((1, d), fixed),
                  pl.BlockSpec((None, d, ff), lambda i: (layer, 0, 0), pipeline_mode=once),
                  pl.BlockSpec((None, ff, d), lambda i: (layer, 0, 0), pipeline_mode=once),
                  pl.BlockSpec((None, tm, pd), lambda i: (layer, i, 0)),
                  pl.BlockSpec((1, d), fixed),
                  pl.BlockSpec((None, d, d), lambda i: (layer, 0, 0), pipeline_mode=once),
                  pl.BlockSpec((None, pd, d), lambda i: (layer, 0, 0), pipeline_mode=once),
                  pl.BlockSpec((1, d), fixed)],
        out_specs=out_specs,
        out_shape=out_shape,
        compiler_params=_cparams("parallel"),
        name="layer_tail",
    )(h, y, w_out, g_mlp.reshape(1, d), w_up, w_down, p, g_ple.reshape(1, d), w_gate, w_proj, g_next.reshape(1, d))
    return outs[0] if final else tuple(outs)


def _ssd_head_select():
    rows = jnp.arange(LANES)[:, None]
    cols = jnp.arange(SSD_HEADS * LANES)[None, :] // LANES
    return ((rows % SSD_HEADS == cols) & (rows < 3 * SSD_HEADS)).astype(BF16)


def _ssd_chunk_kernel(tbl_ref, z_ref, xbc_ref, dt_ref, cw_ref, cb_ref, dtb_ref, alog_ref, dskip_ref, ng_ref, sel_ref,
                      *refs):
    del tbl_ref
    n_pages = len(refs) - 10
    pages = refs[:n_pages]
    y_ref, conv_ref, state_ref, ksum_ref, xp_ref, act_ref, acol_ref, st_ref, xbd_ref, sbd_ref = refs[n_pages:]
    c = pl.program_id(1)
    t = SSD_CHUNK
    gw = SSD_GROUP_W
    hpg = SSD_HEADS_PER_GROUP
    hd = SSD_HEAD_DIM

    @pl.when(c == 0)
    def _():
        xp_ref[:, 0:SUBLANES, :] = jnp.zeros((xp_ref.shape[0], SUBLANES, LANES), F32)
        st_ref[...] = jnp.zeros_like(st_ref)
        xbd_ref[...] = jnp.zeros_like(xbd_ref)
        sbd_ref[...] = jnp.zeros_like(sbd_ref)

    for j in range(SSD_CONV_DIM // LANES):
        cols = slice(j * LANES, (j + 1) * LANES)
        xp_ref[j, SUBLANES:SUBLANES + t, :] = xbc_ref[:, cols]
        acc = xp_ref[j, SUBLANES - 3:SUBLANES - 3 + t, :] * cw_ref[0:1, cols]
        for k in range(1, CONV_W):
            acc = acc + xp_ref[j, SUBLANES - 3 + k:SUBLANES - 3 + k + t, :] * cw_ref[k:k + 1, cols]
        act_ref[:, cols] = _silu(acc + cb_ref[:, cols])
        xp_ref[j, 0:SUBLANES, :] = xbc_ref[t - SUBLANES:t, cols]
    conv_ref[0] = xbc_ref[t - SUBLANES:t, :]

    def row_sum(x):
        n = x.shape[0]
        while n > 1:
            n //= 2
            x = x[:n] + x[n:]
        return x[0]

    for k in range(n_pages // PAGES_PER_BLOCK):
        s = row_sum(pages[PAGES_PER_BLOCK * k][...])
        for j in range(1, PAGES_PER_BLOCK):
            s = s + row_sum(pages[PAGES_PER_BLOCK * k + j][...])
        ksum_ref[0, k] = s

    dtv = _softplus(dt_ref[...] + dtb_ref[...])
    da = dtv * (-jnp.exp(alog_ref[...]))
    row = lax.broadcasted_iota(jnp.int32, (t, t), 0)
    col = lax.broadcasted_iota(jnp.int32, (t, t), 1)
    causal = row >= col
    tril = causal.astype(BF16)
    d_hi, d_mid, d_lo = _split3(da)
    a2 = (_dot(tril, d_hi) + (_dot(tril, d_mid) + _dot(tril, d_lo))) * LOG2E
    arow_t = (a2 - jnp.log(dtv) * LOG2E).T

    lane = lax.broadcasted_iota(jnp.int32, (t, LANES), 1)
    a_hi, a_mid, a_lo = _split3(jnp.where(lane < SSD_HEADS, a2, 0.0))
    packed = (a_hi.astype(F32) + pltpu.roll(a_mid.astype(F32), SSD_HEADS, 1)
              + pltpu.roll(a_lo.astype(F32), 2 * SSD_HEADS, 1)).astype(BF16)
    acol_ref[...] = _dot(packed, sel_ref[...])

    lane_g = lax.broadcasted_iota(jnp.int32, (1, gw), 1) // hd
    for g in range(SSD_GROUPS):
        xs_g = act_ref[:, g * gw:(g + 1) * gw]
        b_g = act_ref[:, SSD_D_INNER + g * SSD_STATE:SSD_D_INNER + (g + 1) * SSD_STATE]
        c_g = act_ref[:, SSD_D_INNER + SSD_BC_DIM + g * SSD_STATE:
                      SSD_D_INNER + SSD_BC_DIM + (g + 1) * SSD_STATE]
        cb = jnp.where(causal, _dot_nt(c_g.astype(BF16), b_g.astype(BF16)), 0.0)
        b_gt = b_g.T
        st_g = st_ref[g]
        xs_b = xs_g.astype(BF16)
        for r in range(hpg):
            xbd_ref[g, r * t:(r + 1) * t, r * hd:(r + 1) * hd] = xs_b[:, r * hd:(r + 1) * hd]
        m_parts, d_parts, w_parts = [], [], []
        dec_g = jnp.zeros((1, gw), F32)
        for r in range(hpg):
            h = g * hpg + r
            acol = acol_ref[:, h * LANES:(h + 1) * LANES]
            arow = arow_t[h:h + 1, :]
            aend = acol[t - 1:t, :]
            m_parts.append((cb * jnp.exp2(jnp.minimum(acol - arow, EXP2_CAP))).astype(BF16))
            d_parts.append((c_g * jnp.exp2(acol)).astype(BF16))
            w_parts.append((b_gt * jnp.exp2(aend - arow)).astype(BF16))
            dec_g = jnp.where(lane_g == r, jnp.exp2(jnp.concatenate([aend, aend], axis=1)), dec_g)
        x_bd = xbd_ref[g]
        y_g = (_dot(jnp.concatenate(m_parts, axis=1), x_bd) + _dot(jnp.concatenate(d_parts, axis=1), sbd_ref[g])
               + xs_g * dskip_ref[:, g * gw:(g + 1) * gw])
        st_new = st_g * dec_g + _dot(jnp.concatenate(w_parts, axis=1), x_bd)
        st_ref[g] = st_new
        st_b = st_new.astype(BF16)
        for r in range(hpg):
            sbd_ref[g, r * SSD_STATE:(r + 1) * SSD_STATE, r * hd:(r + 1) * hd] = st_b[:, r * hd:(r + 1) * hd]
        yz = y_g * _silu(z_ref[:, g * gw:(g + 1) * gw])
        y_ref[:, g * gw:(g + 1) * gw] = _rms(yz, ng_ref[:, g * gw:(g + 1) * gw]).astype(y_ref.dtype)

    @pl.when(c == pl.num_programs(1) - 1)
    def _():
        for g in range(SSD_GROUPS):
            state_ref[0, g * hpg:(g + 1) * hpg] = st_ref[g].T.reshape(hpg, SSD_HEAD_DIM, SSD_STATE)


def ssd_prompt(z, xbc, dt, conv_w, conv_b, dt_bias, a_log, d_skip, norm_g, batch, seq, pool, pool_layer, table):
    t = SSD_CHUNK
    nc = seq // t
    pad = LANES - SSD_HEADS
    n_seqs, n_pages = table.shape
    pps = (n_seqs * n_pages) // (batch * nc)
    assert pps * batch * nc == n_seqs * n_pages and pps % PAGES_PER_BLOCK == 0 and n_pages % pps == 0
    gps = n_pages // pps
    page_shape = pool.shape[2:]
    rowblk = lambda b, c, tbl: (b * nc + c, 0)
    fixed = lambda b, c, tbl: (0, 0)

    def page_spec(k):
        def index(b, c, tbl):
            step = b * nc + c
            return (pool_layer, tbl[(step // gps) * n_pages + (step % gps) * pps + k], 0, 0, 0)
        return pl.BlockSpec((None, None) + page_shape, index)

    y, conv_tail, state, key_sums = pl.pallas_call(
        _ssd_chunk_kernel,
        grid_spec=pltpu.PrefetchScalarGridSpec(
            num_scalar_prefetch=1,
            grid=(batch, nc),
            in_specs=[pl.BlockSpec((t, SSD_D_INNER), rowblk),
                      pl.BlockSpec((t, SSD_CONV_DIM), rowblk),
                      pl.BlockSpec((t, LANES), rowblk),
                      pl.BlockSpec((CONV_W, SSD_CONV_DIM), fixed),
                      pl.BlockSpec((1, SSD_CONV_DIM), fixed),
                      pl.BlockSpec((1, LANES), fixed),
                      pl.BlockSpec((1, LANES), fixed),
                      pl.BlockSpec((1, SSD_D_INNER), fixed),
                      pl.BlockSpec((1, SSD_D_INNER), fixed),
                      pl.BlockSpec((LANES, SSD_HEADS * LANES), fixed)]
                     + [page_spec(k) for k in range(pps)],
            out_specs=[pl.BlockSpec((t, SSD_D_INNER), rowblk),
                       pl.BlockSpec((1, SUBLANES, SSD_CONV_DIM), lambda b, c, tbl: (b, 0, 0)),
                       pl.BlockSpec((1, SSD_HEADS, SSD_HEAD_DIM, SSD_STATE), lambda b, c, tbl: (b, 0, 0, 0)),
                       pl.BlockSpec((1, pps // PAGES_PER_BLOCK) + page_shape[1:],
                                    lambda b, c, tbl: ((b * nc + c) // gps, (b * nc + c) % gps, 0, 0))],
            scratch_shapes=[pltpu.VMEM((SSD_CONV_DIM // LANES, SUBLANES + t, LANES), F32),
                            pltpu.VMEM((t, SSD_CONV_DIM), F32),
                            pltpu.VMEM((t, SSD_HEADS * LANES), F32),
                            pltpu.VMEM((SSD_GROUPS, SSD_STATE, SSD_GROUP_W), F32),
                            pltpu.VMEM((SSD_GROUPS, SSD_HEADS_PER_GROUP * t, SSD_GROUP_W), BF16),
                            pltpu.VMEM((SSD_GROUPS, SSD_HEADS_PER_GROUP * SSD_STATE, SSD_GROUP_W), BF16)]),
        out_shape=[jax.ShapeDtypeStruct((batch * seq, SSD_D_INNER), BF16),
                   jax.ShapeDtypeStruct((batch, SUBLANES, SSD_CONV_DIM), F32),
                   jax.ShapeDtypeStruct((batch, SSD_HEADS, SSD_HEAD_DIM, SSD_STATE), F32),
                   jax.ShapeDtypeStruct((n_seqs, n_pages // PAGES_PER_BLOCK) + page_shape[1:], F32)],
        compiler_params=_cparams("parallel", "arbitrary"),
        name="ssd_prompt",
    )(table.reshape(-1), z, xbc, dt, conv_w, conv_b.reshape(1, -1), jnp.pad(dt_bias, (0, pad)).reshape(1, LANES),
      jnp.pad(a_log, (0, pad)).reshape(1, LANES), jnp.repeat(d_skip, SSD_HEAD_DIM).reshape(1, -1),
      norm_g.reshape(1, -1), _ssd_head_select(), *([pool] * pps))
    return y, conv_tail[:, SUBLANES - (CONV_W - 1):, :], state, key_sums


MOBA_HEADS_PER_STEP = 4


def _col_reduce(x, pair_op, reduce_fn):
    n = x.shape[0]
    while n > 8 * SUBLANES and n % 2 == 0:
        n //= 2
        x = pair_op(x[:n], x[n:])
    return reduce_fn(x, axis=0, keepdims=True)


def _moba_prompt_kernel(q_ref, k_ref, v_ref, o_ref, kb_ref, vt_ref, sel_ref, s_ref, p_ref):
    qi = pl.program_id(2)
    blk = MOBA_BLOCK
    seq = k_ref.shape[0]
    nb = seq // blk
    nbp = sel_ref.shape[2]
    hd = MOBA_HEAD_DIM
    heads = q_ref.shape[1] // hd

    def prepare(hh):
        cols = slice(hh * hd, (hh + 1) * hd)
        means = []
        for j in range(nb):
            kj = k_ref[j * blk:(j + 1) * blk, cols]
            means.append(jnp.sum(kj, axis=0, keepdims=True) * (1.0 / blk))
            kb_ref[hh, j * blk:(j + 1) * blk, :] = kj.astype(BF16)
        for j in range(seq // LANES):
            vt_ref[hh, :, j * LANES:(j + 1) * LANES] = v_ref[j * LANES:(j + 1) * LANES, cols].T.astype(BF16)
        if nbp > nb:
            means.append(jnp.zeros((nbp - nb, hd), F32))
        kmean = jnp.concatenate(means, axis=0)
        sub = lax.broadcasted_iota(jnp.int32, (nbp, blk), 0)
        for i in range(nb):
            gm = jnp.where(sub < i, _dot3_nt(kmean, q_ref[i * blk:(i + 1) * blk, cols]), -jnp.inf)
            rank = jnp.zeros((nbp, blk), F32)
            for m in range(i):
                gm_m = gm[m:m + 1, :]
                beats = (gm_m > gm) | ((gm_m == gm) & (sub > m))
                rank = rank + jnp.where(beats, 1.0, 0.0)
            chosen = ((sub < i) & (rank < MOBA_TOPK)) | (sub == i)
            sel_ref[hh, i] = jnp.where(chosen, 1.0, 0.0)

    @pl.when(qi == 0)
    def _():
        for hh in range(heads):
            prepare(hh)

    causal = (lax.broadcasted_iota(jnp.int32, (blk, blk), 0)
              <= lax.broadcasted_iota(jnp.int32, (blk, blk), 1))

    def attend_head(hh, i):
        cols = slice(hh * hd, (hh + 1) * hd)
        qb = (q_ref[i * blk:(i + 1) * blk, cols] * (hd ** -0.5)).astype(BF16)
        sel = sel_ref[hh, i]
        n_keys = (i + 1) * blk
        m = None
        for j in range(i + 1):
            rows = slice(j * blk, (j + 1) * blk)
            s = _dot_nt(kb_ref[hh, rows, :], qb)
            if j < i:
                s = s + jnp.where(sel[j:j + 1, :] > 0.0, 0.0, NEG_BIG)
            else:
                s = jnp.where(causal, s, NEG_BIG)
            s_ref[hh, rows, :] = s
            bm = _col_reduce(s, jnp.maximum, jnp.max)
            m = bm if m is None else jnp.maximum(m, bm)
        l = jnp.zeros((1, blk), F32)
        for j in range(i + 1):
            rows = slice(j * blk, (j + 1) * blk)
            p = jnp.exp(s_ref[hh, rows, :] - m)
            l = l + _col_reduce(p, jnp.add, jnp.sum)
            p_ref[hh, rows, :] = p.astype(BF16)
        acc = _dot(vt_ref[hh, :, 0:n_keys], p_ref[hh, 0:n_keys, :])
        o_ref[:, cols] = (acc / l).T.astype(o_ref.dtype)

    def attend(i):
        for hh in range(heads):
            attend_head(hh, i)

    for i in range(nb):
        pl.when(qi == i)(functools.partial(attend, i))


def moba_prompt(q, k, v, batch, seq):
    nb = seq // MOBA_BLOCK
    hd = MOBA_HEAD_DIM
    hps = MOBA_HEADS_PER_STEP
    wide = lambda b, h, i: (b, h)
    return pl.pallas_call(
        _moba_prompt_kernel,
        grid=(batch, MOBA_HEADS // hps, nb),
        in_specs=[pl.BlockSpec((seq, hps * hd), wide),
                  pl.BlockSpec((seq, hps * hd), wide),
                  pl.BlockSpec((seq, hps * hd), wide)],
        out_specs=pl.BlockSpec((MOBA_BLOCK, hps * hd), lambda b, h, i: (b * nb + i, h)),
        out_shape=jax.ShapeDtypeStruct((batch * seq, D_MODEL), BF16),
        scratch_shapes=[pltpu.VMEM((hps, seq, hd), BF16),
                        pltpu.VMEM((hps, hd, seq), BF16),
                        pltpu.VMEM((hps, nb, -(-nb // SUBLANES) * SUBLANES, MOBA_BLOCK), F32),
                        pltpu.VMEM((hps, seq, MOBA_BLOCK), F32),
                        pltpu.VMEM((hps, seq, MOBA_BLOCK), BF16)],
        compiler_params=_cparams("parallel", "parallel", "arbitrary"),
        name="moba_prompt",
    )(q, k, v)


def _lru_gates(xc, wa, ba, wx, bx, lam):
    xb = xc.astype(BF16)
    r = _sigmoid(_dot(xb, wa) + ba)
    i = _sigmoid(_dot(xb, wx) + bx)
    log_a = (-LRU_C) * r * _softplus(-lam)
    a = jnp.exp(log_a)
    v = -jnp.tanh(log_a) * (a * a + 1.0)
    u = jnp.where(v > 0.0, v * lax.rsqrt(v), 0.0) * (i * xc)
    return a, u


def _lru_chunk_kernel(gate_ref, x_ref, cw_ref, cb_ref, wa_ref, ba_ref, wx_ref, bx_ref, lam_ref,
                      y_ref, conv_ref, hlast_ref, xp_ref, a_ref, u_ref, h_ref):
    c = pl.program_id(1)
    t = LRU_CHUNK
    w = LRU_BLOCK_W

    @pl.when(c == 0)
    def _():
        xp_ref[:, 0:SUBLANES, :] = jnp.zeros((LRU_BLOCKS, SUBLANES, w), F32)
        h_ref[...] = jnp.zeros_like(h_ref)

    for n in range(LRU_BLOCKS):
        cols = slice(n * w, (n + 1) * w)
        xp_ref[n, SUBLANES:SUBLANES + t, :] = x_ref[:, cols]
        acc = xp_ref[n, SUBLANES - 3:SUBLANES - 3 + t, :] * cw_ref[0:1, cols]
        for k in range(1, CONV_W):
            acc = acc + xp_ref[n, SUBLANES - 3 + k:SUBLANES - 3 + k + t, :] * cw_ref[k:k + 1, cols]
        xc = acc + cb_ref[:, cols]
        a, u = _lru_gates(xc, wa_ref[n], ba_ref[:, cols], wx_ref[n], bx_ref[:, cols], lam_ref[:, cols])
        a_ref[:, cols] = a
        u_ref[:, cols] = u
        xp_ref[n, 0:SUBLANES, :] = x_ref[t - SUBLANES:t, cols]
    conv_ref[0] = x_ref[t - SUBLANES:t, :]

    srow = lax.broadcasted_iota(jnp.int32, (SUBLANES, LRU_WIDTH), 0)

    def tile(i, h):
        rows = pl.ds(pl.multiple_of(i * SUBLANES, SUBLANES), SUBLANES)
        a = a_ref[rows, :]
        u = u_ref[rows, :]
        for d in (1, 2, 4):
            keep = srow >= d
            a_sh = jnp.where(keep, pltpu.roll(a, d, 0), 1.0)
            u_sh = jnp.where(keep, pltpu.roll(u, d, 0), 0.0)
            u = a * u_sh + u
            a = a * a_sh
        hs = a * h + u
        u_ref[rows, :] = hs
        return jnp.broadcast_to(hs[SUBLANES - 1:SUBLANES, :], (SUBLANES, LRU_WIDTH))

    h = lax.fori_loop(0, t // SUBLANES, tile, h_ref[...])
    h_ref[...] = h
    hlast_ref[0] = h
    y_ref[...] = (_gelu_tanh(gate_ref[...]) * u_ref[...]).astype(y_ref.dtype)


def lru_prompt(gate_br, x_br, conv_w, conv_b, w_a, b_a, w_x, b_x, lam, batch, seq):
    t = LRU_CHUNK
    nc = seq // t
    wd = LRU_WIDTH
    rowblk = lambda b, c: (b * nc + c, 0)
    fixed = lambda b, c: (0, 0)
    fixed3 = lambda b, c: (0, 0, 0)
    perb = lambda b, c: (b, 0, 0)
    y, conv_tail, h_last = pl.pallas_call(
        _lru_chunk_kernel,
        grid=(batch, nc),
        in_specs=[pl.BlockSpec((t, wd), rowblk),
                  pl.BlockSpec((t, wd), rowblk),
                  pl.BlockSpec((CONV_W, wd), fixed),
                  pl.BlockSpec((1, wd), fixed),
                  pl.BlockSpec((LRU_BLOCKS, LRU_BLOCK_W, LRU_BLOCK_W), fixed3),
                  pl.BlockSpec((1, wd), fixed),
                  pl.BlockSpec((LRU_BLOCKS, LRU_BLOCK_W, LRU_BLOCK_W), fixed3),
                  pl.BlockSpec((1, wd), fixed),
                  pl.BlockSpec((1, wd), fixed)],
        out_specs=[pl.BlockSpec((t, wd), rowblk),
                   pl.BlockSpec((1, SUBLANES, wd), perb),
                   pl.BlockSpec((1, SUBLANES, wd), perb)],
        out_shape=[jax.ShapeDtypeStruct((batch * seq, wd), BF16),
                   jax.ShapeDtypeStruct((batch, SUBLANES, wd), F32),
                   jax.ShapeDtypeStruct((batch, SUBLANES, wd), F32)],
        scratch_shapes=[pltpu.VMEM((LRU_BLOCKS, SUBLANES + t, LRU_BLOCK_W), F32),
                        pltpu.VMEM((t, wd), F32),
                        pltpu.VMEM((t, wd), F32),
                        pltpu.VMEM((SUBLANES, wd), F32)],
        compiler_params=_cparams("parallel", "arbitrary"),
        name="lru_prompt",
    )(gate_br, x_br, conv_w, conv_b.reshape(1, wd), w_a.astype(BF16), b_a.reshape(1, wd),
      w_x.astype(BF16), b_x.reshape(1, wd), lam.reshape(1, wd))
    return y, conv_tail[:, SUBLANES - (CONV_W - 1):, :], h_last[:, 0, :]


SSD_STEP_SEQS = 4


def _ssd_step_kernel(z_ref, xbc_ref, dt_ref, cs_ref, st_ref, cw_ref, cb_ref, dtb_ref, alog_ref, dskip_ref, ng_ref,
                     y_ref, cso_ref, sto_ref):
    for s in range(z_ref.shape[0]):
        _ssd_step_one(s, z_ref, xbc_ref, dt_ref, cs_ref, st_ref, cw_ref, cb_ref, dtb_ref, alog_ref, dskip_ref,
                      ng_ref, y_ref, cso_ref, sto_ref)


def _ssd_step_one(s, z_ref, xbc_ref, dt_ref, cs_ref, st_ref, cw_ref, cb_ref, dtb_ref, alog_ref, dskip_ref, ng_ref,
                  y_ref, cso_ref, sto_ref):
    gw = SSD_GROUP_W
    hpg = SSD_HEADS_PER_GROUP
    x = xbc_ref[s]
    buf = cs_ref[s]
    conv = x * cw_ref[CONV_W - 1:CONV_W, :]
    for k in range(CONV_W - 1):
        conv = conv + buf[k:k + 1, :] * cw_ref[k:k + 1, :]
    act = _silu(conv + cb_ref[...])
    cso_ref[s, 0:CONV_W - 2, :] = buf[1:CONV_W - 1, :]
    cso_ref[s, CONV_W - 2:CONV_W - 1, :] = x

    dtv = _softplus(dt_ref[s] + dtb_ref[...])
    dec = jnp.exp(dtv * (-jnp.exp(alog_ref[...])))
    lane_g = lax.broadcasted_iota(jnp.int32, (1, gw), 1) // SSD_HEAD_DIM
    first_row = lax.broadcasted_iota(jnp.int32, (SUBLANES, gw), 0) == 0
    ys = []
    for g in range(SSD_GROUPS):
        xs_g = act[:, g * gw:(g + 1) * gw]
        b_g = act[:, SSD_D_INNER + g * SSD_STATE:SSD_D_INNER + (g + 1) * SSD_STATE]
        c_g = act[:, SSD_D_INNER + SSD_BC_DIM + g * SSD_STATE:SSD_D_INNER + SSD_BC_DIM + (g + 1) * SSD_STATE]
        dt_e = jnp.zeros((1, gw), F32)
        dec_rows = []
        for r in range(hpg):
            h = g * hpg + r
            dt_e = jnp.where(lane_g == r, dtv[:, h:h + 1], dt_e)
            dec_rows.append(jnp.broadcast_to(dec[:, h:h + 1], (SSD_HEAD_DIM, SSD_STATE)))
        xdt = xs_g * dt_e
        x8 = jnp.where(first_row, jnp.broadcast_to(xdt, (SUBLANES, gw)), 0.0).astype(BF16)
        b8 = jnp.broadcast_to(b_g, (SUBLANES, SSD_STATE)).astype(BF16)
        outer = lax.dot_general(x8, b8, (((0,), (0,)), ((), ())), preferred_element_type=F32)
        old = st_ref[s, g * hpg:(g + 1) * hpg].reshape(gw, SSD_STATE)
        new = old * jnp.concatenate(dec_rows, axis=0) + outer
        sto_ref[s, g * hpg:(g + 1) * hpg] = new.reshape(hpg, SSD_HEAD_DIM, SSD_STATE)
        c8 = jnp.broadcast_to(c_g, (SUBLANES, SSD_STATE)).astype(BF16)
        y_g = _dot_nt(c8, new.astype(BF16))[0:1, :] + xs_g * dskip_ref[:, g * gw:(g + 1) * gw]
        yz = y_g * _silu(z_ref[s][:, g * gw:(g + 1) * gw])
        ys.append(_rms(yz, ng_ref[:, g * gw:(g + 1) * gw]))
    y_ref[s] = jnp.concatenate(ys, axis=1)


def ssd_step(z, xbc, dt, conv_state, ssm_state, layer, conv_w, conv_b, dt_bias, a_log, d_skip, norm_g):
    nb = z.shape[0]
    pad = LANES - SSD_HEADS
    per3 = lambda b: (b, 0, 0)
    fixed = lambda b: (0, 0)
    ns = SSD_STEP_SEQS if nb % SSD_STEP_SEQS == 0 else 1
    y, conv_new, state_new = pl.pallas_call(
        _ssd_step_kernel,
        grid=(nb // ns,),
        in_specs=[pl.BlockSpec((ns, 1, SSD_D_INNER), per3),
                  pl.BlockSpec((ns, 1, SSD_CONV_DIM), per3),
                  pl.BlockSpec((ns, 1, LANES), per3),
                  pl.BlockSpec((None, ns, CONV_W - 1, SSD_CONV_DIM), lambda b: (layer, b, 0, 0)),
                  pl.BlockSpec((None, ns, SSD_HEADS, SSD_HEAD_DIM, SSD_STATE), lambda b: (layer, b, 0, 0, 0)),
                  pl.BlockSpec((CONV_W, SSD_CONV_DIM), fixed),
                  pl.BlockSpec((1, SSD_CONV_DIM), fixed),
                  pl.BlockSpec((1, LANES), fixed),
                  pl.BlockSpec((1, LANES), fixed),
                  pl.BlockSpec((1, SSD_D_INNER), fixed),
                  pl.BlockSpec((1, SSD_D_INNER), fixed)],
        out_specs=[pl.BlockSpec((ns, 1, SSD_D_INNER), per3),
                   pl.BlockSpec((ns, CONV_W - 1, SSD_CONV_DIM), per3),
                   pl.BlockSpec((ns, SSD_HEADS, SSD_HEAD_DIM, SSD_STATE), lambda b: (b, 0, 0, 0))],
        out_shape=[jax.ShapeDtypeStruct((nb, 1, SSD_D_INNER), F32),
                   jax.ShapeDtypeStruct((nb, CONV_W - 1, SSD_CONV_DIM), F32),
                   jax.ShapeDtypeStruct((nb, SSD_HEADS, SSD_HEAD_DIM, SSD_STATE), F32)],
        compiler_params=_cparams("parallel"),
        name="ssd_step",
    )(z.reshape(nb, 1, -1), xbc.reshape(nb, 1, -1), dt.reshape(nb, 1, -1), conv_state, ssm_state,
      conv_w, conv_b.reshape(1, -1), jnp.pad(dt_bias, (0, pad)).reshape(1, LANES),
      jnp.pad(a_log, (0, pad)).reshape(1, LANES), jnp.repeat(d_skip, SSD_HEAD_DIM).reshape(1, -1),
      norm_g.reshape(1, -1))
    return y.reshape(nb, SSD_D_INNER), conv_new, state_new


def _lru_step_kernel(gate_ref, x_ref, cs_ref, h0_ref, cw_ref, cb_ref, wa_ref, ba_ref, wx_ref, bx_ref, lam_ref,
                     y_ref, cso_ref, ho_ref):
    w = LRU_BLOCK_W
    x = x_ref[...]
    xc = x * cw_ref[CONV_W - 1:CONV_W, :] + cb_ref[...]
    for k in range(CONV_W - 1):
        xc = xc + cs_ref[k] * cw_ref[k:k + 1, :]
    for k in range(CONV_W - 2):
        cso_ref[k] = cs_ref[k + 1]
    cso_ref[CONV_W - 2] = x
    for n in range(LRU_BLOCKS):
        cols = slice(n * w, (n + 1) * w)
        a, u = _lru_gates(xc[:, cols], wa_ref[n], ba_ref[:, cols], wx_ref[n], bx_ref[:, cols], lam_ref[:, cols])
        h = a * h0_ref[:, cols] + u
        ho_ref[:, cols] = h
        y_ref[:, cols] = _gelu_tanh(gate_ref[:, cols]) * h


def lru_step(gate_br, x_br, conv_state, h0, conv_w, conv_b, w_a, b_a, w_x, b_x, lam):
    nb, wd = x_br.shape
    return pl.pallas_call(
        _lru_step_kernel,
        out_shape=[jax.ShapeDtypeStruct((nb, wd), F32),
                   jax.ShapeDtypeStruct((CONV_W - 1, nb, wd), F32),
                   jax.ShapeDtypeStruct((nb, wd), F32)],
        compiler_params=pltpu.CompilerParams(vmem_limit_bytes=VMEM_LIMIT),
        name="lru_step",
    )(gate_br, x_br, conv_state, h0, conv_w, conv_b.reshape(1, wd), w_a.astype(BF16), b_a.reshape(1, wd),
      w_x.astype(BF16), b_x.reshape(1, wd), lam.reshape(1, wd))


def _moba_gate_kernel(q_ref, bs_ref, sel_ref):
    n_blocks = bs_ref.shape[1]
    q = q_ref[0]
    gates = [jnp.sum((bs_ref[0, n] * (1.0 / MOBA_BLOCK)) * q, axis=-1, keepdims=True) for n in range(n_blocks)]
    lane = lax.broadcasted_iota(jnp.int32, (MOBA_HEADS, LANES), 1)
    out = jnp.zeros((MOBA_HEADS, LANES), F32)
    for k in range(MOBA_TOPK):
        best = functools.reduce(jnp.maximum, gates)
        idx = jnp.full((MOBA_HEADS, 1), float(n_blocks), F32)
        for n in reversed(range(n_blocks)):
            idx = jnp.where(gates[n] == best, float(n), idx)
        out = jnp.where(lane == k, idx, out)
        gates = [jnp.where(idx == float(n), -jnp.inf, gates[n]) for n in range(n_blocks)]
    sel_ref[0] = out.astype(jnp.int32)


def moba_gate(q, block_sums):
    nb = q.shape[0]
    n_blocks = block_sums.shape[1]
    return pl.pallas_call(
        _moba_gate_kernel,
        grid=(nb,),
        in_specs=[pl.BlockSpec((1, MOBA_HEADS, MOBA_HEAD_DIM), lambda b: (b, 0, 0)),
                  pl.BlockSpec((1, n_blocks, MOBA_HEADS, MOBA_HEAD_DIM), lambda b: (b, 0, 0, 0))],
        out_specs=pl.BlockSpec((1, MOBA_HEADS, LANES), lambda b: (b, 0, 0)),
        out_shape=jax.ShapeDtypeStruct((nb, MOBA_HEADS, LANES), jnp.int32),
        compiler_params=_cparams("parallel"),
        name="moba_gate",
    )(q.reshape(nb, MOBA_HEADS, MOBA_HEAD_DIM), block_sums)


MOBA_DECODE_SLOTS = 3


def _moba_decode_kernel(tbl_ref, sel_ref, q_ref, kn_ref, vn_ref, kpool_ref, vpool_ref, o_ref, kbuf, vbuf, sem,
                        *, layer, n_table_pages):
    b = pl.program_id(0)
    nb = pl.num_programs(0)
    n_slots = kbuf.shape[0]
    slot = lax.rem(b, n_slots)
    n_sel = MOBA_TOPK * PAGES_PER_BLOCK
    hd = MOBA_HEAD_DIM

    def page_copies(bb, sl):
        out = []
        for hh in range(MOBA_HEADS):
            for k in range(MOBA_TOPK):
                blk = sel_ref[(bb * MOBA_HEADS + hh) * MOBA_TOPK + k]
                for s in range(PAGES_PER_BLOCK):
                    page = tbl_ref[bb * n_table_pages + blk * PAGES_PER_BLOCK + s]
                    i = k * PAGES_PER_BLOCK + s
                    out.append(pltpu.make_async_copy(kpool_ref.at[layer, page, :, hh, :], kbuf.at[sl, hh, i],
                                                     sem.at[sl, 0, hh, i]))
                    out.append(pltpu.make_async_copy(vpool_ref.at[layer, page, :, hh, :], vbuf.at[sl, hh, i],
                                                     sem.at[sl, 1, hh, i]))
        return out

    def start_all(copies):
        for i, c in enumerate(copies):
            c.start(priority=i % 2)

    @pl.when(b == 0)
    def _():
        for ahead in range(n_slots - 1):
            @pl.when(ahead < nb)
            def _():
                start_all(page_copies(ahead, ahead))

    @pl.when(b + n_slots - 1 < nb)
    def _():
        start_all(page_copies(b + n_slots - 1, lax.rem(b + n_slots - 1, n_slots)))

    for c in page_copies(b, slot):
        c.wait()

    scale = hd ** -0.5
    outs = []
    for hh in range(MOBA_HEADS):
        cols = slice(hh * hd, (hh + 1) * hd)
        q = q_ref[0][:, cols]
        q8 = jnp.broadcast_to(q, (SUBLANES, hd)).astype(BF16)
        k_cat = kbuf[slot, hh].reshape(n_sel * PAGE_SIZE, hd).astype(BF16)
        v_cat = vbuf[slot, hh].reshape(n_sel * PAGE_SIZE, hd).astype(BF16)
        s = _dot_nt(q8, k_cat) * scale
        s_new = jnp.sum(q * kn_ref[0][:, cols], axis=1, keepdims=True) * scale
        m = jnp.maximum(jnp.max(s, axis=1, keepdims=True), s_new)
        p = jnp.exp(s - m)
        p_new = jnp.exp(s_new - m)
        denom = jnp.sum(p, axis=1, keepdims=True) + p_new
        o = (_dot(p.astype(BF16), v_cat) + p_new * vn_ref[0][:, cols]) / denom
        outs.append(o[0:1, :])
    o_ref[0] = jnp.concatenate(outs, axis=1)


def moba_decode(q, k_new, v_new, k_pool, v_pool, layer, table, sel):
    nb = q.shape[0]
    n_pages = table.shape[1]
    hd = MOBA_HEAD_DIM
    n_sel = MOBA_TOPK * PAGES_PER_BLOCK
    tok = pl.BlockSpec((1, 1, D_MODEL), lambda b, tbl, sl: (b, 0, 0))
    hbm = pl.BlockSpec(memory_space=pl.ANY)
    out = pl.pallas_call(
        functools.partial(_moba_decode_kernel, layer=layer, n_table_pages=n_pages),
        grid_spec=pltpu.PrefetchScalarGridSpec(
            num_scalar_prefetch=2,
            grid=(nb,),
            in_specs=[tok, tok, tok, hbm, hbm],
            out_specs=tok,
            scratch_shapes=[pltpu.VMEM((MOBA_DECODE_SLOTS, MOBA_HEADS, n_sel, PAGE_SIZE, hd), F32),
                            pltpu.VMEM((MOBA_DECODE_SLOTS, MOBA_HEADS, n_sel, PAGE_SIZE, hd), F32),
                            pltpu.SemaphoreType.DMA((MOBA_DECODE_SLOTS, 2, MOBA_HEADS, n_sel))]),
        out_shape=jax.ShapeDtypeStruct((nb, 1, D_MODEL), F32),
        compiler_params=_cparams("arbitrary"),
        name="moba_decode",
    )(table.reshape(-1), sel[:, :, :MOBA_TOPK].reshape(-1),
      q.reshape(nb, 1, D_MODEL), k_new.reshape(nb, 1, D_MODEL), v_new.reshape(nb, 1, D_MODEL), k_pool, v_pool)
    return out.reshape(nb, D_MODEL)


def _prep_weights(W):
    bf = lambda a: a.astype(BF16)
    w_in, w_qkv, w_lru = lax.optimization_barrier((bf(W["ssd_w_in"]), bf(W["moba_w_qkv"]), bf(W["lru_w_in"])))
    zx = SSD_D_INNER + SSD_CONV_DIM
    return dict(
        ssd_w_z=w_in[:, :, :SSD_D_INNER],
        ssd_w_xbc=w_in[:, :, SSD_D_INNER:zx],
        ssd_w_dt=jnp.pad(w_in[:, :, zx:], ((0, 0), (0, 0), (0, LANES - SSD_HEADS))),
        ssd_w_out=bf(W["ssd_w_out"]),
        moba_w_q=w_qkv[:, :, :D_MODEL],
        moba_w_k=w_qkv[:, :, D_MODEL:2 * D_MODEL],
        moba_w_v=w_qkv[:, :, 2 * D_MODEL:],
        moba_w_o=bf(W["moba_w_o"]),
        lru_w_gate=w_lru[:, :, :LRU_WIDTH],
        lru_w_x=w_lru[:, :, LRU_WIDTH:],
        lru_w_out=bf(W["lru_w_out"]),
        mlp_w_up=bf(W["mlp_w_up"]), mlp_w_down=bf(W["mlp_w_down"]),
        ple_w_proj=bf(W["ple_w_proj"]), ple_w_gate=bf(W["ple_w_gate"]),
    )


def _trunk(h, p, W, Wb, tiles, ssd_fn, moba_fn, lru_fn):
    tm, tf = tiles
    ssd_out, moba_out, lru_out = [], [], []
    xn = None
    for i in range(DEPTH):
        j, kind = i // N_MIXERS, i % N_MIXERS
        if i == 0:
            proj = lambda *names: multi_linear(h, [Wb[n] for n in names], j, tm=tm, norm_g=W["norm_mix"][0])
        else:
            proj = lambda *names: multi_linear(xn, [Wb[n] for n in names], j, tm=tm)
        if kind == 0:
            y, conv, state = ssd_fn(j, *proj("ssd_w_z", "ssd_w_xbc", "ssd_w_dt"))
            ssd_out.append((conv, state))
            w_out = Wb["ssd_w_out"]
        elif kind == 1:
            q, k, v = proj("moba_w_q", "moba_w_k", "moba_w_v")
            moba_out.append((k, v))
            y, w_out = moba_fn(j, q, k, v), Wb["moba_w_o"]
        else:
            y, conv, state = lru_fn(j, *proj("lru_w_gate", "lru_w_x"))
            lru_out.append((conv, state))
            w_out = Wb["lru_w_out"]
        final = i == DEPTH - 1
        g_next = W["norm_final"] if final else W["norm_mix"][i + 1]
        out = layer_tail(h, y, w_out, j, W["norm_mlp"][i], Wb["mlp_w_up"], Wb["mlp_w_down"], p,
                         W["norm_ple"][i], Wb["ple_w_gate"], Wb["ple_w_proj"], i, g_next,
                         tm=tm, tf=tf, final=final)
        h, xn = (out, None) if final else out
    stack = lambda pairs, n: jnp.stack([pr[n] for pr in pairs])
    return (h, stack(ssd_out, 0), stack(ssd_out, 1), stack(moba_out, 0), stack(moba_out, 1),
            stack(lru_out, 0), stack(lru_out, 1))


PROMPT_TILES = (512, 1024)
SAMPLE_TILES = (32, 1024)


def kernel(x_prompt, x_sample, state_ssd_conv, state_ssd, cache_k, cache_v, page_table, state_lru_conv,
           state_lru, p_prompt, p_sample, norm_mix, norm_mlp, norm_ple, norm_final, ssd_w_in, ssd_conv_w,
           ssd_conv_b, ssd_dt_bias, ssd_a_log, ssd_d, ssd_norm, ssd_w_out, moba_w_qkv, moba_w_o, lru_w_in,
           lru_conv_w, lru_conv_b, lru_w_a, lru_b_a, lru_w_x, lru_b_x, lru_lambda, lru_w_out, mlp_w_up,
           mlp_w_down, ple_w_proj, ple_w_gate):
    W = dict(norm_mix=norm_mix, norm_mlp=norm_mlp, norm_ple=norm_ple, norm_final=norm_final,
             ssd_w_in=ssd_w_in, ssd_w_out=ssd_w_out, moba_w_qkv=moba_w_qkv, moba_w_o=moba_w_o,
             lru_w_in=lru_w_in, lru_w_out=lru_w_out, mlp_w_up=mlp_w_up, mlp_w_down=mlp_w_down,
             ple_w_proj=ple_w_proj, ple_w_gate=ple_w_gate)
    Wb = _prep_weights(W)
    batch, seq, d = x_prompt.shape
    dec_batch, dec_seq, _ = x_sample.shape
    assert dec_seq == 1 and seq % MOBA_BLOCK == 0 and seq % LRU_CHUNK == 0

    ssd_args = lambda j: (ssd_conv_w[j], ssd_conv_b[j], ssd_dt_bias[j], ssd_a_log[j], ssd_d[j], ssd_norm[j])
    lru_args = lambda j: (lru_conv_w[j], lru_conv_b[j], lru_w_a[j], lru_b_a[j].reshape(-1), lru_w_x[j],
                          lru_b_x[j].reshape(-1), lru_lambda[j])

    n_ssd_calls = ssd_w_in.shape[0]
    assert cache_k.shape[0] == 1 and dec_batch % n_ssd_calls == 0
    share = dec_batch // n_ssd_calls
    key_sums = []

    def ssd_prompt_fn(j, z, xbc, dt):
        y, conv, state, sums = ssd_prompt(z, xbc, dt, *ssd_args(j), batch, seq,
                                          cache_k, 0, page_table[j * share:(j + 1) * share])
        key_sums.append(sums)
        return y, conv, state

    out_p = _trunk(
        x_prompt.reshape(batch * seq, d), p_prompt.reshape(DEPTH, batch * seq, PLE_DIM), W, Wb, PROMPT_TILES,
        ssd_prompt_fn,
        lambda j, q, k, v: moba_prompt(q, k, v, batch, seq),
        lambda j, gate, x: lru_prompt(gate, x, *lru_args(j), batch, seq))

    def moba_sample(j, q, k, v):
        sel = moba_gate(q, jnp.concatenate(key_sums, axis=0))
        return moba_decode(q, k, v, cache_k, cache_v, j, page_table, sel)

    def lru_sample(j, gate, x):
        y, conv, h = lru_step(gate, x, jnp.swapaxes(state_lru_conv[j], 0, 1), state_lru[j], *lru_args(j))
        return y, jnp.swapaxes(conv, 0, 1), h

    out_s = _trunk(
        x_sample.reshape(dec_batch, d), p_sample.reshape(DEPTH, dec_batch, PLE_DIM), W, Wb, SAMPLE_TILES,
        lambda j, z, xbc, dt: ssd_step(z, xbc, dt, state_ssd_conv, state_ssd, j, *ssd_args(j)),
        moba_sample, lru_sample)

    def shaped(out, b, s):
        y, ssd_conv, ssd_state, k, v, lru_conv, lru_state = out
        kv_shape = (-1, b, s, MOBA_HEADS, MOBA_HEAD_DIM)
        return (y.reshape(b, s, d), ssd_conv, ssd_state, k.reshape(kv_shape), v.reshape(kv_shape),
                lru_conv, lru_state)

    yp, *rest_p = shaped(out_p, batch, seq)
    ys, *rest_s = shaped(out_s, dec_batch, dec_seq)
    return (yp, ys, *rest_p, *rest_s)
```
